```python
import jax, jax.numpy as jnp
from jax import lax
import numpy as np

D_MODEL = 2048
BATCH = 2
SEQ = 16384
DEPTH = 2
DEC_BATCH = 8
DEC_SEQ = 16
PAST_LEN = 4096

CHUNK = 64
N_MIXERS = 2
N_RET_LAYERS = (DEPTH + 1) // 2
N_RWKV_LAYERS = DEPTH // 2

RET_HEADS = 8
RET_DK = D_MODEL // RET_HEADS
RET_DV = 2 * RET_DK
RET_QK = RET_HEADS * RET_DK
RET_VDIM = RET_HEADS * RET_DV
ROPE_BASE = 10000.0

RWKV_HEAD = 64
RWKV_HEADS = D_MODEL // RWKV_HEAD
LORA_DECAY = 96
LORA_AAA = 96
LORA_GATE = 256

D_FF = 5632
CONV_W = 3

NORM_EPS = 1e-6
RET_GN_EPS = 1e-5
RWKV_GN_EPS = 64e-5

kernel_name = "retnet_rwkv7_convffn_adaln_stream_step"


def rms_norm(x, eps=NORM_EPS):
    xf = x.astype(jnp.float32)
    return (xf * lax.rsqrt(jnp.mean(xf * xf, -1, keepdims=True) + eps)).astype(x.dtype)


def head_norm(x, gain, eps):
    xf = x.astype(jnp.float32)
    mu = jnp.mean(xf, -1, keepdims=True)
    var = jnp.mean(jnp.square(xf - mu), -1, keepdims=True)
    y = ((xf - mu) * lax.rsqrt(var + eps)).reshape(x.shape[:-2] + (-1,))
    return y * gain.astype(jnp.float32)


def rotary(x, pos):
    half = x.shape[-1] // 2
    inv = ROPE_BASE ** (-jnp.arange(half, dtype=jnp.float32) / half)
    ang = pos.astype(jnp.float32)[:, None] * inv[None, :]
    cos = jnp.cos(ang)[None, :, None, :]
    sin = jnp.sin(ang)[None, :, None, :]
    x1 = x[..., :half].astype(jnp.float32)
    x2 = x[..., half:].astype(jnp.float32)
    return jnp.concatenate([x1 * cos - x2 * sin, x1 * sin + x2 * cos], -1).astype(x.dtype)


def retention_mixer(h, state, pos0, w_in, w_out, gn_gain):
    B, T, _ = h.shape
    proj = h @ w_in
    q, k, v, g = jnp.split(proj, [RET_QK, 2 * RET_QK, 2 * RET_QK + RET_VDIM], axis=-1)
    pos = pos0 + jnp.arange(T)
    q = rotary(q.reshape(B, T, RET_HEADS, RET_DK), pos)
    k = rotary(k.reshape(B, T, RET_HEADS, RET_DK), pos) * (RET_DK ** -0.5)
    v = v.reshape(B, T, RET_HEADS, RET_DV)
    L = min(CHUNK, T)
    n_chunks = T // L
    log_g = jnp.log1p(-(2.0 ** (-5.0 - jnp.arange(RET_HEADS, dtype=jnp.float32))))
    idx = jnp.arange(L, dtype=jnp.float32)
    diff = idx[:, None] - idx[None, :]
    intra = jnp.where(diff >= 0, jnp.exp(log_g[:, None, None] * jnp.maximum(diff, 0.0)), 0.0)
    cross = jnp.exp(log_g[:, None] * (idx[None, :] + 1.0))[None, :, :, None]
    into = jnp.exp(log_g[:, None] * (L - 1.0 - idx[None, :]))[None, :, :, None]
    chunk_decay = jnp.exp(log_g * L)[None, :, None, None]

    def to_chunks(t):
        return t.reshape(B, n_chunks, L, RET_HEADS, -1).transpose(1, 0, 3, 2, 4)

    def step(S, inp):
        qi, ki, vi = inp
        scores = jnp.einsum('bhld,bhmd->bhlm', qi, ki) * intra
        o = (jnp.einsum('bhlm,bhmv->bhlv', scores, vi)
             + jnp.einsum('bhld,bhdv->bhlv', qi, S) * cross)
        S = S * chunk_decay + jnp.einsum('bhmd,bhmv->bhdv', ki * into, vi)
        return S, o.astype(jnp.float32)

    S, o = lax.scan(step, state.astype(jnp.float32), (to_chunks(q), to_chunks(k), to_chunks(v)))
    o = o.transpose(1, 0, 3, 2, 4).reshape(B, T, RET_HEADS, RET_DV)
    y = head_norm(o, gn_gain, RET_GN_EPS) * jax.nn.silu(g.astype(jnp.float32))
    return (y.astype(h.dtype) @ w_out).astype(h.dtype), S


def rwkv7_mixer(h, shift_state, wkv_state, mu, w_r, w_k, w_v, w_o, w0, w1, w2,
                a0, a1, a2, g1, g2, k_k, k_a, r_k, gn_gain):
    B, T, _ = h.shape
    H, N = RWKV_HEADS, RWKV_HEAD
    prev = jnp.concatenate([shift_state[:, None, :].astype(h.dtype), h[:, :-1]], axis=1)
    xx = prev - h
    xr, xw, xk, xv, xa, xg = [h + xx * mu[n] for n in range(6)]
    r = xr @ w_r
    k = xk @ w_k
    v = xv @ w_v
    w_log = -jax.nn.softplus(-(w0 + jnp.tanh(xw @ w1) @ w2).astype(jnp.float32)) - 0.5
    decay = jnp.exp(-jnp.exp(w_log))
    a = jax.nn.sigmoid((a0 + (xa @ a1) @ a2).astype(jnp.float32))
    g = jax.nn.sigmoid(xg @ g1) @ g2
    kk = (k * k_k).astype(jnp.float32).reshape(B, T, H, N)
    kk = kk / jnp.maximum(jnp.sqrt(jnp.sum(kk * kk, -1, keepdims=True)), 1e-12)
    k = k.astype(jnp.float32) * (1.0 + (a - 1.0) * k_a.astype(jnp.float32))

    def heads(t):
        return t.astype(jnp.float32).reshape(B, T, H, N)

    r4, k4, v4, a4, d4 = heads(r), heads(k), heads(v), heads(a), heads(decay)
    b4 = kk * a4

    def tm(t):
        return t.transpose(1, 0, 2, 3)

    def step(S, inp):
        r_t, d_t, k_t, v_t, kk_t, b_t = inp
        sa = jnp.einsum('bhij,bhj->bhi', S, -kk_t)
        S = S * d_t[:, :, None, :] + sa[..., None] * b_t[:, :, None, :] + v_t[..., None] * k_t[:, :, None, :]
        return S, jnp.einsum('bhij,bhj->bhi', S, r_t)

    S, o = lax.scan(step, wkv_state.astype(jnp.float32),
                    (tm(r4), tm(d4), tm(k4), tm(v4), tm(kk), tm(b4)))
    o = o.transpose(1, 0, 2, 3)
    y = head_norm(o, gn_gain, RWKV_GN_EPS)
    bonus = jnp.sum(r4 * k4 * r_k.astype(jnp.float32), -1, keepdims=True) * v4
    y = (y + bonus.reshape(B, T, D_MODEL)) * g.astype(jnp.float32)
    return (y.astype(h.dtype) @ w_o).astype(h.dtype), S, h[:, -1]


def conv_ffn(h, conv_state, w_gate, w_up, conv_w, conv_b, w_down):
    T = h.shape[1]
    u = h @ w_gate
    up = h @ w_up
    ext = jnp.concatenate([conv_state.astype(u.dtype), u], axis=1)
    conv = conv_b + sum(ext[:, j:j + T] * conv_w[j] for j in range(CONV_W))
    y = (jax.nn.silu(conv) * up) @ w_down
    return y.astype(h.dtype), ext[:, -(CONV_W - 1):]


def run_group(x, c, pos0, st_ret, st_wkv, st_shift, st_conv, p):
    new_ret, new_wkv, new_shift, new_conv = [], [], [], []
    for i in range(DEPTH):
        mod = (jax.nn.silu(c) @ p["ada_w"][i] + p["ada_b"][i]).reshape(c.shape[0], 1, 6, D_MODEL)
        shift_m, scale_m, gate_m, shift_f, scale_f, gate_f = [mod[:, :, n] for n in range(6)]
        h = rms_norm(x) * (1.0 + scale_m) + shift_m
        j = i // N_MIXERS
        if i % N_MIXERS == 0:
            y, s_ret = retention_mixer(h, st_ret[j], pos0, p["ret_w_in"][j], p["ret_w_out"][j],
                                       p["ret_gn_gain"][j])
            new_ret.append(s_ret)
        else:
            y, s_wkv, s_shift = rwkv7_mixer(
                h, st_shift[j], st_wkv[j], p["rwkv_mu"][j], p["rwkv_w_r"][j], p["rwkv_w_k"][j],
                p["rwkv_w_v"][j], p["rwkv_w_o"][j], p["rwkv_w0"][j], p["rwkv_w1"][j], p["rwkv_w2"][j],
                p["rwkv_a0"][j], p["rwkv_a1"][j], p["rwkv_a2"][j], p["rwkv_g1"][j], p["rwkv_g2"][j],
                p["rwkv_k_k"][j], p["rwkv_k_a"][j], p["rwkv_r_k"][j], p["rwkv_gn_gain"][j])
            new_wkv.append(s_wkv)
            new_shift.append(s_shift)
        x = x + gate_m * y
        h = rms_norm(x) * (1.0 + scale_f) + shift_f
        y, s_conv = conv_ffn(h, st_conv[i], p["ffn_w_gate"][i], p["ffn_w_up"][i], p["ffn_conv_w"][i],
                             p["ffn_conv_b"][i], p["ffn_w_down"][i])
        new_conv.append(s_conv)
        x = x + gate_f * y
    out = rms_norm(x) * p["final_gain"]
    return out, jnp.stack(new_ret), jnp.stack(new_wkv), jnp.stack(new_shift), jnp.stack(new_conv)


def setup_inputs(seed: int = 0) -> dict:
    key = jax.random.key(seed)
    ks = iter(jax.random.split(key, 48))
    f32 = jnp.float32

    def nrm(shape, scale):
        return jax.random.normal(next(ks), shape, f32) * scale

    def uni(shape, lo, hi):
        return jax.random.uniform(next(ks), shape, f32, lo, hi)

    D, F = D_MODEL, D_FF
    R, W = N_RET_LAYERS, N_RWKV_LAYERS
    return {
        "x_prompt": nrm((BATCH, SEQ, D), 1.0),
        "x_sample": nrm((DEC_BATCH, DEC_SEQ, D), 1.0),
        "c_prompt": nrm((BATCH, D), 1.0),
        "c_sample": nrm((DEC_BATCH, D), 1.0),
        "state_ret": nrm((R, DEC_BATCH, RET_HEADS, RET_DK, RET_DV), 0.05),
        "state_rwkv_wkv": nrm((W, DEC_BATCH, RWKV_HEADS, RWKV_HEAD, RWKV_HEAD), 0.1),
        "state_rwkv_shift": nrm((W, DEC_BATCH, D), 1.0),
        "state_ffn_conv": nrm((DEPTH, DEC_BATCH, CONV_W - 1, F), 1.0),
        "ada_w": nrm((DEPTH, D, 6 * D), 0.5 * D ** -0.5),
        "ada_b": nrm((DEPTH, 6 * D), 0.02),
        "ret_w_in": nrm((R, D, 2 * RET_QK + 2 * RET_VDIM), D ** -0.5),
        "ret_w_out": nrm((R, RET_VDIM, D), RET_VDIM ** -0.5),
        "ret_gn_gain": 1.0 + nrm((R, RET_VDIM), 0.02),
        "rwkv_mu": uni((W, 6, D), 0.0, 1.0),
        "rwkv_w_r": nrm((W, D, D), D ** -0.5),
        "rwkv_w_k": nrm((W, D, D), D ** -0.5),
        "rwkv_w_v": nrm((W, D, D), D ** -0.5),
        "rwkv_w_o": nrm((W, D, D), D ** -0.5),
        "rwkv_w0": uni((W, D), -6.5, -1.5),
        "rwkv_w1": nrm((W, D, LORA_DECAY), D ** -0.5),
        "rwkv_w2": nrm((W, LORA_DECAY, D), 0.1 * LORA_DECAY ** -0.5),
        "rwkv_a0": nrm((W, D), 0.1),
        "rwkv_a1": nrm((W, D, LORA_AAA), D ** -0.5),
        "rwkv_a2": nrm((W, LORA_AAA, D), 0.3 * LORA_AAA ** -0.5),
        "rwkv_g1": nrm((W, D, LORA_GATE), D ** -0.5),
        "rwkv_g2": nrm((W, LORA_GATE, D), LORA_GATE ** -0.5),
        "rwkv_k_k": 0.85 + nrm((W, D), 0.02),
        "rwkv_k_a": 1.0 + nrm((W, D), 0.02),
        "rwkv_r_k": nrm((W, RWKV_HEADS, RWKV_HEAD), 0.1),
        "rwkv_gn_gain": 1.0 + nrm((W, D), 0.02),
        "ffn_w_gate": nrm((DEPTH, D, F), D ** -0.5),
        "ffn_w_up": nrm((DEPTH, D, F), D ** -0.5),
        "ffn_conv_w": nrm((DEPTH, CONV_W, F), CONV_W ** -0.5),
        "ffn_conv_b": nrm((DEPTH, F), 0.02),
        "ffn_w_down": nrm((DEPTH, F, D), F ** -0.5),
        "final_gain": 1.0 + nrm((D,), 0.02),
    }


def reference(x_prompt, x_sample, c_prompt, c_sample, state_ret, state_rwkv_wkv, state_rwkv_shift,
              state_ffn_conv, ada_w, ada_b, ret_w_in, ret_w_out, ret_gn_gain, rwkv_mu, rwkv_w_r,
              rwkv_w_k, rwkv_w_v, rwkv_w_o, rwkv_w0, rwkv_w1, rwkv_w2, rwkv_a0, rwkv_a1, rwkv_a2,
              rwkv_g1, rwkv_g2, rwkv_k_k, rwkv_k_a, rwkv_r_k, rwkv_gn_gain, ffn_w_gate, ffn_w_up,
              ffn_conv_w, ffn_conv_b, ffn_w_down, final_gain):
    p = dict(ada_w=ada_w, ada_b=ada_b, ret_w_in=ret_w_in, ret_w_out=ret_w_out, ret_gn_gain=ret_gn_gain,
             rwkv_mu=rwkv_mu, rwkv_w_r=rwkv_w_r, rwkv_w_k=rwkv_w_k, rwkv_w_v=rwkv_w_v, rwkv_w_o=rwkv_w_o,
             rwkv_w0=rwkv_w0, rwkv_w1=rwkv_w1, rwkv_w2=rwkv_w2, rwkv_a0=rwkv_a0, rwkv_a1=rwkv_a1,
             rwkv_a2=rwkv_a2, rwkv_g1=rwkv_g1, rwkv_g2=rwkv_g2, rwkv_k_k=rwkv_k_k, rwkv_k_a=rwkv_k_a,
             rwkv_r_k=rwkv_r_k, rwkv_gn_gain=rwkv_gn_gain, ffn_w_gate=ffn_w_gate, ffn_w_up=ffn_w_up,
             ffn_conv_w=ffn_conv_w, ffn_conv_b=ffn_conv_b, ffn_w_down=ffn_w_down, final_gain=final_gain)
    z_ret = jnp.zeros((N_RET_LAYERS, BATCH, RET_HEADS, RET_DK, RET_DV), jnp.float32)
    z_wkv = jnp.zeros((N_RWKV_LAYERS, BATCH, RWKV_HEADS, RWKV_HEAD, RWKV_HEAD), jnp.float32)
    z_shift = jnp.zeros((N_RWKV_LAYERS, BATCH, D_MODEL), x_prompt.dtype)
    z_conv = jnp.zeros((DEPTH, BATCH, CONV_W - 1, D_FF), x_prompt.dtype)
    y_prompt, p_ret, p_wkv, p_shift, p_conv = run_group(
        x_prompt, c_prompt, 0, z_ret, z_wkv, z_shift, z_conv, p)
    y_sample, s_ret, s_wkv, s_shift, s_conv = run_group(
        x_sample, c_sample, PAST_LEN, state_ret, state_rwkv_wkv, state_rwkv_shift, state_ffn_conv, p)
    return (y_prompt, y_sample, p_ret, p_wkv, p_shift, p_conv, s_ret, s_wkv, s_shift, s_conv)
```

```python
import functools

import jax
import jax.numpy as jnp
from jax import lax
from jax.experimental import pallas as pl
from jax.experimental.pallas import tpu as pltpu

F32 = jnp.float32
BF16 = jnp.bfloat16

NORM_EPS = 1e-6
RET_GN_EPS = 1e-5
RWKV_GN_EPS = 64e-5
ROPE_BASE = 10000.0
PAST_LEN = 4096
CONV_W = 3

LANE = 128
SUBLANE = 8
VMEM_LIMIT_MB = 56

RET_CHUNK = 256
RWKV_CHUNK = 64
RWKV_HEAD = 64
RWKV_TBLOCK = 512


def _pick(n, cands):
    for c in cands:
        if n % c == 0:
            return c
    return n


def _cparams(sem):
    return pltpu.CompilerParams(dimension_semantics=sem, vmem_limit_bytes=VMEM_LIMIT_MB << 20)


def _token_tiles(B, T):
    if T >= 512:
        return 1, _pick(T, (512, 256, 128))
    return B, T


def _silu(x):
    return x * jax.nn.sigmoid(x)


def _ada_body(c_ref, w_ref, b_ref, o_ref):
    c = c_ref[...]
    s = _silu(c).astype(BF16)
    o_ref[0] = jnp.dot(s, w_ref[0].astype(BF16), preferred_element_type=F32) + b_ref[0]


def _ada(c_all, ada_w, ada_b):
    depth, D, N = ada_w.shape
    R = c_all.shape[0]
    tn = _pick(N, (1024, 512, 256, 128))
    return pl.pallas_call(
        _ada_body,
        grid=(depth, N // tn),
        in_specs=[pl.BlockSpec((R, D), lambda l, j: (0, 0)),
                  pl.BlockSpec((1, D, tn), lambda l, j: (l, 0, j)),
                  pl.BlockSpec((1, 1, tn), lambda l, j: (l, 0, j))],
        out_specs=pl.BlockSpec((1, R, tn), lambda l, j: (l, 0, j)),
        out_shape=jax.ShapeDtypeStruct((depth, R, N), F32),
        compiler_params=_cparams(("parallel", "parallel")),
        name="ada_mod",
    )(c_all, ada_w, ada_b.reshape(depth, 1, N))


def _norm_mod_body(x_ref, sh_ref, sc_ref, o_ref):
    x = x_ref[...]
    xn = x * lax.rsqrt(jnp.mean(x * x, -1, keepdims=True) + NORM_EPS)
    o_ref[...] = (xn * (1.0 + sc_ref[...]) + sh_ref[...]).astype(o_ref.dtype)


def _norm_gain_body(x_ref, g_ref, o_ref):
    x = x_ref[...]
    xn = x * lax.rsqrt(jnp.mean(x * x, -1, keepdims=True) + NORM_EPS)
    o_ref[...] = (xn * g_ref[...]).astype(o_ref.dtype)


def _norm_mod(x, shift, scale, out_dtype):
    B, T, D = x.shape
    bb, tm = _token_tiles(B, T)
    xs = pl.BlockSpec((bb, tm, D), lambda b, t: (b, t, 0))
    ms = pl.BlockSpec((bb, 1, D), lambda b, t: (b, 0, 0))
    return pl.pallas_call(
        _norm_mod_body, grid=(B // bb, T // tm), in_specs=[xs, ms, ms], out_specs=xs,
        out_shape=jax.ShapeDtypeStruct((B, T, D), out_dtype),
        compiler_params=_cparams(("parallel", "parallel")), name="norm_mod",
    )(x, shift, scale)


def _norm_gain(x, gain):
    B, T, D = x.shape
    bb, tm = _token_tiles(B, T)
    xs = pl.BlockSpec((bb, tm, D), lambda b, t: (b, t, 0))
    return pl.pallas_call(
        _norm_gain_body, grid=(B // bb, T // tm),
        in_specs=[xs, pl.BlockSpec((1, 1, D), lambda b, t: (0, 0, 0))], out_specs=xs,
        out_shape=jax.ShapeDtypeStruct((B, T, D), F32),
        compiler_params=_cparams(("parallel", "parallel")), name="final_norm",
    )(x, gain.reshape(1, 1, D))


def _mm_body(x_ref, w_ref, o_ref, *, act):
    bb, tm, K = x_ref.shape
    acc = jnp.dot(x_ref[...].reshape(bb * tm, K), w_ref[...], preferred_element_type=F32)
    if act == "tanh":
        acc = jnp.tanh(acc)
    elif act == "sigmoid":
        acc = jax.nn.sigmoid(acc)
    o_ref[...] = acc.reshape(bb, tm, -1).astype(o_ref.dtype)


def _mm(x, w, out_dtype, act=None, name="mm"):
    B, T, K = x.shape
    N = w.shape[1]
    bb, tm = _token_tiles(B, T)
    tn = _pick(N, (1024, 512, 256, 128))
    return pl.pallas_call(
        functools.partial(_mm_body, act=act),
        grid=(B // bb, T // tm, N // tn),
        in_specs=[pl.BlockSpec((bb, tm, K), lambda b, t, j: (b, t, 0)),
                  pl.BlockSpec((K, tn), lambda b, t, j: (0, j))],
        out_specs=pl.BlockSpec((bb, tm, tn), lambda b, t, j: (b, t, j)),
        out_shape=jax.ShapeDtypeStruct((B, T, N), out_dtype),
        compiler_params=_cparams(("parallel", "parallel", "parallel")), name=name,
    )(x, w)


def _mm_res_body(y_ref, w_ref, x_ref, gate_ref, o_ref):
    bb, tm, K = y_ref.shape
    acc = jnp.dot(y_ref[...].reshape(bb * tm, K), w_ref[...], preferred_element_type=F32)
    o_ref[...] = x_ref[...] + gate_ref[...] * acc.reshape(bb, tm, -1)


def _mm_res(y, w, x, gate, name="mm_res"):
    B, T, K = y.shape
    N = w.shape[1]
    bb, tm = _token_tiles(B, T)
    tn = _pick(N, (512, 256, 128))
    return pl.pallas_call(
        _mm_res_body,
        grid=(B // bb, T // tm, N // tn),
        in_specs=[pl.BlockSpec((bb, tm, K), lambda b, t, j: (b, t, 0)),
                  pl.BlockSpec((K, tn), lambda b, t, j: (0, j)),
                  pl.BlockSpec((bb, tm, tn), lambda b, t, j: (b, t, j)),
                  pl.BlockSpec((bb, 1, tn), lambda b, t, j: (b, 0, j))],
        out_specs=pl.BlockSpec((bb, tm, tn), lambda b, t, j: (b, t, j)),
        out_shape=jax.ShapeDtypeStruct((B, T, N), F32),
        compiler_params=_cparams(("parallel", "parallel", "parallel")), name=name,
    )(y, w, x, gate)


def _ret_proj_body(h_ref, w_ref, cos_ref, sin_ref, o_ref, *, n_rot, n_q, dk):
    j = pl.program_id(2)
    bb, tm, K = h_ref.shape
    tn = w_ref.shape[1]
    acc = jnp.dot(h_ref[...].reshape(bb * tm, K), w_ref[...], preferred_element_type=F32)
    acc = acc.reshape(bb, tm, tn)

    @pl.when(j < n_rot)
    def _():
        cos = cos_ref[...][None]
        sin = sin_ref[...][None]
        sc = jnp.where(j >= n_q, dk ** -0.5, 1.0).astype(F32)
        half = dk // 2
        for hh in range(tn // dk):
            lo = hh * dk
            x1 = acc[:, :, lo:lo + half]
            x2 = acc[:, :, lo + half:lo + dk]
            o_ref[:, :, lo:lo + half] = ((x1 * cos - x2 * sin) * sc).astype(o_ref.dtype)
            o_ref[:, :, lo + half:lo + dk] = ((x1 * sin + x2 * cos) * sc).astype(o_ref.dtype)

    @pl.when(j >= n_rot)
    def _():
        o_ref[...] = acc.astype(o_ref.dtype)


def _ret_proj(h, w_in, cos, sin, qk_dim, dk):
    B, T, K = h.shape
    N = w_in.shape[1]
    bb, tm = _token_tiles(B, T)
    tn = _pick(qk_dim, (1024, 512, 256))
    half = dk // 2
    return pl.pallas_call(
        functools.partial(_ret_proj_body, n_rot=2 * qk_dim // tn, n_q=qk_dim // tn, dk=dk),
        grid=(B // bb, T // tm, N // tn),
        in_specs=[pl.BlockSpec((bb, tm, K), lambda b, t, j: (b, t, 0)),
                  pl.BlockSpec((K, tn), lambda b, t, j: (0, j)),
                  pl.BlockSpec((tm, half), lambda b, t, j: (t, 0)),
                  pl.BlockSpec((tm, half), lambda b, t, j: (t, 0))],
        out_specs=pl.BlockSpec((bb, tm, tn), lambda b, t, j: (b, t, j)),
        out_shape=jax.ShapeDtypeStruct((B, T, N), BF16),
        compiler_params=_cparams(("parallel", "parallel", "parallel")), name="ret_proj",
    )(h, w_in, cos, sin)


def _ret_body(q_ref, k_ref, v_ref, g_ref, gain_ref, intra_ref, cross_ref, into_ref, cd_ref, s0_ref,
              y_ref, s_ref):
    c = pl.program_id(2)

    @pl.when(c == 0)
    def _():
        s_ref[...] = s0_ref[...]

    q = q_ref[0]
    k = k_ref[0]
    v = v_ref[0]
    S = s_ref[0, 0]
    scores = lax.dot_general(q, k, (((1,), (1,)), ((), ())), preferred_element_type=F32) * intra_ref[0]
    o = (jnp.dot(scores.astype(BF16), v, preferred_element_type=F32)
         + jnp.dot(q, S.astype(BF16), preferred_element_type=F32) * cross_ref[0])
    kin = (k.astype(F32) * into_ref[0]).astype(BF16)
    s_ref[0, 0] = S * cd_ref[0] + lax.dot_general(kin, v, (((0,), (0,)), ((), ())),
                                                  preferred_element_type=F32)
    mu = jnp.mean(o, -1, keepdims=True)
    d = o - mu
    var = jnp.mean(d * d, -1, keepdims=True)
    g = g_ref[0].astype(F32)
    y_ref[0] = (d * lax.rsqrt(var + RET_GN_EPS) * gain_ref[...] * _silu(g)).astype(y_ref.dtype)


def _retention(proj, state, gain, H, dk, dv, n_valid):
    B, T, _ = proj.shape
    L = min(RET_CHUNK, T)
    nv = min(n_valid, L)
    log_g = jnp.log1p(-(2.0 ** (-5.0 - jnp.arange(H, dtype=F32))))
    idx = jnp.arange(L, dtype=F32)
    diff = idx[:, None] - idx[None, :]
    intra = jnp.where(diff >= 0, jnp.exp(log_g[:, None, None] * jnp.maximum(diff, 0.0)), 0.0)
    cross = jnp.exp(log_g[:, None] * (idx[None, :] + 1.0))[:, :, None]
    into = jnp.exp(log_g[:, None] * (nv - 1.0 - idx[None, :]))[:, :, None]
    cd = jnp.exp(log_g * nv)[:, None, None]
    kq = (H * dk) // dk
    vq = (2 * H * dk) // dv
    return pl.pallas_call(
        _ret_body,
        grid=(B, H, T // L),
        in_specs=[pl.BlockSpec((1, L, dk), lambda b, h, c: (b, c, h)),
                  pl.BlockSpec((1, L, dk), lambda b, h, c: (b, c, kq + h)),
                  pl.BlockSpec((1, L, dv), lambda b, h, c: (b, c, vq + h)),
                  pl.BlockSpec((1, L, dv), lambda b, h, c: (b, c, vq + H + h)),
                  pl.BlockSpec((1, dv), lambda b, h, c: (0, h)),
                  pl.BlockSpec((1, L, L), lambda b, h, c: (h, 0, 0)),
                  pl.BlockSpec((1, L, 1), lambda b, h, c: (h, 0, 0)),
                  pl.BlockSpec((1, L, 1), lambda b, h, c: (h, 0, 0)),
                  pl.BlockSpec((1, 1, 1), lambda b, h, c: (h, 0, 0)),
                  pl.BlockSpec((1, 1, dk, dv), lambda b, h, c: (b, h, 0, 0))],
        out_specs=[pl.BlockSpec((1, L, dv), lambda b, h, c: (b, c, h)),
                   pl.BlockSpec((1, 1, dk, dv), lambda b, h, c: (b, h, 0, 0))],
        out_shape=[jax.ShapeDtypeStruct((B, T, H * dv), BF16),
                   jax.ShapeDtypeStruct((B, H, dk, dv), F32)],
        compiler_params=_cparams(("parallel", "parallel", "arbitrary")), name="retention",
    )(proj, proj, proj, proj, gain.reshape(1, H * dv), intra, cross, into, cd, state)


def _ffn_body(h_ref, x_ref, gate_ref, wg_ref, wu_ref, wd_ref, cw_ref, cb_ref, cs_ref,
              o_ref, tail_ref, acc_ref, ext_ref):
    t = pl.program_id(1)
    f = pl.program_id(2)
    nf = pl.num_programs(2)
    bb, tm, D = h_ref.shape
    tf = wg_ref.shape[1]
    pad = SUBLANE

    h = h_ref[...].reshape(bb * tm, D)
    u = jnp.dot(h, wg_ref[...], preferred_element_type=F32).reshape(bb, tm, tf)
    up = jnp.dot(h, wu_ref[...], preferred_element_type=F32).reshape(bb, tm, tf)

    @pl.when(t == 0)
    def _():
        tail_ref[f] = cs_ref[...]

    ext_ref[:, pad - 2:pad, :] = tail_ref[f]
    ext_ref[:, pad:, :] = u
    cw = cw_ref[...]
    conv = (cb_ref[...] + ext_ref[:, pad - 2:pad - 2 + tm, :] * cw[0:1]
            + ext_ref[:, pad - 1:pad - 1 + tm, :] * cw[1:2] + u * cw[2:3])
    new_tail = ext_ref[:, pad + tm - 2:pad + tm, :]
    tail_ref[f] = new_tail
    act = (_silu(conv) * up).reshape(bb * tm, tf).astype(BF16)
    part = jnp.dot(act, wd_ref[...], preferred_element_type=F32)

    @pl.when(f == 0)
    def _():
        acc_ref[...] = part

    @pl.when(f > 0)
    def _():
        acc_ref[...] += part

    @pl.when(f == nf - 1)
    def _():
        o_ref[...] = x_ref[...] + gate_ref[...] * acc_ref[...].reshape(bb, tm, D)


def _ffn(h, x, gate, wg, wu, wd, conv_w, conv_b, conv_state):
    B, T, D = x.shape
    Fd = wg.shape[1]
    bb, tm = _token_tiles(B, T)
    tf = _pick(Fd, (512, 256, 128))
    nf = Fd // tf
    xs = pl.BlockSpec((bb, tm, D), lambda b, t, f: (b, t, 0))
    y, tails = pl.pallas_call(
        _ffn_body,
        grid=(B // bb, T // tm, nf),
        in_specs=[xs, xs,
                  pl.BlockSpec((bb, 1, D), lambda b, t, f: (b, 0, 0)),
                  pl.BlockSpec((D, tf), lambda b, t, f: (0, f)),
                  pl.BlockSpec((D, tf), lambda b, t, f: (0, f)),
                  pl.BlockSpec((tf, D), lambda b, t, f: (f, 0)),
                  pl.BlockSpec((CONV_W, tf), lambda b, t, f: (0, f)),
                  pl.BlockSpec((1, tf), lambda b, t, f: (0, f)),
                  pl.BlockSpec((bb, CONV_W - 1, tf), lambda b, t, f: (b, 0, f))],
        out_specs=[xs, pl.BlockSpec((nf, bb, CONV_W - 1, tf), lambda b, t, f: (0, b, 0, 0))],
        out_shape=[jax.ShapeDtypeStruct((B, T, D), F32),
                   jax.ShapeDtypeStruct((nf, B, CONV_W - 1, tf), F32)],
        scratch_shapes=[pltpu.VMEM((bb * tm, D), F32),
                        pltpu.VMEM((bb, tm + SUBLANE, tf), F32)],
        compiler_params=_cparams(("parallel", "arbitrary", "arbitrary")), name="conv_ffn",
    )(h, x, gate, wg, wu, wd, conv_w, conv_b.reshape(1, Fd), conv_state)
    return y, tails.transpose(1, 2, 0, 3).reshape(B, CONV_W - 1, Fd)


def _rwkv_mix_body(x_ref, sh_ref, sc_ref, mu_ref, st_ref,
                   xr_ref, xw_ref, xk_ref, xv_ref, xa_ref, xg_ref, last_ref, hbuf_ref, carry_ref):
    t = pl.program_id(1)
    tm = x_ref.shape[1]
    pad = SUBLANE
    x = x_ref[0]
    xn = x * lax.rsqrt(jnp.mean(x * x, -1, keepdims=True) + NORM_EPS)
    h = xn * (1.0 + sc_ref[0]) + sh_ref[0]

    @pl.when(t == 0)
    def _():
        carry_ref[...] = st_ref[0]

    hbuf_ref[pad - 1:pad, :] = carry_ref[...]
    hbuf_ref[pad:, :] = h
    xx = hbuf_ref[pad - 1:pad - 1 + tm, :] - h
    for n, ref in enumerate((xr_ref, xw_ref, xk_ref, xv_ref, xa_ref, xg_ref)):
        ref[0] = (h + xx * mu_ref[n:n + 1, :]).astype(ref.dtype)
    last = hbuf_ref[pad + tm - 1:pad + tm, :]
    carry_ref[...] = last
    last_ref[0] = last


def _rwkv_mix(x, shift, scale, mu, shift_state):
    B, T, D = x.shape
    tm = _pick(T, (512, 256, 128))
    xs = pl.BlockSpec((1, tm, D), lambda b, t: (b, t, 0))
    ms = pl.BlockSpec((1, 1, D), lambda b, t: (b, 0, 0))
    outs = pl.pallas_call(
        _rwkv_mix_body, grid=(B, T // tm),
        in_specs=[xs, ms, ms, pl.BlockSpec((6, D), lambda b, t: (0, 0)), ms],
        out_specs=[xs] * 6 + [ms],
        out_shape=[jax.ShapeDtypeStruct((B, T, D), BF16)] * 6 + [jax.ShapeDtypeStruct((B, 1, D), F32)],
        scratch_shapes=[pltpu.VMEM((tm + SUBLANE, D), F32), pltpu.VMEM((1, D), F32)],
        compiler_params=_cparams(("parallel", "arbitrary")), name="rwkv_mix",
    )(x, shift, scale, mu, shift_state.reshape(B, 1, D))
    return outs[:6], outs[6].reshape(B, D)


def _dot_f32(a, b):
    return jnp.dot(a, b, precision=lax.Precision.HIGHEST, preferred_element_type=F32)


def _rwkv_rec_body(r_ref, k_ref, v_ref, wl_ref, al_ref, g_ref, par_ref, s0_ref, y_ref, s_ref,
                   *, L, nc, n_valid):
    tb = pl.program_id(2)
    N = RWKV_HEAD
    P = 2 * N

    @pl.when(tb == 0)
    def _():
        s_ref[...] = s0_ref[...]

    par = par_ref[...]
    w0, a0, k_k, k_a, r_k, gain = (par[i:i + 1] for i in range(6))

    lane_l = lax.broadcasted_iota(jnp.int32, (L, P), 1)
    head0_l = lane_l < N
    r2 = lax.broadcasted_iota(jnp.int32, (2 * L, 2 * L), 0)
    c2 = lax.broadcasted_iota(jnp.int32, (2 * L, 2 * L), 1)
    same = (r2 >= L) == (c2 >= L)
    strict = same & (c2 < r2)
    incl = same & (c2 <= r2)
    eye = (r2 == c2).astype(F32)
    rowhead = (lax.broadcasted_iota(jnp.int32, (2 * L, P), 0) >= L) == \
              (lax.broadcasted_iota(jnp.int32, (2 * L, P), 1) >= N)
    rl = lax.broadcasted_iota(jnp.int32, (L, L), 0)
    cl_ = lax.broadcasted_iota(jnp.int32, (L, L), 1)
    tri = (cl_ <= rl).astype(F32)
    ip = lax.broadcasted_iota(jnp.int32, (P, P), 0)
    jp = lax.broadcasted_iota(jnp.int32, (P, P), 1)
    blockdiag = (ip >= N) == (jp >= N)
    seg = blockdiag.astype(BF16)

    def segsum(x):
        return jnp.dot(x.astype(BF16), seg, preferred_element_type=F32)

    def stack_heads(x):
        return jnp.concatenate([jnp.where(head0_l, x, 0.0), jnp.where(head0_l, 0.0, x)], axis=0)

    def fold_heads(x2):
        return x2[:L] + x2[L:]

    nt = (((1,), (1,)), ((), ()))
    tn = (((0,), (0,)), ((), ()))

    def chunk(c, carry):
        rows = pl.ds(pl.multiple_of(c * L, L), L)
        r = r_ref[0, rows, :]
        k = k_ref[0, rows, :]
        v = v_ref[0, rows, :]
        z = -(w0 + wl_ref[0, rows, :])
        softplus = jnp.maximum(z, 0.0) + jnp.log1p(jnp.exp(-jnp.abs(z)))
        lnd = -jnp.exp(-softplus - 0.5)
        a = jax.nn.sigmoid(a0 + al_ref[0, rows, :])
        kk = k * k_k
        kk = kk / jnp.maximum(jnp.sqrt(segsum(kk * kk)), 1e-12)
        kmod = k * (1.0 + (a - 1.0) * k_a)
        if n_valid < L:
            live = lax.broadcasted_iota(jnp.int32, (L, P), 0) < n_valid
            lnd = jnp.where(live, lnd, 0.0)
            kk = jnp.where(live, kk, 0.0)
            kmod = jnp.where(live, kmod, 0.0)
            v = jnp.where(live, v, 0.0)
        bvec = kk * a
        cum = _dot_f32(tri, lnd)
        e_neg = jnp.exp(-cum)
        alpha = kk * jnp.exp(cum - lnd)
        beta = bvec * e_neg
        kappa = kmod * e_neg
        rho = r * jnp.exp(cum)

        a2 = stack_heads(alpha)
        p2 = stack_heads(rho)
        lhs = jnp.concatenate([a2, p2], axis=0).astype(BF16)
        beta2 = jnp.concatenate([beta, beta], axis=0).astype(BF16)
        kappa2 = jnp.concatenate([kappa, kappa], axis=0).astype(BF16)
        gb = lax.dot_general(lhs, beta2, nt, preferred_element_type=F32)
        gk = lax.dot_general(lhs, kappa2, nt, preferred_element_type=F32)
        nmat = jnp.where(strict, gb[:2 * L], 0.0)
        ak = jnp.where(strict, gk[:2 * L], 0.0)
        pb = jnp.where(incl, gb[2 * L:], 0.0)
        pk = jnp.where(incl, gk[2 * L:], 0.0)

        tinv = eye - nmat
        pw = nmat
        for _ in range(max(L.bit_length() - 2, 0)):
            pw = _dot_f32(pw, pw)
            tinv = tinv + _dot_f32(tinv, pw)

        S = s_ref[0, 0]
        Sb = S.astype(BF16)
        v2 = jnp.concatenate([v, v], axis=0).astype(BF16)
        rhs = (lax.dot_general(a2.astype(BF16), Sb, nt, preferred_element_type=F32)
               + jnp.where(rowhead, jnp.dot(ak.astype(BF16), v2, preferred_element_type=F32), 0.0))
        w2 = _dot_f32(tinv, rhs)
        w = fold_heads(w2)
        wfull = jnp.concatenate([w, w], axis=0).astype(BF16)
        o2 = (lax.dot_general(p2.astype(BF16), Sb, nt, preferred_element_type=F32)
              + jnp.where(rowhead,
                          jnp.dot(pk.astype(BF16), v2, preferred_element_type=F32)
                          - jnp.dot(pb.astype(BF16), wfull, preferred_element_type=F32), 0.0))
        o = fold_heads(o2)

        xs = jnp.concatenate([v, -w], axis=0).astype(BF16)
        ys = jnp.concatenate([kappa, beta], axis=0).astype(BF16)
        ds = lax.dot_general(xs, ys, tn, preferred_element_type=F32)
        s_ref[0, 0] = (S + jnp.where(blockdiag, ds, 0.0)) * jnp.exp(cum[L - 1:L, :])

        mu = segsum(o) * (1.0 / N)
        d = o - mu
        var = segsum(d * d) * (1.0 / N)
        yn = d * lax.rsqrt(var + RWKV_GN_EPS) * gain
        bonus = segsum(r * kmod * r_k) * v
        y_ref[0, rows, :] = ((yn + bonus) * g_ref[0, rows, :]).astype(y_ref.dtype)
        return carry

    lax.fori_loop(0, nc, chunk, 0)


def _rwkv_rec(r, k, v, wl, al, g, par, s0_blk, n_valid):
    B, T, D = r.shape
    P = 2 * RWKV_HEAD
    L = RWKV_CHUNK
    tb = _pick(T, (RWKV_TBLOCK, 256, 128, 64))
    ts = pl.BlockSpec((1, tb, P), lambda b, p, t: (b, t, p))
    ss = pl.BlockSpec((1, 1, P, P), lambda b, p, t: (b, p, 0, 0))
    return pl.pallas_call(
        functools.partial(_rwkv_rec_body, L=L, nc=tb // L, n_valid=n_valid),
        grid=(B, D // P, T // tb),
        in_specs=[ts] * 6 + [pl.BlockSpec((SUBLANE, P), lambda b, p, t: (0, p)), ss],
        out_specs=[ts, ss],
        out_shape=[jax.ShapeDtypeStruct((B, T, D), BF16),
                   jax.ShapeDtypeStruct((B, D // P, P, P), F32)],
        compiler_params=_cparams(("parallel", "parallel", "arbitrary")), name="rwkv_rec",
    )(r, k, v, wl, al, g, par, s0_blk)


def _pair_blockdiag(s):
    B, H, N, _ = s.shape
    s = s.reshape(B, H // 2, 2, N, N)
    z = jnp.zeros_like(s[:, :, 0])
    top = jnp.concatenate([s[:, :, 0], z], axis=-1)
    bot = jnp.concatenate([z, s[:, :, 1]], axis=-1)
    return jnp.concatenate([top, bot], axis=-2)


def _pair_unblock(sb):
    B, Pn, P, _ = sb.shape
    N = P // 2
    return jnp.stack([sb[:, :, :N, :N], sb[:, :, N:, N:]], axis=2).reshape(B, 2 * Pn, N, N)


def _pad_time(a, T2):
    return jnp.pad(a, ((0, 0), (0, T2 - a.shape[1]), (0, 0)))


def _retention_layer(x, mods, pos0, state, w_in, w_out, gn_gain):
    B, T, D = x.shape
    H, dk, dv = state.shape[1:]
    shift_m, scale_m, gate_m = mods
    h = _norm_mod(x, shift_m, scale_m, BF16)
    half = dk // 2
    inv = ROPE_BASE ** (-jnp.arange(half, dtype=F32) / half)
    ang = (pos0 + jnp.arange(T)).astype(F32)[:, None] * inv[None, :]
    proj = _ret_proj(h, w_in, jnp.cos(ang), jnp.sin(ang), H * dk, dk)
    Tp = T if T % RET_CHUNK == 0 else -(-T // 64) * 64
    assert Tp == T or Tp <= RET_CHUNK
    y, s_new = _retention(_pad_time(proj, Tp), state, gn_gain, H, dk, dv, T)
    x = _mm_res(y[:, :T], w_out, x, gate_m, name="ret_out")
    return x, s_new


def _rwkv_layer(x, mods, shift_state, wkv_state, p):
    B, T, D = x.shape
    shift_m, scale_m, gate_m = mods
    (xr, xw, xk, xv, xa, xg), last = _rwkv_mix(x, shift_m, scale_m, p["mu"], shift_state)
    r = _mm(xr, p["w_r"], F32, name="rwkv_r")
    k = _mm(xk, p["w_k"], F32, name="rwkv_k")
    v = _mm(xv, p["w_v"], F32, name="rwkv_v")
    wl = _mm(_mm(xw, p["w1"], BF16, act="tanh", name="rwkv_w1"), p["w2"], F32, name="rwkv_w2")
    al = _mm(_mm(xa, p["a1"], BF16, name="rwkv_a1"), p["a2"], F32, name="rwkv_a2")
    g = _mm(_mm(xg, p["g1"], BF16, act="sigmoid", name="rwkv_g1"), p["g2"], F32, name="rwkv_g2")
    Tp = -(-T // RWKV_CHUNK) * RWKV_CHUNK
    ins = [_pad_time(t, Tp) for t in (r, k, v, wl, al, g)]
    y, s_blk = _rwkv_rec(*ins, p["par"], _pair_blockdiag(wkv_state), T)
    x = _mm_res(y[:, :T], p["w_o"], x, gate_m, name="rwkv_out")
    return x, _pair_unblock(s_blk), last


def _pad_cols(w, n):
    return jnp.pad(w, ((0, 0), (0, n - w.shape[1])))


def _pad_rows(w, n):
    return jnp.pad(w, ((0, n - w.shape[0]), (0, 0)))


def _run_group(x, mod, pos0, st_ret, st_wkv, st_shift, st_conv, w):
    depth = mod.shape[0]
    new_ret, new_wkv, new_shift, new_conv = [], [], [], []
    for i in range(depth):
        m = [mod[i, :, n][:, None, :] for n in range(6)]
        j = i // 2
        if i % 2 == 0:
            x, s = _retention_layer(x, m[:3], pos0, st_ret[j], w["ret_w_in"][j], w["ret_w_out"][j],
                                    w["ret_gn_gain"][j])
            new_ret.append(s)
        else:
            x, s, last = _rwkv_layer(x, m[:3], st_shift[j], st_wkv[j], w["rwkv"][j])
            new_wkv.append(s)
            new_shift.append(last)
        h = _norm_mod(x, m[3], m[4], BF16)
        x, cs = _ffn(h, x, m[5], w["ffn_w_gate"][i], w["ffn_w_up"][i], w["ffn_w_down"][i],
                     w["ffn_conv_w"][i], w["ffn_conv_b"][i], st_conv[i])
        new_conv.append(cs)
    out = _norm_gain(x, w["final_gain"])
    return out, jnp.stack(new_ret), jnp.stack(new_wkv), jnp.stack(new_shift), jnp.stack(new_conv)


def kernel(x_prompt, x_sample, c_prompt, c_sample, state_ret, state_rwkv_wkv, state_rwkv_shift, state_ffn_conv, ada_w, ada_b, ret_w_in, ret_w_out, ret_gn_gain, rwkv_mu, rwkv_w_r, rwkv_w_k, rwkv_w_v, rwkv_w_o, rwkv_w0, rwkv_w1, rwkv_w2, rwkv_a0, rwkv_a1, rwkv_a2, rwkv_g1, rwkv_g2, rwkv_k_k, rwkv_k_a, rwkv_r_k, rwkv_gn_gain, ffn_w_gate, ffn_w_up, ffn_conv_w, ffn_conv_b, ffn_w_down, final_gain):
    B, T, D = x_prompt.shape
    Bs = x_sample.shape[0]
    depth = ada_w.shape[0]
    n_rwkv = rwkv_mu.shape[0]

    rows = -(-(B + Bs) // SUBLANE) * SUBLANE
    c_all = jnp.pad(jnp.concatenate([c_prompt, c_sample], axis=0), ((0, rows - B - Bs), (0, 0)))
    mod = _ada(c_all, ada_w, ada_b).reshape(depth, rows, 6, D)

    bf = lambda a: a.astype(BF16)
    rwkv = []
    for j in range(n_rwkv):
        lw = -(-rwkv_w1.shape[2] // LANE) * LANE
        la = -(-rwkv_a1.shape[2] // LANE) * LANE
        par = jnp.stack([rwkv_w0[j], rwkv_a0[j], rwkv_k_k[j], rwkv_k_a[j], rwkv_r_k[j].reshape(D),
                         rwkv_gn_gain[j], jnp.zeros((D,), F32), jnp.zeros((D,), F32)])
        rwkv.append(dict(
            mu=rwkv_mu[j], w_r=bf(rwkv_w_r[j]), w_k=bf(rwkv_w_k[j]), w_v=bf(rwkv_w_v[j]),
            w_o=bf(rwkv_w_o[j]),
            w1=bf(_pad_cols(rwkv_w1[j], lw)), w2=bf(_pad_rows(rwkv_w2[j], lw)),
            a1=bf(_pad_cols(rwkv_a1[j], la)), a2=bf(_pad_rows(rwkv_a2[j], la)),
            g1=bf(rwkv_g1[j]), g2=bf(rwkv_g2[j]), par=par))
    w = dict(ret_w_in=bf(ret_w_in), ret_w_out=bf(ret_w_out), ret_gn_gain=ret_gn_gain, rwkv=rwkv,
             ffn_w_gate=bf(ffn_w_gate), ffn_w_up=bf(ffn_w_up), ffn_w_down=bf(ffn_w_down),
             ffn_conv_w=ffn_conv_w, ffn_conv_b=ffn_conv_b, final_gain=final_gain)

    n_ret = state_ret.shape[0]
    F_ = ffn_w_gate.shape[2]
    z_ret = jnp.zeros((n_ret, B) + state_ret.shape[2:], F32)
    z_wkv = jnp.zeros((n_rwkv, B) + state_rwkv_wkv.shape[2:], F32)
    z_shift = jnp.zeros((n_rwkv, B, D), F32)
    z_conv = jnp.zeros((depth, B, CONV_W - 1, F_), F32)
    y_p, p_ret, p_wkv, p_shift, p_conv = _run_group(
        x_prompt, mod[:, :B], 0, z_ret, z_wkv, z_shift, z_conv, w)
    y_s, s_ret, s_wkv, s_shift, s_conv = _run_group(
        x_sample, mod[:, B:B + Bs], PAST_LEN, state_ret, state_rwkv_wkv, state_rwkv_shift,
        state_ffn_conv, w)
    return (y_p, y_s, p_ret, p_wkv, p_shift, p_conv, s_ret, s_wkv, s_shift, s_conv)
```

```python
import functools

import jax
import jax.numpy as jnp
from jax import lax
from jax.experimental import pallas as pl
from jax.experimental.pallas import tpu as pltpu

F32 = jnp.float32
BF16 = jnp.bfloat16

NORM_EPS = 1e-6
RET_GN_EPS = 1e-5
RWKV_GN_EPS = 64e-5
ROPE_BASE = 10000.0
PAST_LEN = 4096
CONV_W = 3

LANE = 128
SUBLANE = 8
VMEM_LIMIT_MB = 56

RET_CHUNK = 256
RWKV_CHUNK = 64
RWKV_HEAD = 64
RWKV_GROUP = 8


def _pick(n, cands):
    for c in cands:
        if n % c == 0:
            return c
    return n


def _cparams(sem):
    return pltpu.CompilerParams(dimension_semantics=sem, vmem_limit_bytes=VMEM_LIMIT_MB << 20)


def _token_tiles(B, T):
    if T >= 512:
        return 1, _pick(T, (512, 256, 128))
    return B, T


def _silu(x):
    return x * jax.nn.sigmoid(x)


def _ada_body(c_ref, w_ref, b_ref, o_ref):
    c = c_ref[...]
    s = _silu(c).astype(BF16)
    o_ref[0] = jnp.dot(s, w_ref[0].astype(BF16), preferred_element_type=F32) + b_ref[0]


def _ada(c_all, ada_w, ada_b):
    depth, D, N = ada_w.shape
    R = c_all.shape[0]
    tn = _pick(N, (1024, 512, 256, 128))
    return pl.pallas_call(
        _ada_body,
        grid=(depth, N // tn),
        in_specs=[pl.BlockSpec((R, D), lambda l, j: (0, 0)),
                  pl.BlockSpec((1, D, tn), lambda l, j: (l, 0, j)),
                  pl.BlockSpec((1, 1, tn), lambda l, j: (l, 0, j))],
        out_specs=pl.BlockSpec((1, R, tn), lambda l, j: (l, 0, j)),
        out_shape=jax.ShapeDtypeStruct((depth, R, N), F32),
        compiler_params=_cparams(("parallel", "parallel")),
        name="ada_mod",
    )(c_all, ada_w, ada_b.reshape(depth, 1, N))


def _norm_mod_body(x_ref, sh_ref, sc_ref, o_ref):
    x = x_ref[...]
    xn = x * lax.rsqrt(jnp.mean(x * x, -1, keepdims=True) + NORM_EPS)
    o_ref[...] = (xn * (1.0 + sc_ref[...]) + sh_ref[...]).astype(o_ref.dtype)


def _norm_gain_body(x_ref, g_ref, o_ref):
    x = x_ref[...]
    xn = x * lax.rsqrt(jnp.mean(x * x, -1, keepdims=True) + NORM_EPS)
    o_ref[...] = (xn * g_ref[...]).astype(o_ref.dtype)


def _norm_mod(x, shift, scale, out_dtype):
    B, T, D = x.shape
    bb, tm = _token_tiles(B, T)
    xs = pl.BlockSpec((bb, tm, D), lambda b, t: (b, t, 0))
    ms = pl.BlockSpec((bb, 1, D), lambda b, t: (b, 0, 0))
    return pl.pallas_call(
        _norm_mod_body, grid=(B // bb, T // tm), in_specs=[xs, ms, ms], out_specs=xs,
        out_shape=jax.ShapeDtypeStruct((B, T, D), out_dtype),
        compiler_params=_cparams(("parallel", "parallel")), name="norm_mod",
    )(x, shift, scale)


def _norm_gain(x, gain):
    B, T, D = x.shape
    bb, tm = _token_tiles(B, T)
    xs = pl.BlockSpec((bb, tm, D), lambda b, t: (b, t, 0))
    return pl.pallas_call(
        _norm_gain_body, grid=(B // bb, T // tm),
        in_specs=[xs, pl.BlockSpec((1, 1, D), lambda b, t: (0, 0, 0))], out_specs=xs,
        out_shape=jax.ShapeDtypeStruct((B, T, D), F32),
        compiler_params=_cparams(("parallel", "parallel")), name="final_norm",
    )(x, gain.reshape(1, 1, D))


def _mm_body(x_ref, w_ref, o_ref, *, act):
    bb, tm, K = x_ref.shape
    acc = jnp.dot(x_ref[...].reshape(bb * tm, K), w_ref[...], preferred_element_type=F32)
    if act == "tanh":
        acc = jnp.tanh(acc)
    elif act == "sigmoid":
        acc = jax.nn.sigmoid(acc)
    o_ref[...] = acc.reshape(bb, tm, -1).astype(o_ref.dtype)


def _mm(x, w, out_dtype, act=None, name="mm"):
    B, T, K = x.shape
    N = w.shape[1]
    bb, tm = _token_tiles(B, T)
    tn = _pick(N, (1024, 512, 256, 128))
    return pl.pallas_call(
        functools.partial(_mm_body, act=act),
        grid=(B // bb, T // tm, N // tn),
        in_specs=[pl.BlockSpec((bb, tm, K), lambda b, t, j: (b, t, 0)),
                  pl.BlockSpec((K, tn), lambda b, t, j: (0, j))],
        out_specs=pl.BlockSpec((bb, tm, tn), lambda b, t, j: (b, t, j)),
        out_shape=jax.ShapeDtypeStruct((B, T, N), out_dtype),
        compiler_params=_cparams(("parallel", "parallel", "parallel")), name=name,
    )(x, w)


def _mm_res_body(y_ref, w_ref, x_ref, gate_ref, o_ref):
    bb, tm, K = y_ref.shape
    acc = jnp.dot(y_ref[...].reshape(bb * tm, K), w_ref[...], preferred_element_type=F32)
    o_ref[...] = x_ref[...] + gate_ref[...] * acc.reshape(bb, tm, -1)


def _mm_res(y, w, x, gate, name="mm_res"):
    B, T, K = y.shape
    N = w.shape[1]
    bb, tm = _token_tiles(B, T)
    tn = _pick(N, (512, 256, 128))
    return pl.pallas_call(
        _mm_res_body,
        grid=(B // bb, T // tm, N // tn),
        in_specs=[pl.BlockSpec((bb, tm, K), lambda b, t, j: (b, t, 0)),
                  pl.BlockSpec((K, tn), lambda b, t, j: (0, j)),
                  pl.BlockSpec((bb, tm, tn), lambda b, t, j: (b, t, j)),
                  pl.BlockSpec((bb, 1, tn), lambda b, t, j: (b, 0, j))],
        out_specs=pl.BlockSpec((bb, tm, tn), lambda b, t, j: (b, t, j)),
        out_shape=jax.ShapeDtypeStruct((B, T, N), F32),
        compiler_params=_cparams(("parallel", "parallel", "parallel")), name=name,
    )(y, w, x, gate)


def _ret_proj_body(h_ref, w_ref, cos_ref, sin_ref, o_ref, *, n_rot, n_q, dk):
    j = pl.program_id(2)
    bb, tm, K = h_ref.shape
    tn = w_ref.shape[1]
    acc = jnp.dot(h_ref[...].reshape(bb * tm, K), w_ref[...], preferred_element_type=F32)
    acc = acc.reshape(bb, tm, tn)

    @pl.when(j < n_rot)
    def _():
        cos = cos_ref[...][None]
        sin = sin_ref[...][None]
        sc = jnp.where(j >= n_q, dk ** -0.5, 1.0).astype(F32)
        half = dk // 2
        for hh in range(tn // dk):
            lo = hh * dk
            x1 = acc[:, :, lo:lo + half]
            x2 = acc[:, :, lo + half:lo + dk]
            o_ref[:, :, lo:lo + half] = ((x1 * cos - x2 * sin) * sc).astype(o_ref.dtype)
            o_ref[:, :, lo + half:lo + dk] = ((x1 * sin + x2 * cos) * sc).astype(o_ref.dtype)

    @pl.when(j >= n_rot)
    def _():
        o_ref[...] = acc.astype(o_ref.dtype)


def _ret_proj(h, w_in, cos, sin, qk_dim, dk):
    B, T, K = h.shape
    N = w_in.shape[1]
    bb, tm = _token_tiles(B, T)
    tn = _pick(qk_dim, (1024, 512, 256))
    half = dk // 2
    return pl.pallas_call(
        functools.partial(_ret_proj_body, n_rot=2 * qk_dim // tn, n_q=qk_dim // tn, dk=dk),
        grid=(B // bb, T // tm, N // tn),
        in_specs=[pl.BlockSpec((bb, tm, K), lambda b, t, j: (b, t, 0)),
                  pl.BlockSpec((K, tn), lambda b, t, j: (0, j)),
                  pl.BlockSpec((tm, half), lambda b, t, j: (t, 0)),
                  pl.BlockSpec((tm, half), lambda b, t, j: (t, 0))],
        out_specs=pl.BlockSpec((bb, tm, tn), lambda b, t, j: (b, t, j)),
        out_shape=jax.ShapeDtypeStruct((B, T, N), BF16),
        compiler_params=_cparams(("parallel", "parallel", "parallel")), name="ret_proj",
    )(h, w_in, cos, sin)


def _ret_body(q_ref, k_ref, v_ref, g_ref, gain_ref, intra_ref, cross_ref, into_ref, cd_ref, s0_ref,
              y_ref, s_ref):
    c = pl.program_id(2)

    @pl.when(c == 0)
    def _():
        s_ref[...] = s0_ref[...]

    q = q_ref[0]
    k = k_ref[0]
    v = v_ref[0]
    S = s_ref[0, 0]
    scores = lax.dot_general(q, k, (((1,), (1,)), ((), ())), preferred_element_type=F32) * intra_ref[0]
    o = (jnp.dot(scores.astype(BF16), v, preferred_element_type=F32)
         + jnp.dot(q, S.astype(BF16), preferred_element_type=F32) * cross_ref[0])
    kin = (k.astype(F32) * into_ref[0]).astype(BF16)
    s_ref[0, 0] = S * cd_ref[0] + lax.dot_general(kin, v, (((0,), (0,)), ((), ())),
                                                  preferred_element_type=F32)
    mu = jnp.mean(o, -1, keepdims=True)
    d = o - mu
    var = jnp.mean(d * d, -1, keepdims=True)
    g = g_ref[0].astype(F32)
    y_ref[0] = (d * lax.rsqrt(var + RET_GN_EPS) * gain_ref[...] * _silu(g)).astype(y_ref.dtype)


def _retention(proj, state, gain, H, dk, dv, n_valid):
    B, T, _ = proj.shape
    L = min(RET_CHUNK, T)
    nv = min(n_valid, L)
    log_g = jnp.log1p(-(2.0 ** (-5.0 - jnp.arange(H, dtype=F32))))
    idx = jnp.arange(L, dtype=F32)
    diff = idx[:, None] - idx[None, :]
    intra = jnp.where(diff >= 0, jnp.exp(log_g[:, None, None] * jnp.maximum(diff, 0.0)), 0.0)
    cross = jnp.exp(log_g[:, None] * (idx[None, :] + 1.0))[:, :, None]
    into = jnp.exp(log_g[:, None] * (nv - 1.0 - idx[None, :]))[:, :, None]
    cd = jnp.exp(log_g * nv)[:, None, None]
    kq = (H * dk) // dk
    vq = (2 * H * dk) // dv
    return pl.pallas_call(
        _ret_body,
        grid=(B, H, T // L),
        in_specs=[pl.BlockSpec((1, L, dk), lambda b, h, c: (b, c, h)),
                  pl.BlockSpec((1, L, dk), lambda b, h, c: (b, c, kq + h)),
                  pl.BlockSpec((1, L, dv), lambda b, h, c: (b, c, vq + h)),
                  pl.BlockSpec((1, L, dv), lambda b, h, c: (b, c, vq + H + h)),
                  pl.BlockSpec((1, dv), lambda b, h, c: (0, h)),
                  pl.BlockSpec((1, L, L), lambda b, h, c: (h, 0, 0)),
                  pl.BlockSpec((1, L, 1), lambda b, h, c: (h, 0, 0)),
                  pl.BlockSpec((1, L, 1), lambda b, h, c: (h, 0, 0)),
                  pl.BlockSpec((1, 1, 1), lambda b, h, c: (h, 0, 0)),
                  pl.BlockSpec((1, 1, dk, dv), lambda b, h, c: (b, h, 0, 0))],
        out_specs=[pl.BlockSpec((1, L, dv), lambda b, h, c: (b, c, h)),
                   pl.BlockSpec((1, 1, dk, dv), lambda b, h, c: (b, h, 0, 0))],
        out_shape=[jax.ShapeDtypeStruct((B, T, H * dv), BF16),
                   jax.ShapeDtypeStruct((B, H, dk, dv), F32)],
        compiler_params=_cparams(("parallel", "parallel", "arbitrary")), name="retention",
    )(proj, proj, proj, proj, gain.reshape(1, H * dv), intra, cross, into, cd, state)


def _ffn_body(h_ref, x_ref, gate_ref, wg_ref, wu_ref, wd_ref, cw_ref, cb_ref, cs_ref,
              o_ref, tail_ref, acc_ref, ext_ref):
    t = pl.program_id(1)
    f = pl.program_id(2)
    nf = pl.num_programs(2)
    bb, tm, D = h_ref.shape
    tf = wg_ref.shape[1]
    pad = SUBLANE

    h = h_ref[...].reshape(bb * tm, D)
    u = jnp.dot(h, wg_ref[...], preferred_element_type=F32).reshape(bb, tm, tf)
    up = jnp.dot(h, wu_ref[...], preferred_element_type=F32).reshape(bb, tm, tf)

    @pl.when(t == 0)
    def _():
        tail_ref[f] = cs_ref[...]

    ext_ref[:, pad - 2:pad, :] = tail_ref[f]
    ext_ref[:, pad:, :] = u
    cw = cw_ref[...]
    conv = (cb_ref[...] + ext_ref[:, pad - 2:pad - 2 + tm, :] * cw[0:1]
            + ext_ref[:, pad - 1:pad - 1 + tm, :] * cw[1:2] + u * cw[2:3])
    new_tail = ext_ref[:, pad + tm - 2:pad + tm, :]
    tail_ref[f] = new_tail
    act = (_silu(conv) * up).reshape(bb * tm, tf).astype(BF16)
    part = jnp.dot(act, wd_ref[...], preferred_element_type=F32)

    @pl.when(f == 0)
    def _():
        acc_ref[...] = part

    @pl.when(f > 0)
    def _():
        acc_ref[...] += part

    @pl.when(f == nf - 1)
    def _():
        o_ref[...] = x_ref[...] + gate_ref[...] * acc_ref[...].reshape(bb, tm, D)


def _ffn(h, x, gate, wg, wu, wd, conv_w, conv_b, conv_state):
    B, T, D = x.shape
    Fd = wg.shape[1]
    bb, tm = _token_tiles(B, T)
    tf = _pick(Fd, (512, 256, 128))
    nf = Fd // tf
    xs = pl.BlockSpec((bb, tm, D), lambda b, t, f: (b, t, 0))
    y, tails = pl.pallas_call(
        _ffn_body,
        grid=(B // bb, T // tm, nf),
        in_specs=[xs, xs,
                  pl.BlockSpec((bb, 1, D), lambda b, t, f: (b, 0, 0)),
                  pl.BlockSpec((D, tf), lambda b, t, f: (0, f)),
                  pl.BlockSpec((D, tf), lambda b, t, f: (0, f)),
                  pl.BlockSpec((tf, D), lambda b, t, f: (f, 0)),
                  pl.BlockSpec((CONV_W, tf), lambda b, t, f: (0, f)),
                  pl.BlockSpec((1, tf), lambda b, t, f: (0, f)),
                  pl.BlockSpec((bb, CONV_W - 1, tf), lambda b, t, f: (b, 0, f))],
        out_specs=[xs, pl.BlockSpec((nf, bb, CONV_W - 1, tf), lambda b, t, f: (0, b, 0, 0))],
        out_shape=[jax.ShapeDtypeStruct((B, T, D), F32),
                   jax.ShapeDtypeStruct((nf, B, CONV_W - 1, tf), F32)],
        scratch_shapes=[pltpu.VMEM((bb * tm, D), F32),
                        pltpu.VMEM((bb, tm + SUBLANE, tf), F32)],
        compiler_params=_cparams(("parallel", "arbitrary", "arbitrary")), name="conv_ffn",
    )(h, x, gate, wg, wu, wd, conv_w, conv_b.reshape(1, Fd), conv_state)
    return y, tails.transpose(1, 2, 0, 3).reshape(B, CONV_W - 1, Fd)


def _rwkv_mix_body(x_ref, sh_ref, sc_ref, mu_ref, st_ref,
                   xr_ref, xw_ref, xk_ref, xv_ref, xa_ref, xg_ref, last_ref, hbuf_ref, carry_ref):
    t = pl.program_id(1)
    tm = x_ref.shape[1]
    pad = SUBLANE
    x = x_ref[0]
    xn = x * lax.rsqrt(jnp.mean(x * x, -1, keepdims=True) + NORM_EPS)
    h = xn * (1.0 + sc_ref[0]) + sh_ref[0]

    @pl.when(t == 0)
    def _():
        carry_ref[...] = st_ref[0]

    hbuf_ref[pad - 1:pad, :] = carry_ref[...]
    hbuf_ref[pad:, :] = h
    xx = hbuf_ref[pad - 1:pad - 1 + tm, :] - h
    for n, ref in enumerate((xr_ref, xw_ref, xk_ref, xv_ref, xa_ref, xg_ref)):
        ref[0] = (h + xx * mu_ref[n:n + 1, :]).astype(ref.dtype)
    last = hbuf_ref[pad + tm - 1:pad + tm, :]
    carry_ref[...] = last
    last_ref[0] = last


def _rwkv_mix(x, shift, scale, mu, shift_state):
    B, T, D = x.shape
    tm = _pick(T, (512, 256, 128))
    xs = pl.BlockSpec((1, tm, D), lambda b, t: (b, t, 0))
    ms = pl.BlockSpec((1, 1, D), lambda b, t: (b, 0, 0))
    outs = pl.pallas_call(
        _rwkv_mix_body, grid=(B, T // tm),
        in_specs=[xs, ms, ms, pl.BlockSpec((6, D), lambda b, t: (0, 0)), ms],
        out_specs=[xs] * 6 + [ms],
        out_shape=[jax.ShapeDtypeStruct((B, T, D), BF16)] * 6 + [jax.ShapeDtypeStruct((B, 1, D), F32)],
        scratch_shapes=[pltpu.VMEM((tm + SUBLANE, D), F32), pltpu.VMEM((1, D), F32)],
        compiler_params=_cparams(("parallel", "arbitrary")), name="rwkv_mix",
    )(x, shift, scale, mu, shift_state.reshape(B, 1, D))
    return outs[:6], outs[6].reshape(B, D)


def _dot(a, b):
    return jnp.dot(a.astype(BF16), b.astype(BF16), preferred_element_type=F32)


def _dot_nt(a, b):
    return lax.dot_general(a.astype(BF16), b.astype(BF16), (((1,), (1,)), ((), ())),
                           preferred_element_type=F32)


def _dot_tn(a, b):
    return lax.dot_general(a.astype(BF16), b.astype(BF16), (((0,), (0,)), ((), ())),
                           preferred_element_type=F32)


def _rwkv_rec_body(r_ref, k_ref, v_ref, wl_ref, al_ref, g_ref, par_ref, s0_ref, y_ref, s_ref,
                   *, L, npair, group, n_valid):
    c = pl.program_id(1)
    N = RWKV_HEAD
    P = 2 * N

    lane_l = lax.broadcasted_iota(jnp.int32, (L, P), 1)
    head0_l = lane_l < N
    r2 = lax.broadcasted_iota(jnp.int32, (2 * L, 2 * L), 0)
    c2 = lax.broadcasted_iota(jnp.int32, (2 * L, 2 * L), 1)
    same = (r2 >= L) == (c2 >= L)
    strict = same & (c2 < r2)
    incl = same & (c2 <= r2)
    rowhead = (lax.broadcasted_iota(jnp.int32, (2 * L, P), 0) >= L) == \
              (lax.broadcasted_iota(jnp.int32, (2 * L, P), 1) >= N)
    rl = lax.broadcasted_iota(jnp.int32, (L, L), 0)
    cl_ = lax.broadcasted_iota(jnp.int32, (L, L), 1)
    tri = (cl_ <= rl).astype(BF16)
    ip = lax.broadcasted_iota(jnp.int32, (P, P), 0)
    jp = lax.broadcasted_iota(jnp.int32, (P, P), 1)
    blockdiag = (ip >= N) == (jp >= N)
    seg = blockdiag.astype(BF16)
    live = lax.broadcasted_iota(jnp.int32, (L, P), 0) < n_valid

    def segsum(x):
        return jnp.dot(x.astype(BF16), seg, preferred_element_type=F32)

    def cumsum_rows(x):
        h1 = x.astype(BF16)
        r1 = x - h1.astype(F32)
        h2 = r1.astype(BF16)
        h3 = (r1 - h2.astype(F32)).astype(BF16)
        return (jnp.dot(tri, h1, preferred_element_type=F32) + jnp.dot(tri, h2, preferred_element_type=F32)
                + jnp.dot(tri, h3, preferred_element_type=F32))

    def stack_heads(x):
        return jnp.concatenate([jnp.where(head0_l, x, 0.0), jnp.where(head0_l, 0.0, x)], axis=0)

    def fold_heads(x2):
        return x2[:L] + x2[L:]

    def twice(x):
        return jnp.concatenate([x, x], axis=0)

    @pl.when(c == 0)
    def _():
        s_ref[...] = s0_ref[...]

    def segsum_all(xs):
        out = segsum(jnp.concatenate(xs, axis=0))
        return [out[i * L:(i + 1) * L] for i in range(len(xs))]

    for g0 in range(0, npair, group):
        pairs = range(g0, min(g0 + group, npair))
        lanes = [slice(p * P, (p + 1) * P) for p in pairs]
        n = len(lanes)
        par = [par_ref[:, ln] for ln in lanes]
        r = [r_ref[0, :, ln] for ln in lanes]
        k = [k_ref[0, :, ln] for ln in lanes]
        v = [v_ref[0, :, ln] for ln in lanes]
        lnd, a, kkraw, kmod = [], [], [], []
        for i in range(n):
            w0, a0, k_k, k_a = (par[i][j:j + 1] for j in range(4))
            z = -(w0 + wl_ref[0, :, lanes[i]])
            softplus = jnp.maximum(z, 0.0) + jnp.log1p(jnp.exp(-jnp.abs(z)))
            lnd.append(-jnp.exp(-softplus - 0.5))
            a.append(jax.nn.sigmoid(a0 + al_ref[0, :, lanes[i]]))
            kkraw.append(k[i] * k_k)
            kmod.append(k[i] * (1.0 + (a[i] - 1.0) * k_a))
        sums = segsum_all([x * x for x in kkraw] + [r[i] * kmod[i] * par[i][4:5] for i in range(n)])
        kk = [kkraw[i] / jnp.maximum(jnp.sqrt(sums[i]), 1e-12) for i in range(n)]
        rk = sums[n:]
        if n_valid < L:
            lnd = [jnp.where(live, x, 0.0) for x in lnd]
            kk = [jnp.where(live, x, 0.0) for x in kk]
            kmod = [jnp.where(live, x, 0.0) for x in kmod]
            v = [jnp.where(live, x, 0.0) for x in v]
        cum = [cumsum_rows(x) for x in lnd]
        beta, kappa, rho, a2, lhs = [], [], [], [], []
        for i in range(n):
            e_neg = jnp.exp(-cum[i])
            beta.append((kk[i] * a[i] * e_neg).astype(BF16))
            kappa.append((kmod[i] * e_neg).astype(BF16))
            rho.append(r[i] * jnp.exp(cum[i]))
            a2.append(stack_heads(kk[i] * jnp.exp(cum[i] - lnd[i])))
            lhs.append(jnp.concatenate([a2[i], stack_heads(rho[i])], axis=0).astype(BF16))
        gb = [_dot_nt(lhs[i], twice(beta[i])) for i in range(n)]
        gk = [_dot_nt(lhs[i], twice(kappa[i])) for i in range(n)]
        nmat = [jnp.where(strict, x[:2 * L], 0.0) for x in gb]
        pb = [jnp.where(incl, x[2 * L:], 0.0).astype(BF16) for x in gb]
        v2 = [twice(x).astype(BF16) for x in v]
        akv = [jnp.where(rowhead, _dot(jnp.where(strict, gk[i][:2 * L], 0.0), v2[i]), 0.0)
               for i in range(n)]
        pkv = [_dot(jnp.where(incl, gk[i][2 * L:], 0.0), v2[i]) for i in range(n)]

        corr = [-x for x in nmat]
        pw = [x.astype(BF16) for x in nmat]
        for _ in range(max(L.bit_length() - 2, 0)):
            sq = [_dot(x, x) for x in pw]
            pw = [x.astype(BF16) for x in sq]
            corr = [corr[i] + sq[i] + _dot(corr[i], pw[i]) for i in range(n)]

        both = [jnp.concatenate([a2[i], akv[i]], axis=1) for i in range(n)]
        both = [both[i] + _dot(corr[i], both[i]) for i in range(n)]
        at = [fold_heads(x[:, :P]) for x in both]
        w0_ = [fold_heads(x[:, P:]) for x in both]
        pbx = [_dot(pb[i], twice(jnp.concatenate([at[i], w0_[i]], axis=1))) for i in range(n)]
        rho_t = [rho[i] - fold_heads(jnp.where(rowhead, pbx[i][:, :P], 0.0)) for i in range(n)]
        o0 = [fold_heads(jnp.where(rowhead, pkv[i] - pbx[i][:, P:], 0.0)) for i in range(n)]
        gmat = [jnp.where(blockdiag, _dot_tn(at[i], beta[i]), 0.0) for i in range(n)]
        umat = [jnp.where(blockdiag, _dot_tn(jnp.concatenate([v[i], -w0_[i]], axis=0),
                                             jnp.concatenate([kappa[i], beta[i]], axis=0)), 0.0)
                for i in range(n)]

        S = [s_ref[0, p] for p in pairs]
        Sb = [x.astype(BF16) for x in S]
        o = [_dot_nt(rho_t[i], Sb[i]) + o0[i] for i in range(n)]
        for i, p in enumerate(pairs):
            s_ref[0, p] = (S[i] - _dot(Sb[i], gmat[i]) + umat[i]) * jnp.exp(cum[i][L - 1:L, :])

        mu = segsum_all(o)
        d = [o[i] - mu[i] * (1.0 / N) for i in range(n)]
        var = segsum_all([x * x for x in d])
        for i in range(n):
            yn = d[i] * lax.rsqrt(var[i] * (1.0 / N) + RWKV_GN_EPS) * par[i][5:6]
            y_ref[0, :, lanes[i]] = ((yn + rk[i] * v[i]) * g_ref[0, :, lanes[i]]).astype(y_ref.dtype)


def _rwkv_rec(r, k, v, wl, al, g, par, s0_blk, n_valid):
    B, T, D = r.shape
    P = 2 * RWKV_HEAD
    L = RWKV_CHUNK
    assert T % L == 0 and (n_valid == T or T == L)
    npair = D // P
    ts = pl.BlockSpec((1, L, D), lambda b, c: (b, c, 0))
    ss = pl.BlockSpec((1, npair, P, P), lambda b, c: (b, 0, 0, 0))
    return pl.pallas_call(
        functools.partial(_rwkv_rec_body, L=L, npair=npair, group=RWKV_GROUP, n_valid=min(n_valid, L)),
        grid=(B, T // L),
        in_specs=[ts] * 6 + [pl.BlockSpec((SUBLANE, D), lambda b, c: (0, 0)), ss],
        out_specs=[ts, ss],
        out_shape=[jax.ShapeDtypeStruct((B, T, D), BF16),
                   jax.ShapeDtypeStruct((B, npair, P, P), F32)],
        compiler_params=_cparams(("parallel", "arbitrary")), name="rwkv_rec",
    )(r, k, v, wl, al, g, par, s0_blk)


def _pair_blockdiag(s):
    B, H, N, _ = s.shape
    s = s.reshape(B, H // 2, 2, N, N)
    z = jnp.zeros_like(s[:, :, 0])
    top = jnp.concatenate([s[:, :, 0], z], axis=-1)
    bot = jnp.concatenate([z, s[:, :, 1]], axis=-1)
    return jnp.concatenate([top, bot], axis=-2)


def _pair_unblock(sb):
    B, Pn, P, _ = sb.shape
    N = P // 2
    return jnp.stack([sb[:, :, :N, :N], sb[:, :, N:, N:]], axis=2).reshape(B, 2 * Pn, N, N)


def _pad_time(a, T2):
    return jnp.pad(a, ((0, 0), (0, T2 - a.shape[1]), (0, 0)))


def _retention_layer(x, mods, pos0, state, w_in, w_out, gn_gain):
    B, T, D = x.shape
    H, dk, dv = state.shape[1:]
    shift_m, scale_m, gate_m = mods
    h = _norm_mod(x, shift_m, scale_m, BF16)
    half = dk // 2
    inv = ROPE_BASE ** (-jnp.arange(half, dtype=F32) / half)
    ang = (pos0 + jnp.arange(T)).astype(F32)[:, None] * inv[None, :]
    proj = _ret_proj(h, w_in, jnp.cos(ang), jnp.sin(ang), H * dk, dk)
    Tp = T if T % RET_CHUNK == 0 else -(-T // 64) * 64
    assert Tp == T or Tp <= RET_CHUNK
    y, s_new = _retention(_pad_time(proj, Tp), state, gn_gain, H, dk, dv, T)
    x = _mm_res(y[:, :T], w_out, x, gate_m, name="ret_out")
    return x, s_new


def _rwkv_layer(x, mods, shift_state, wkv_state, p):
    B, T, D = x.shape
    shift_m, scale_m, gate_m = mods
    (xr, xw, xk, xv, xa, xg), last = _rwkv_mix(x, shift_m, scale_m, p["mu"], shift_state)
    r = _mm(xr, p["w_r"], F32, name="rwkv_r")
    k = _mm(xk, p["w_k"], F32, name="rwkv_k")
    v = _mm(xv, p["w_v"], F32, name="rwkv_v")
    wl = _mm(_mm(xw, p["w1"], BF16, act="tanh", name="rwkv_w1"), p["w2"], F32, name="rwkv_w2")
    al = _mm(_mm(xa, p["a1"], BF16, name="rwkv_a1"), p["a2"], F32, name="rwkv_a2")
    g = _mm(_mm(xg, p["g1"], BF16, act="sigmoid", name="rwkv_g1"), p["g2"], F32, name="rwkv_g2")
    Tp = -(-T // RWKV_CHUNK) * RWKV_CHUNK
    ins = [_pad_time(t, Tp) for t in (r, k, v, wl, al, g)]
    y, s_blk = _rwkv_rec(*ins, p["par"], _pair_blockdiag(wkv_state), T)
    x = _mm_res(y[:, :T], p["w_o"], x, gate_m, name="rwkv_out")
    return x, _pair_unblock(s_blk), last


def _pad_cols(w, n):
    return jnp.pad(w, ((0, 0), (0, n - w.shape[1])))


def _pad_rows(w, n):
    return jnp.pad(w, ((0, n - w.shape[0]), (0, 0)))


def _run_group(x, mod, pos0, st_ret, st_wkv, st_shift, st_conv, w):
    depth = mod.shape[0]
    new_ret, new_wkv, new_shift, new_conv = [], [], [], []
    for i in range(depth):
        m = [mod[i, :, n][:, None, :] for n in range(6)]
        j = i // 2
        if i % 2 == 0:
            x, s = _retention_layer(x, m[:3], pos0, st_ret[j], w["ret_w_in"][j], w["ret_w_out"][j],
                                    w["ret_gn_gain"][j])
            new_ret.append(s)
        else:
            x, s, last = _rwkv_layer(x, m[:3], st_shift[j], st_wkv[j], w["rwkv"][j])
            new_wkv.append(s)
            new_shift.append(last)
        h = _norm_mod(x, m[3], m[4], BF16)
        x, cs = _ffn(h, x, m[5], w["ffn_w_gate"][i], w["ffn_w_up"][i], w["ffn_w_down"][i],
                     w["ffn_conv_w"][i], w["ffn_conv_b"][i], st_conv[i])
        new_conv.append(cs)
    out = _norm_gain(x, w["final_gain"])
    return out, jnp.stack(new_ret), jnp.stack(new_wkv), jnp.stack(new_shift), jnp.stack(new_conv)


def kernel(x_prompt, x_sample, c_prompt, c_sample, state_ret, state_rwkv_wkv, state_rwkv_shift, state_ffn_conv, ada_w, ada_b, ret_w_in, ret_w_out, ret_gn_gain, rwkv_mu, rwkv_w_r, rwkv_w_k, rwkv_w_v, rwkv_w_o, rwkv_w0, rwkv_w1, rwkv_w2, rwkv_a0, rwkv_a1, rwkv_a2, rwkv_g1, rwkv_g2, rwkv_k_k, rwkv_k_a, rwkv_r_k, rwkv_gn_gain, ffn_w_gate, ffn_w_up, ffn_conv_w, ffn_conv_b, ffn_w_down, final_gain):
    B, T, D = x_prompt.shape
    Bs = x_sample.shape[0]
    depth = ada_w.shape[0]
    n_rwkv = rwkv_mu.shape[0]

    rows = -(-(B + Bs) // SUBLANE) * SUBLANE
    c_all = jnp.pad(jnp.concatenate([c_prompt, c_sample], axis=0), ((0, rows - B - Bs), (0, 0)))
    mod = _ada(c_all, ada_w, ada_b).reshape(depth, rows, 6, D)

    bf = lambda a: a.astype(BF16)
    rwkv = []
    for j in range(n_rwkv):
        lw = -(-rwkv_w1.shape[2] // LANE) * LANE
        la = -(-rwkv_a1.shape[2] // LANE) * LANE
        par = jnp.stack([rwkv_w0[j], rwkv_a0[j], rwkv_k_k[j], rwkv_k_a[j], rwkv_r_k[j].reshape(D),
                         rwkv_gn_gain[j], jnp.zeros((D,), F32), jnp.zeros((D,), F32)])
        rwkv.append(dict(
            mu=rwkv_mu[j], w_r=bf(rwkv_w_r[j]), w_k=bf(rwkv_w_k[j]), w_v=bf(rwkv_w_v[j]),
            w_o=bf(rwkv_w_o[j]),
            w1=bf(_pad_cols(rwkv_w1[j], lw)), w2=bf(_pad_rows(rwkv_w2[j], lw)),
            a1=bf(_pad_cols(rwkv_a1[j], la)), a2=bf(_pad_rows(rwkv_a2[j], la)),
            g1=bf(rwkv_g1[j]), g2=bf(rwkv_g2[j]), par=par))
    w = dict(ret_w_in=bf(ret_w_in), ret_w_out=bf(ret_w_out), ret_gn_gain=ret_gn_gain, rwkv=rwkv,
             ffn_w_gate=bf(ffn_w_gate), ffn_w_up=bf(ffn_w_up), ffn_w_down=bf(ffn_w_down),
             ffn_conv_w=ffn_conv_w, ffn_conv_b=ffn_conv_b, final_gain=final_gain)

    n_ret = state_ret.shape[0]
    F_ = ffn_w_gate.shape[2]
    z_ret = jnp.zeros((n_ret, B) + state_ret.shape[2:], F32)
    z_wkv = jnp.zeros((n_rwkv, B) + state_rwkv_wkv.shape[2:], F32)
    z_shift = jnp.zeros((n_rwkv, B, D), F32)
    z_conv = jnp.zeros((depth, B, CONV_W - 1, F_), F32)
    y_p, p_ret, p_wkv, p_shift, p_conv = _run_group(
        x_prompt, mod[:, :B], 0, z_ret, z_wkv, z_shift, z_conv, w)
    y_s, s_ret, s_wkv, s_shift, s_conv = _run_group(
        x_sample, mod[:, B:B + Bs], PAST_LEN, state_ret, state_rwkv_wkv, state_rwkv_shift,
        state_ffn_conv, w)
    return (y_p, y_s, p_ret, p_wkv, p_shift, p_conv, s_ret, s_wkv, s_shift, s_conv)
```

```python
import functools

import jax
import jax.numpy as jnp
from jax import lax
from jax.experimental import pallas as pl
from jax.experimental.pallas import tpu as pltpu

F32 = jnp.float32
BF16 = jnp.bfloat16

NORM_EPS = 1e-6
RET_GN_EPS = 1e-5
RWKV_GN_EPS = 64e-5
ROPE_BASE = 10000.0
PAST_LEN = 4096
CONV_W = 3

LANE = 128
SUBLANE = 8
VMEM_LIMIT_MB = 56

RET_CHUNK = 256
RET_TBLOCK = 1024
RWKV_CHUNK = 64
RWKV_HEAD = 64
RWKV_GROUP = 8


def _pick(n, cands):
    for c in cands:
        if n % c == 0:
            return c
    return n


def _cparams(sem):
    return pltpu.CompilerParams(dimension_semantics=sem, vmem_limit_bytes=VMEM_LIMIT_MB << 20)


def _token_tiles(B, T):
    if T >= 512:
        return 1, _pick(T, (1024, 512, 256, 128))
    return B, T


def _silu(x):
    return x * jax.nn.sigmoid(x)


def _ada_body(c_ref, w_ref, b_ref, o_ref):
    c = c_ref[...]
    s = _silu(c).astype(BF16)
    o_ref[0] = jnp.dot(s, w_ref[0].astype(BF16), preferred_element_type=F32) + b_ref[0]


def _ada(c_all, ada_w, ada_b):
    depth, D, N = ada_w.shape
    R = c_all.shape[0]
    tn = _pick(N, (1024, 512, 256, 128))
    return pl.pallas_call(
        _ada_body,
        grid=(depth, N // tn),
        in_specs=[pl.BlockSpec((R, D), lambda l, j: (0, 0)),
                  pl.BlockSpec((1, D, tn), lambda l, j: (l, 0, j)),
                  pl.BlockSpec((1, 1, tn), lambda l, j: (l, 0, j))],
        out_specs=pl.BlockSpec((1, R, tn), lambda l, j: (l, 0, j)),
        out_shape=jax.ShapeDtypeStruct((depth, R, N), F32),
        compiler_params=_cparams(("parallel", "parallel")),
        name="ada_mod",
    )(c_all, ada_w, ada_b.reshape(depth, 1, N))


def _norm_mod_body(x_ref, sh_ref, sc_ref, o_ref):
    x = x_ref[...]
    xn = x * lax.rsqrt(jnp.mean(x * x, -1, keepdims=True) + NORM_EPS)
    o_ref[...] = (xn * (1.0 + sc_ref[...]) + sh_ref[...]).astype(o_ref.dtype)


def _norm_gain_body(x_ref, g_ref, o_ref):
    x = x_ref[...]
    xn = x * lax.rsqrt(jnp.mean(x * x, -1, keepdims=True) + NORM_EPS)
    o_ref[...] = (xn * g_ref[...]).astype(o_ref.dtype)


def _norm_mod(x, shift, scale, out_dtype):
    B, T, D = x.shape
    bb, tm = _token_tiles(B, T)
    xs = pl.BlockSpec((bb, tm, D), lambda b, t: (b, t, 0))
    ms = pl.BlockSpec((bb, 1, D), lambda b, t: (b, 0, 0))
    return pl.pallas_call(
        _norm_mod_body, grid=(B // bb, T // tm), in_specs=[xs, ms, ms], out_specs=xs,
        out_shape=jax.ShapeDtypeStruct((B, T, D), out_dtype),
        compiler_params=_cparams(("parallel", "parallel")), name="norm_mod",
    )(x, shift, scale)


def _norm_gain(x, gain):
    B, T, D = x.shape
    bb, tm = _token_tiles(B, T)
    xs = pl.BlockSpec((bb, tm, D), lambda b, t: (b, t, 0))
    return pl.pallas_call(
        _norm_gain_body, grid=(B // bb, T // tm),
        in_specs=[xs, pl.BlockSpec((1, 1, D), lambda b, t: (0, 0, 0))], out_specs=xs,
        out_shape=jax.ShapeDtypeStruct((B, T, D), F32),
        compiler_params=_cparams(("parallel", "parallel")), name="final_norm",
    )(x, gain.reshape(1, 1, D))


def _mm_body(x_ref, w_ref, o_ref, *, act):
    bb, tm, K = x_ref.shape
    acc = jnp.dot(x_ref[...].reshape(bb * tm, K), w_ref[...], preferred_element_type=F32)
    if act == "tanh":
        acc = jnp.tanh(acc)
    elif act == "sigmoid":
        acc = jax.nn.sigmoid(acc)
    o_ref[...] = acc.reshape(bb, tm, -1).astype(o_ref.dtype)


def _mm(x, w, out_dtype, act=None, name="mm"):
    B, T, K = x.shape
    N = w.shape[1]
    bb, tm = _token_tiles(B, T)
    tn = _pick(N, (1024, 512, 256, 128))
    return pl.pallas_call(
        functools.partial(_mm_body, act=act),
        grid=(B // bb, T // tm, N // tn),
        in_specs=[pl.BlockSpec((bb, tm, K), lambda b, t, j: (b, t, 0)),
                  pl.BlockSpec((K, tn), lambda b, t, j: (0, j))],
        out_specs=pl.BlockSpec((bb, tm, tn), lambda b, t, j: (b, t, j)),
        out_shape=jax.ShapeDtypeStruct((B, T, N), out_dtype),
        compiler_params=_cparams(("parallel", "parallel", "parallel")), name=name,
    )(x, w)


def _mm_res_body(y_ref, w_ref, x_ref, gate_ref, o_ref):
    bb, tm, K = y_ref.shape
    acc = jnp.dot(y_ref[...].reshape(bb * tm, K), w_ref[...], preferred_element_type=F32)
    o_ref[...] = x_ref[...] + gate_ref[...] * acc.reshape(bb, tm, -1)


def _mm_res(y, w, x, gate, name="mm_res"):
    B, T, K = y.shape
    N = w.shape[1]
    bb, tm = _token_tiles(B, T)
    tn = _pick(N, (512, 256, 128))
    return pl.pallas_call(
        _mm_res_body,
        grid=(B // bb, T // tm, N // tn),
        in_specs=[pl.BlockSpec((bb, tm, K), lambda b, t, j: (b, t, 0)),
                  pl.BlockSpec((K, tn), lambda b, t, j: (0, j)),
                  pl.BlockSpec((bb, tm, tn), lambda b, t, j: (b, t, j)),
                  pl.BlockSpec((bb, 1, tn), lambda b, t, j: (b, 0, j))],
        out_specs=pl.BlockSpec((bb, tm, tn), lambda b, t, j: (b, t, j)),
        out_shape=jax.ShapeDtypeStruct((B, T, N), F32),
        compiler_params=_cparams(("parallel", "parallel", "parallel")), name=name,
    )(y, w, x, gate)


def _ret_proj_body(h_ref, w_ref, cos_ref, sin_ref, o_ref, *, n_rot, n_q, dk):
    j = pl.program_id(2)
    bb, tm, K = h_ref.shape
    tn = w_ref.shape[1]
    acc = jnp.dot(h_ref[...].reshape(bb * tm, K), w_ref[...], preferred_element_type=F32)
    acc = acc.reshape(bb, tm, tn)

    @pl.when(j < n_rot)
    def _():
        cos = cos_ref[...][None]
        sin = sin_ref[...][None]
        sc = jnp.where(j >= n_q, dk ** -0.5, 1.0).astype(F32)
        half = dk // 2
        for hh in range(tn // dk):
            lo = hh * dk
            x1 = acc[:, :, lo:lo + half]
            x2 = acc[:, :, lo + half:lo + dk]
            o_ref[:, :, lo:lo + half] = ((x1 * cos - x2 * sin) * sc).astype(o_ref.dtype)
            o_ref[:, :, lo + half:lo + dk] = ((x1 * sin + x2 * cos) * sc).astype(o_ref.dtype)

    @pl.when(j >= n_rot)
    def _():
        o_ref[...] = acc.astype(o_ref.dtype)


def _ret_proj(h, w_in, cos, sin, qk_dim, dk):
    B, T, K = h.shape
    N = w_in.shape[1]
    bb, tm = _token_tiles(B, T)
    tn = _pick(qk_dim, (1024, 512, 256))
    half = dk // 2
    return pl.pallas_call(
        functools.partial(_ret_proj_body, n_rot=2 * qk_dim // tn, n_q=qk_dim // tn, dk=dk),
        grid=(B // bb, T // tm, N // tn),
        in_specs=[pl.BlockSpec((bb, tm, K), lambda b, t, j: (b, t, 0)),
                  pl.BlockSpec((K, tn), lambda b, t, j: (0, j)),
                  pl.BlockSpec((tm, half), lambda b, t, j: (t, 0)),
                  pl.BlockSpec((tm, half), lambda b, t, j: (t, 0))],
        out_specs=pl.BlockSpec((bb, tm, tn), lambda b, t, j: (b, t, j)),
        out_shape=jax.ShapeDtypeStruct((B, T, N), BF16),
        compiler_params=_cparams(("parallel", "parallel", "parallel")), name="ret_proj",
    )(h, w_in, cos, sin)


def _ret_body(q_ref, k_ref, v_ref, g_ref, gain_ref, intra_ref, cross_ref, into_ref, cd_ref, s0_ref,
              y_ref, s_ref):
    c = pl.program_id(2)
    L = intra_ref.shape[1]

    @pl.when(c == 0)
    def _():
        s_ref[...] = s0_ref[...]

    for ci in range(q_ref.shape[1] // L):
        rows = slice(ci * L, (ci + 1) * L)
        q = q_ref[0, rows, :]
        k = k_ref[0, rows, :]
        v = v_ref[0, rows, :]
        S = s_ref[0, 0]
        scores = lax.dot_general(q, k, (((1,), (1,)), ((), ())), preferred_element_type=F32) * intra_ref[0]
        o = (jnp.dot(scores.astype(BF16), v, preferred_element_type=F32)
             + jnp.dot(q, S.astype(BF16), preferred_element_type=F32) * cross_ref[0])
        kin = (k.astype(F32) * into_ref[0]).astype(BF16)
        s_ref[0, 0] = S * cd_ref[0] + lax.dot_general(kin, v, (((0,), (0,)), ((), ())),
                                                      preferred_element_type=F32)
        mu = jnp.mean(o, -1, keepdims=True)
        d = o - mu
        var = jnp.mean(d * d, -1, keepdims=True)
        g = g_ref[0, rows, :].astype(F32)
        y_ref[0, rows, :] = (d * lax.rsqrt(var + RET_GN_EPS) * gain_ref[...] * _silu(g)).astype(y_ref.dtype)


def _retention(proj, state, gain, H, dk, dv, n_valid):
    B, T, _ = proj.shape
    L = min(RET_CHUNK, T)
    nv = min(n_valid, L)
    log_g = jnp.log1p(-(2.0 ** (-5.0 - jnp.arange(H, dtype=F32))))
    idx = jnp.arange(L, dtype=F32)
    diff = idx[:, None] - idx[None, :]
    intra = jnp.where(diff >= 0, jnp.exp(log_g[:, None, None] * jnp.maximum(diff, 0.0)), 0.0)
    cross = jnp.exp(log_g[:, None] * (idx[None, :] + 1.0))[:, :, None]
    into = jnp.exp(log_g[:, None] * (nv - 1.0 - idx[None, :]))[:, :, None]
    cd = jnp.exp(log_g * nv)[:, None, None]
    tb = _pick(T, (RET_TBLOCK, L))
    kq = (H * dk) // dk
    vq = (2 * H * dk) // dv
    return pl.pallas_call(
        _ret_body,
        grid=(B, H, T // tb),
        in_specs=[pl.BlockSpec((1, tb, dk), lambda b, h, c: (b, c, h)),
                  pl.BlockSpec((1, tb, dk), lambda b, h, c: (b, c, kq + h)),
                  pl.BlockSpec((1, tb, dv), lambda b, h, c: (b, c, vq + h)),
                  pl.BlockSpec((1, tb, dv), lambda b, h, c: (b, c, vq + H + h)),
                  pl.BlockSpec((1, dv), lambda b, h, c: (0, h)),
                  pl.BlockSpec((1, L, L), lambda b, h, c: (h, 0, 0)),
                  pl.BlockSpec((1, L, 1), lambda b, h, c: (h, 0, 0)),
                  pl.BlockSpec((1, L, 1), lambda b, h, c: (h, 0, 0)),
                  pl.BlockSpec((1, 1, 1), lambda b, h, c: (h, 0, 0)),
                  pl.BlockSpec((1, 1, dk, dv), lambda b, h, c: (b, h, 0, 0))],
        out_specs=[pl.BlockSpec((1, tb, dv), lambda b, h, c: (b, c, h)),
                   pl.BlockSpec((1, 1, dk, dv), lambda b, h, c: (b, h, 0, 0))],
        out_shape=[jax.ShapeDtypeStruct((B, T, H * dv), BF16),
                   jax.ShapeDtypeStruct((B, H, dk, dv), F32)],
        compiler_params=_cparams(("parallel", "parallel", "arbitrary")), name="retention",
    )(proj, proj, proj, proj, gain.reshape(1, H * dv), intra, cross, into, cd, state)


def _ffn_up_body(h_ref, wg_ref, wu_ref, cw_ref, cb_ref, cs_ref, o_ref, tail_ref, ext_ref):
    t = pl.program_id(1)
    f = pl.program_id(2)
    bb, tm, D = h_ref.shape
    tf = wg_ref.shape[1]
    pad = SUBLANE

    h = h_ref[...].reshape(bb * tm, D)
    u = jnp.dot(h, wg_ref[...], preferred_element_type=F32).reshape(bb, tm, tf)
    up = jnp.dot(h, wu_ref[...], preferred_element_type=F32).reshape(bb, tm, tf)

    @pl.when(t == 0)
    def _():
        tail_ref[f] = cs_ref[...]

    ext_ref[:, pad - 2:pad, :] = tail_ref[f]
    ext_ref[:, pad:, :] = u
    cw = cw_ref[...]
    conv = (cb_ref[...] + ext_ref[:, pad - 2:pad - 2 + tm, :] * cw[0:1]
            + ext_ref[:, pad - 1:pad - 1 + tm, :] * cw[1:2] + u * cw[2:3])
    new_tail = ext_ref[:, pad + tm - 2:pad + tm, :]
    tail_ref[f] = new_tail
    o_ref[...] = (_silu(conv) * up).astype(o_ref.dtype)


def _ffn(h, x, gate, wg, wu, wd, conv_w, conv_b, conv_state):
    B, T, D = x.shape
    Fd = wg.shape[1]
    bb, tm = _token_tiles(B, T)
    tf = _pick(Fd, (512, 256, 128))
    nf = Fd // tf
    act, tails = pl.pallas_call(
        _ffn_up_body,
        grid=(B // bb, T // tm, nf),
        in_specs=[pl.BlockSpec((bb, tm, D), lambda b, t, f: (b, t, 0)),
                  pl.BlockSpec((D, tf), lambda b, t, f: (0, f)),
                  pl.BlockSpec((D, tf), lambda b, t, f: (0, f)),
                  pl.BlockSpec((CONV_W, tf), lambda b, t, f: (0, f)),
                  pl.BlockSpec((1, tf), lambda b, t, f: (0, f)),
                  pl.BlockSpec((bb, CONV_W - 1, tf), lambda b, t, f: (b, 0, f))],
        out_specs=[pl.BlockSpec((bb, tm, tf), lambda b, t, f: (b, t, f)),
                   pl.BlockSpec((nf, bb, CONV_W - 1, tf), lambda b, t, f: (0, b, 0, 0))],
        out_shape=[jax.ShapeDtypeStruct((B, T, Fd), BF16),
                   jax.ShapeDtypeStruct((nf, B, CONV_W - 1, tf), F32)],
        scratch_shapes=[pltpu.VMEM((bb, tm + SUBLANE, tf), F32)],
        compiler_params=_cparams(("parallel", "arbitrary", "arbitrary")), name="ffn_up",
    )(h, wg, wu, conv_w, conv_b.reshape(1, Fd), conv_state)
    y = _mm_res(act, wd, x, gate, name="ffn_down")
    return y, tails.transpose(1, 2, 0, 3).reshape(B, CONV_W - 1, Fd)


def _rwkv_mix_body(x_ref, sh_ref, sc_ref, mu_ref, st_ref,
                   xr_ref, xw_ref, xk_ref, xv_ref, xa_ref, xg_ref, last_ref, hbuf_ref, carry_ref):
    t = pl.program_id(1)
    tm = x_ref.shape[1]
    pad = SUBLANE
    x = x_ref[0]
    xn = x * lax.rsqrt(jnp.mean(x * x, -1, keepdims=True) + NORM_EPS)
    h = xn * (1.0 + sc_ref[0]) + sh_ref[0]

    @pl.when(t == 0)
    def _():
        carry_ref[...] = st_ref[0]

    hbuf_ref[pad - 1:pad, :] = carry_ref[...]
    hbuf_ref[pad:, :] = h
    xx = hbuf_ref[pad - 1:pad - 1 + tm, :] - h
    for n, ref in enumerate((xr_ref, xw_ref, xk_ref, xv_ref, xa_ref, xg_ref)):
        ref[0] = (h + xx * mu_ref[n:n + 1, :]).astype(ref.dtype)
    last = hbuf_ref[pad + tm - 1:pad + tm, :]
    carry_ref[...] = last
    last_ref[0] = last


def _rwkv_mix(x, shift, scale, mu, shift_state):
    B, T, D = x.shape
    tm = _pick(T, (512, 256, 128))
    xs = pl.BlockSpec((1, tm, D), lambda b, t: (b, t, 0))
    ms = pl.BlockSpec((1, 1, D), lambda b, t: (b, 0, 0))
    outs = pl.pallas_call(
        _rwkv_mix_body, grid=(B, T // tm),
        in_specs=[xs, ms, ms, pl.BlockSpec((6, D), lambda b, t: (0, 0)), ms],
        out_specs=[xs] * 6 + [ms],
        out_shape=[jax.ShapeDtypeStruct((B, T, D), BF16)] * 6 + [jax.ShapeDtypeStruct((B, 1, D), F32)],
        scratch_shapes=[pltpu.VMEM((tm + SUBLANE, D), F32), pltpu.VMEM((1, D), F32)],
        compiler_params=_cparams(("parallel", "arbitrary")), name="rwkv_mix",
    )(x, shift, scale, mu, shift_state.reshape(B, 1, D))
    return outs[:6], outs[6].reshape(B, D)


def _dot(a, b):
    return jnp.dot(a.astype(BF16), b.astype(BF16), preferred_element_type=F32)


def _dot_nt(a, b):
    return lax.dot_general(a.astype(BF16), b.astype(BF16), (((1,), (1,)), ((), ())),
                           preferred_element_type=F32)


def _dot_tn(a, b):
    return lax.dot_general(a.astype(BF16), b.astype(BF16), (((0,), (0,)), ((), ())),
                           preferred_element_type=F32)


def _rwkv_rec_body(r_ref, k_ref, v_ref, wl_ref, al_ref, g_ref, par_ref, s0_ref, y_ref, s_ref,
                   *, L, npair, group, n_valid):
    c = pl.program_id(1)
    N = RWKV_HEAD
    P = 2 * N

    lane_l = lax.broadcasted_iota(jnp.int32, (L, P), 1)
    head0_l = lane_l < N
    r2 = lax.broadcasted_iota(jnp.int32, (2 * L, 2 * L), 0)
    c2 = lax.broadcasted_iota(jnp.int32, (2 * L, 2 * L), 1)
    same = (r2 >= L) == (c2 >= L)
    strict = same & (c2 < r2)
    incl = same & (c2 <= r2)
    rowhead = (lax.broadcasted_iota(jnp.int32, (2 * L, P), 0) >= L) == \
              (lax.broadcasted_iota(jnp.int32, (2 * L, P), 1) >= N)
    rl = lax.broadcasted_iota(jnp.int32, (L, L), 0)
    cl_ = lax.broadcasted_iota(jnp.int32, (L, L), 1)
    tri = (cl_ <= rl).astype(BF16)
    ip = lax.broadcasted_iota(jnp.int32, (P, P), 0)
    jp = lax.broadcasted_iota(jnp.int32, (P, P), 1)
    blockdiag = (ip >= N) == (jp >= N)
    seg = blockdiag.astype(BF16)
    live = lax.broadcasted_iota(jnp.int32, (L, P), 0) < n_valid

    def segsum(x):
        return jnp.dot(x.astype(BF16), seg, preferred_element_type=F32)

    def cumsum_rows(x):
        h1 = x.astype(BF16)
        r1 = x - h1.astype(F32)
        h2 = r1.astype(BF16)
        h3 = (r1 - h2.astype(F32)).astype(BF16)
        return (jnp.dot(tri, h1, preferred_element_type=F32) + jnp.dot(tri, h2, preferred_element_type=F32)
                + jnp.dot(tri, h3, preferred_element_type=F32))

    def stack_heads(x):
        return jnp.concatenate([jnp.where(head0_l, x, 0.0), jnp.where(head0_l, 0.0, x)], axis=0)

    def fold_heads(x2):
        return x2[:L] + x2[L:]

    def twice(x):
        return jnp.concatenate([x, x], axis=0)

    @pl.when(c == 0)
    def _():
        s_ref[...] = s0_ref[...]

    def segsum_all(xs):
        out = segsum(jnp.concatenate(xs, axis=0))
        return [out[i * L:(i + 1) * L] for i in range(len(xs))]

    for g0 in range(0, npair, group):
        pairs = range(g0, min(g0 + group, npair))
        lanes = [slice(p * P, (p + 1) * P) for p in pairs]
        n = len(lanes)
        par = [par_ref[:, ln] for ln in lanes]
        r = [r_ref[0, :, ln] for ln in lanes]
        k = [k_ref[0, :, ln] for ln in lanes]
        v = [v_ref[0, :, ln] for ln in lanes]
        lnd, a, kkraw, kmod = [], [], [], []
        for i in range(n):
            w0, a0, k_k, k_a = (par[i][j:j + 1] for j in range(4))
            z = -(w0 + wl_ref[0, :, lanes[i]])
            softplus = jnp.maximum(z, 0.0) + jnp.log1p(jnp.exp(-jnp.abs(z)))
            lnd.append(-jnp.exp(-softplus - 0.5))
            a.append(jax.nn.sigmoid(a0 + al_ref[0, :, lanes[i]]))
            kkraw.append(k[i] * k_k)
            kmod.append(k[i] * (1.0 + (a[i] - 1.0) * k_a))
        sums = segsum_all([x * x for x in kkraw] + [r[i] * kmod[i] * par[i][4:5] for i in range(n)])
        kk = [kkraw[i] / jnp.maximum(jnp.sqrt(sums[i]), 1e-12) for i in range(n)]
        rk = sums[n:]
        if n_valid < L:
            lnd = [jnp.where(live, x, 0.0) for x in lnd]
            kk = [jnp.where(live, x, 0.0) for x in kk]
            kmod = [jnp.where(live, x, 0.0) for x in kmod]
            v = [jnp.where(live, x, 0.0) for x in v]
        cum = [cumsum_rows(x) for x in lnd]
        beta, kappa, rho, a2, lhs = [], [], [], [], []
        for i in range(n):
            e_neg = jnp.exp(-cum[i])
            beta.append((kk[i] * a[i] * e_neg).astype(BF16))
            kappa.append((kmod[i] * e_neg).astype(BF16))
            rho.append(r[i] * jnp.exp(cum[i]))
            a2.append(stack_heads(kk[i] * jnp.exp(cum[i] - lnd[i])))
            lhs.append(jnp.concatenate([a2[i], stack_heads(rho[i])], axis=0).astype(BF16))
        gb = [_dot_nt(lhs[i], twice(beta[i])) for i in range(n)]
        gk = [_dot_nt(lhs[i], twice(kappa[i])) for i in range(n)]
        nmat = [jnp.where(strict, x[:2 * L], 0.0) for x in gb]
        pb = [jnp.where(incl, x[2 * L:], 0.0).astype(BF16) for x in gb]
        v2 = [twice(x).astype(BF16) for x in v]
        akv = [jnp.where(rowhead, _dot(jnp.where(strict, gk[i][:2 * L], 0.0), v2[i]), 0.0)
               for i in range(n)]
        pkv = [_dot(jnp.where(incl, gk[i][2 * L:], 0.0), v2[i]) for i in range(n)]

        corr = [-x for x in nmat]
        pw = [x.astype(BF16) for x in nmat]
        for _ in range(max(L.bit_length() - 2, 0)):
            sq = [_dot(x, x) for x in pw]
            pw = [x.astype(BF16) for x in sq]
            corr = [corr[i] + sq[i] + _dot(corr[i], pw[i]) for i in range(n)]

        both = [jnp.concatenate([a2[i], akv[i]], axis=1) for i in range(n)]
        both = [both[i] + _dot(corr[i], both[i]) for i in range(n)]
        at = [fold_heads(x[:, :P]) for x in both]
        w0_ = [fold_heads(x[:, P:]) for x in both]
        pbx = [_dot(pb[i], twice(jnp.concatenate([at[i], w0_[i]], axis=1))) for i in range(n)]
        rho_t = [rho[i] - fold_heads(jnp.where(rowhead, pbx[i][:, :P], 0.0)) for i in range(n)]
        o0 = [fold_heads(jnp.where(rowhead, pkv[i] - pbx[i][:, P:], 0.0)) for i in range(n)]
        gmat = [jnp.where(blockdiag, _dot_tn(at[i], beta[i]), 0.0) for i in range(n)]
        umat = [jnp.where(blockdiag, _dot_tn(jnp.concatenate([v[i], -w0_[i]], axis=0),
                                             jnp.concatenate([kappa[i], beta[i]], axis=0)), 0.0)
                for i in range(n)]

        S = [s_ref[0, p] for p in pairs]
        Sb = [x.astype(BF16) for x in S]
        o = [_dot_nt(rho_t[i], Sb[i]) + o0[i] for i in range(n)]
        for i, p in enumerate(pairs):
            s_ref[0, p] = (S[i] - _dot(Sb[i], gmat[i]) + umat[i]) * jnp.exp(cum[i][L - 1:L, :])

        mu = segsum_all(o)
        d = [o[i] - mu[i] * (1.0 / N) for i in range(n)]
        var = segsum_all([x * x for x in d])
        for i in range(n):
            yn = d[i] * lax.rsqrt(var[i] * (1.0 / N) + RWKV_GN_EPS) * par[i][5:6]
            y_ref[0, :, lanes[i]] = ((yn + rk[i] * v[i]) * g_ref[0, :, lanes[i]]).astype(y_ref.dtype)


def _rwkv_rec(r, k, v, wl, al, g, par, s0_blk, n_valid):
    B, T, D = r.shape
    P = 2 * RWKV_HEAD
    L = RWKV_CHUNK
    assert T % L == 0 and (n_valid == T or T == L)
    npair = D // P
    ts = pl.BlockSpec((1, L, D), lambda b, c: (b, c, 0))
    ss = pl.BlockSpec((1, npair, P, P), lambda b, c: (b, 0, 0, 0))
    return pl.pallas_call(
        functools.partial(_rwkv_rec_body, L=L, npair=npair, group=RWKV_GROUP, n_valid=min(n_valid, L)),
        grid=(B, T // L),
        in_specs=[ts] * 6 + [pl.BlockSpec((SUBLANE, D), lambda b, c: (0, 0)), ss],
        out_specs=[ts, ss],
        out_shape=[jax.ShapeDtypeStruct((B, T, D), BF16),
                   jax.ShapeDtypeStruct((B, npair, P, P), F32)],
        compiler_params=_cparams(("parallel", "arbitrary")), name="rwkv_rec",
    )(r, k, v, wl, al, g, par, s0_blk)


def _pair_blockdiag(s):
    B, H, N, _ = s.shape
    s = s.reshape(B, H // 2, 2, N, N)
    z = jnp.zeros_like(s[:, :, 0])
    top = jnp.concatenate([s[:, :, 0], z], axis=-1)
    bot = jnp.concatenate([z, s[:, :, 1]], axis=-1)
    return jnp.concatenate([top, bot], axis=-2)


def _pair_unblock(sb):
    B, Pn, P, _ = sb.shape
    N = P // 2
    return jnp.stack([sb[:, :, :N, :N], sb[:, :, N:, N:]], axis=2).reshape(B, 2 * Pn, N, N)


def _pad_time(a, T2):
    return jnp.pad(a, ((0, 0), (0, T2 - a.shape[1]), (0, 0)))


def _retention_layer(x, mods, pos0, state, w_in, w_out, gn_gain):
    B, T, D = x.shape
    H, dk, dv = state.shape[1:]
    shift_m, scale_m, gate_m = mods
    h = _norm_mod(x, shift_m, scale_m, BF16)
    half = dk // 2
    inv = ROPE_BASE ** (-jnp.arange(half, dtype=F32) / half)
    ang = (pos0 + jnp.arange(T)).astype(F32)[:, None] * inv[None, :]
    proj = _ret_proj(h, w_in, jnp.cos(ang), jnp.sin(ang), H * dk, dk)
    Tp = T if T % RET_CHUNK == 0 else -(-T // 64) * 64
    assert Tp == T or Tp <= RET_CHUNK
    y, s_new = _retention(_pad_time(proj, Tp), state, gn_gain, H, dk, dv, T)
    x = _mm_res(y[:, :T], w_out, x, gate_m, name="ret_out")
    return x, s_new


def _rwkv_layer(x, mods, shift_state, wkv_state, p):
    B, T, D = x.shape
    shift_m, scale_m, gate_m = mods
    (xr, xw, xk, xv, xa, xg), last = _rwkv_mix(x, shift_m, scale_m, p["mu"], shift_state)
    r = _mm(xr, p["w_r"], F32, name="rwkv_r")
    k = _mm(xk, p["w_k"], F32, name="rwkv_k")
    v = _mm(xv, p["w_v"], F32, name="rwkv_v")
    wl = _mm(_mm(xw, p["w1"], BF16, act="tanh", name="rwkv_w1"), p["w2"], F32, name="rwkv_w2")
    al = _mm(_mm(xa, p["a1"], BF16, name="rwkv_a1"), p["a2"], F32, name="rwkv_a2")
    g = _mm(_mm(xg, p["g1"], BF16, act="sigmoid", name="rwkv_g1"), p["g2"], F32, name="rwkv_g2")
    Tp = -(-T // RWKV_CHUNK) * RWKV_CHUNK
    ins = [_pad_time(t, Tp) for t in (r, k, v, wl, al, g)]
    y, s_blk = _rwkv_rec(*ins, p["par"], _pair_blockdiag(wkv_state), T)
    x = _mm_res(y[:, :T], p["w_o"], x, gate_m, name="rwkv_out")
    return x, _pair_unblock(s_blk), last


def _pad_cols(w, n):
    return jnp.pad(w, ((0, 0), (0, n - w.shape[1])))


def _pad_rows(w, n):
    return jnp.pad(w, ((0, n - w.shape[0]), (0, 0)))


def _run_group(x, mod, pos0, st_ret, st_wkv, st_shift, st_conv, w):
    depth = mod.shape[0]
    new_ret, new_wkv, new_shift, new_conv = [], [], [], []
    for i in range(depth):
        m = [mod[i, :, n][:, None, :] for n in range(6)]
        j = i // 2
        if i % 2 == 0:
            x, s = _retention_layer(x, m[:3], pos0, st_ret[j], w["ret_w_in"][j], w["ret_w_out"][j],
                                    w["ret_gn_gain"][j])
            new_ret.append(s)
        else:
            x, s, last = _rwkv_layer(x, m[:3], st_shift[j], st_wkv[j], w["rwkv"][j])
            new_wkv.append(s)
            new_shift.append(last)
        h = _norm_mod(x, m[3], m[4], BF16)
        x, cs = _ffn(h, x, m[5], w["ffn_w_gate"][i], w["ffn_w_up"][i], w["ffn_w_down"][i],
                     w["ffn_conv_w"][i], w["ffn_conv_b"][i], st_conv[i])
        new_conv.append(cs)
    out = _norm_gain(x, w["final_gain"])
    return out, jnp.stack(new_ret), jnp.stack(new_wkv), jnp.stack(new_shift), jnp.stack(new_conv)


def kernel(x_prompt, x_sample, c_prompt, c_sample, state_ret, state_rwkv_wkv, state_rwkv_shift, state_ffn_conv, ada_w, ada_b, ret_w_in, ret_w_out, ret_gn_gain, rwkv_mu, rwkv_w_r, rwkv_w_k, rwkv_w_v, rwkv_w_o, rwkv_w0, rwkv_w1, rwkv_w2, rwkv_a0, rwkv_a1, rwkv_a2, rwkv_g1, rwkv_g2, rwkv_k_k, rwkv_k_a, rwkv_r_k, rwkv_gn_gain, ffn_w_gate, ffn_w_up, ffn_conv_w, ffn_conv_b, ffn_w_down, final_gain):
    B, T, D = x_prompt.shape
    Bs = x_sample.shape[0]
    depth = ada_w.shape[0]
    n_rwkv = rwkv_mu.shape[0]

    rows = -(-(B + Bs) // SUBLANE) * SUBLANE
    c_all = jnp.pad(jnp.concatenate([c_prompt, c_sample], axis=0), ((0, rows - B - Bs), (0, 0)))
    mod = _ada(c_all, ada_w, ada_b).reshape(depth, rows, 6, D)

    bf = lambda a: a.astype(BF16)
    rwkv = []
    for j in range(n_rwkv):
        lw = -(-rwkv_w1.shape[2] // LANE) * LANE
        la = -(-rwkv_a1.shape[2] // LANE) * LANE
        par = jnp.stack([rwkv_w0[j], rwkv_a0[j], rwkv_k_k[j], rwkv_k_a[j], rwkv_r_k[j].reshape(D),
                         rwkv_gn_gain[j], jnp.zeros((D,), F32), jnp.zeros((D,), F32)])
        rwkv.append(dict(
            mu=rwkv_mu[j], w_r=bf(rwkv_w_r[j]), w_k=bf(rwkv_w_k[j]), w_v=bf(rwkv_w_v[j]),
            w_o=bf(rwkv_w_o[j]),
            w1=bf(_pad_cols(rwkv_w1[j], lw)), w2=bf(_pad_rows(rwkv_w2[j], lw)),
            a1=bf(_pad_cols(rwkv_a1[j], la)), a2=bf(_pad_rows(rwkv_a2[j], la)),
            g1=bf(rwkv_g1[j]), g2=bf(rwkv_g2[j]), par=par))
    w = dict(ret_w_in=bf(ret_w_in), ret_w_out=bf(ret_w_out), ret_gn_gain=ret_gn_gain, rwkv=rwkv,
             ffn_w_gate=bf(ffn_w_gate), ffn_w_up=bf(ffn_w_up), ffn_w_down=bf(ffn_w_down),
             ffn_conv_w=ffn_conv_w, ffn_conv_b=ffn_conv_b, final_gain=final_gain)

    n_ret = state_ret.shape[0]
    F_ = ffn_w_gate.shape[2]
    z_ret = jnp.zeros((n_ret, B) + state_ret.shape[2:], F32)
    z_wkv = jnp.zeros((n_rwkv, B) + state_rwkv_wkv.shape[2:], F32)
    z_shift = jnp.zeros((n_rwkv, B, D), F32)
    z_conv = jnp.zeros((depth, B, CONV_W - 1, F_), F32)
    y_p, p_ret, p_wkv, p_shift, p_conv = _run_group(
        x_prompt, mod[:, :B], 0, z_ret, z_wkv, z_shift, z_conv, w)
    y_s, s_ret, s_wkv, s_shift, s_conv = _run_group(
        x_sample, mod[:, B:B + Bs], PAST_LEN, state_ret, state_rwkv_wkv, state_rwkv_shift,
        state_ffn_conv, w)
    return (y_p, y_s, p_ret, p_wkv, p_shift, p_conv, s_ret, s_wkv, s_shift, s_conv)
```

```python
import functools

import jax
import jax.numpy as jnp
from jax import lax
from jax.experimental import pallas as pl
from jax.experimental.pallas import tpu as pltpu

F32 = jnp.float32
BF16 = jnp.bfloat16

NORM_EPS = 1e-6
RET_GN_EPS = 1e-5
RWKV_GN_EPS = 64e-5
ROPE_BASE = 10000.0
PAST_LEN = 4096
CONV_W = 3

LANE = 128
SUBLANE = 8
VMEM_LIMIT_MB = 56

RET_CHUNK = 256
ROW_CHUNK = 256
RET_TBLOCK = 1024
RWKV_CHUNK = 64
RWKV_HEAD = 64
RWKV_GROUP = 8


def _pick(n, cands):
    for c in cands:
        if n % c == 0:
            return c
    return n


def _cparams(sem):
    return pltpu.CompilerParams(dimension_semantics=sem, vmem_limit_bytes=VMEM_LIMIT_MB << 20)


def _token_tiles(B, T):
    if T >= 512:
        return 1, _pick(T, (1024, 512, 256, 128))
    return B, T


def _silu(x):
    return x * jax.nn.sigmoid(x)


def _ada_body(c_ref, w_ref, b_ref, o_ref):
    c = c_ref[...]
    s = _silu(c).astype(BF16)
    o_ref[0] = jnp.dot(s, w_ref[0].astype(BF16), preferred_element_type=F32) + b_ref[0]


def _ada(c_all, ada_w, ada_b):
    depth, D, N = ada_w.shape
    R = c_all.shape[0]
    tn = _pick(N, (1024, 512, 256, 128))
    return pl.pallas_call(
        _ada_body,
        grid=(depth, N // tn),
        in_specs=[pl.BlockSpec((R, D), lambda l, j: (0, 0)),
                  pl.BlockSpec((1, D, tn), lambda l, j: (l, 0, j)),
                  pl.BlockSpec((1, 1, tn), lambda l, j: (l, 0, j))],
        out_specs=pl.BlockSpec((1, R, tn), lambda l, j: (l, 0, j)),
        out_shape=jax.ShapeDtypeStruct((depth, R, N), F32),
        compiler_params=_cparams(("parallel", "parallel")),
        name="ada_mod",
    )(c_all, ada_w, ada_b.reshape(depth, 1, N))


def _norm_mod_body(x_ref, sh_ref, sc_ref, o_ref):
    x = x_ref[...]
    xn = x * lax.rsqrt(jnp.mean(x * x, -1, keepdims=True) + NORM_EPS)
    o_ref[...] = (xn * (1.0 + sc_ref[...]) + sh_ref[...]).astype(o_ref.dtype)


def _norm_gain_body(x_ref, g_ref, o_ref):
    x = x_ref[...]
    xn = x * lax.rsqrt(jnp.mean(x * x, -1, keepdims=True) + NORM_EPS)
    o_ref[...] = (xn * g_ref[...]).astype(o_ref.dtype)


def _norm_mod(x, shift, scale, out_dtype):
    B, T, D = x.shape
    bb, tm = _token_tiles(B, T)
    xs = pl.BlockSpec((bb, tm, D), lambda b, t: (b, t, 0))
    ms = pl.BlockSpec((bb, 1, D), lambda b, t: (b, 0, 0))
    return pl.pallas_call(
        _norm_mod_body, grid=(B // bb, T // tm), in_specs=[xs, ms, ms], out_specs=xs,
        out_shape=jax.ShapeDtypeStruct((B, T, D), out_dtype),
        compiler_params=_cparams(("parallel", "parallel")), name="norm_mod",
    )(x, shift, scale)


def _norm_gain(x, gain):
    B, T, D = x.shape
    bb, tm = _token_tiles(B, T)
    xs = pl.BlockSpec((bb, tm, D), lambda b, t: (b, t, 0))
    return pl.pallas_call(
        _norm_gain_body, grid=(B // bb, T // tm),
        in_specs=[xs, pl.BlockSpec((1, 1, D), lambda b, t: (0, 0, 0))], out_specs=xs,
        out_shape=jax.ShapeDtypeStruct((B, T, D), F32),
        compiler_params=_cparams(("parallel", "parallel")), name="final_norm",
    )(x, gain.reshape(1, 1, D))


def _mm_body(x_ref, w_ref, o_ref, *, act):
    bb, tm, K = x_ref.shape
    acc = jnp.dot(x_ref[...].reshape(bb * tm, K), w_ref[...], preferred_element_type=F32)
    if act == "tanh":
        acc = jnp.tanh(acc)
    elif act == "sigmoid":
        acc = jax.nn.sigmoid(acc)
    o_ref[...] = acc.reshape(bb, tm, -1).astype(o_ref.dtype)


def _mm(x, w, out_dtype, act=None, name="mm"):
    B, T, K = x.shape
    N = w.shape[1]
    bb, tm = _token_tiles(B, T)
    tn = _pick(N, (1024, 512, 256, 128))
    return pl.pallas_call(
        functools.partial(_mm_body, act=act),
        grid=(B // bb, T // tm, N // tn),
        in_specs=[pl.BlockSpec((bb, tm, K), lambda b, t, j: (b, t, 0)),
                  pl.BlockSpec((K, tn), lambda b, t, j: (0, j))],
        out_specs=pl.BlockSpec((bb, tm, tn), lambda b, t, j: (b, t, j)),
        out_shape=jax.ShapeDtypeStruct((B, T, N), out_dtype),
        compiler_params=_cparams(("parallel", "parallel", "parallel")), name=name,
    )(x, w)


def _mm_res_body(y_ref, w_ref, x_ref, gate_ref, o_ref):
    bb, tm, K = y_ref.shape
    acc = jnp.dot(y_ref[...].reshape(bb * tm, K), w_ref[...], preferred_element_type=F32)
    o_ref[...] = x_ref[...] + gate_ref[...] * acc.reshape(bb, tm, -1)


def _mm_res(y, w, x, gate, name="mm_res"):
    B, T, K = y.shape
    N = w.shape[1]
    bb, tm = _token_tiles(B, T)
    tn = _pick(N, (512, 256, 128))
    return pl.pallas_call(
        _mm_res_body,
        grid=(B // bb, T // tm, N // tn),
        in_specs=[pl.BlockSpec((bb, tm, K), lambda b, t, j: (b, t, 0)),
                  pl.BlockSpec((K, tn), lambda b, t, j: (0, j)),
                  pl.BlockSpec((bb, tm, tn), lambda b, t, j: (b, t, j)),
                  pl.BlockSpec((bb, 1, tn), lambda b, t, j: (b, 0, j))],
        out_specs=pl.BlockSpec((bb, tm, tn), lambda b, t, j: (b, t, j)),
        out_shape=jax.ShapeDtypeStruct((B, T, N), F32),
        compiler_params=_cparams(("parallel", "parallel", "parallel")), name=name,
    )(y, w, x, gate)


def _ret_proj_body(h_ref, w_ref, cos_ref, sin_ref, o_ref, *, n_rot, n_q, dk):
    j = pl.program_id(2)
    bb, tm, K = h_ref.shape
    tn = w_ref.shape[1]

    @pl.when(j < n_rot)
    def _():
        sc = jnp.where(j >= n_q, dk ** -0.5, 1.0).astype(F32)
        half = dk // 2
        rc = _pick(tm, (ROW_CHUNK,))
        for r0 in range(0, tm, rc):
            acc = jnp.dot(h_ref[:, r0:r0 + rc, :].reshape(bb * rc, K), w_ref[...],
                          preferred_element_type=F32).reshape(bb, rc, tn)
            cos = cos_ref[r0:r0 + rc, :][None] * sc
            sin = sin_ref[r0:r0 + rc, :][None] * sc
            for hh in range(tn // dk):
                lo = hh * dk
                x1 = acc[:, :, lo:lo + half]
                x2 = acc[:, :, lo + half:lo + dk]
                o_ref[:, r0:r0 + rc, lo:lo + half] = (x1 * cos - x2 * sin).astype(o_ref.dtype)
                o_ref[:, r0:r0 + rc, lo + half:lo + dk] = (x1 * sin + x2 * cos).astype(o_ref.dtype)

    @pl.when(j >= n_rot)
    def _():
        acc = jnp.dot(h_ref[...].reshape(bb * tm, K), w_ref[...], preferred_element_type=F32)
        o_ref[...] = acc.reshape(bb, tm, tn).astype(o_ref.dtype)


def _ret_proj(h, w_in, cos, sin, qk_dim, dk):
    B, T, K = h.shape
    N = w_in.shape[1]
    bb, tm = _token_tiles(B, T)
    tn = _pick(qk_dim, (1024, 512, 256))
    half = dk // 2
    return pl.pallas_call(
        functools.partial(_ret_proj_body, n_rot=2 * qk_dim // tn, n_q=qk_dim // tn, dk=dk),
        grid=(B // bb, T // tm, N // tn),
        in_specs=[pl.BlockSpec((bb, tm, K), lambda b, t, j: (b, t, 0)),
                  pl.BlockSpec((K, tn), lambda b, t, j: (0, j)),
                  pl.BlockSpec((tm, half), lambda b, t, j: (t, 0)),
                  pl.BlockSpec((tm, half), lambda b, t, j: (t, 0))],
        out_specs=pl.BlockSpec((bb, tm, tn), lambda b, t, j: (b, t, j)),
        out_shape=jax.ShapeDtypeStruct((B, T, N), BF16),
        compiler_params=_cparams(("parallel", "parallel", "parallel")), name="ret_proj",
    )(h, w_in, cos, sin)


def _ret_body(q_ref, k_ref, v_ref, g_ref, gain_ref, intra_ref, cross_ref, into_ref, cd_ref, s0_ref,
              y_ref, s_ref):
    c = pl.program_id(2)
    L = intra_ref.shape[1]

    @pl.when(c == 0)
    def _():
        s_ref[...] = s0_ref[...]

    for ci in range(q_ref.shape[1] // L):
        rows = slice(ci * L, (ci + 1) * L)
        q = q_ref[0, rows, :]
        k = k_ref[0, rows, :]
        v = v_ref[0, rows, :]
        S = s_ref[0, 0]
        scores = lax.dot_general(q, k, (((1,), (1,)), ((), ())), preferred_element_type=F32) * intra_ref[0]
        o = (jnp.dot(scores.astype(BF16), v, preferred_element_type=F32)
             + jnp.dot(q, S.astype(BF16), preferred_element_type=F32) * cross_ref[0])
        kin = (k.astype(F32) * into_ref[0]).astype(BF16)
        s_ref[0, 0] = S * cd_ref[0] + lax.dot_general(kin, v, (((0,), (0,)), ((), ())),
                                                      preferred_element_type=F32)
        mu = jnp.mean(o, -1, keepdims=True)
        d = o - mu
        var = jnp.mean(d * d, -1, keepdims=True)
        g = g_ref[0, rows, :].astype(F32)
        y_ref[0, rows, :] = (d * lax.rsqrt(var + RET_GN_EPS) * gain_ref[...] * _silu(g)).astype(y_ref.dtype)


def _retention(proj, state, gain, H, dk, dv, n_valid):
    B, T, _ = proj.shape
    L = min(RET_CHUNK, T)
    nv = min(n_valid, L)
    log_g = jnp.log1p(-(2.0 ** (-5.0 - jnp.arange(H, dtype=F32))))
    idx = jnp.arange(L, dtype=F32)
    diff = idx[:, None] - idx[None, :]
    intra = jnp.where(diff >= 0, jnp.exp(log_g[:, None, None] * jnp.maximum(diff, 0.0)), 0.0)
    cross = jnp.exp(log_g[:, None] * (idx[None, :] + 1.0))[:, :, None]
    into = jnp.exp(log_g[:, None] * (nv - 1.0 - idx[None, :]))[:, :, None]
    cd = jnp.exp(log_g * nv)[:, None, None]
    tb = _pick(T, (RET_TBLOCK, L))
    kq = (H * dk) // dk
    vq = (2 * H * dk) // dv
    return pl.pallas_call(
        _ret_body,
        grid=(B, H, T // tb),
        in_specs=[pl.BlockSpec((1, tb, dk), lambda b, h, c: (b, c, h)),
                  pl.BlockSpec((1, tb, dk), lambda b, h, c: (b, c, kq + h)),
                  pl.BlockSpec((1, tb, dv), lambda b, h, c: (b, c, vq + h)),
                  pl.BlockSpec((1, tb, dv), lambda b, h, c: (b, c, vq + H + h)),
                  pl.BlockSpec((1, dv), lambda b, h, c: (0, h)),
                  pl.BlockSpec((1, L, L), lambda b, h, c: (h, 0, 0)),
                  pl.BlockSpec((1, L, 1), lambda b, h, c: (h, 0, 0)),
                  pl.BlockSpec((1, L, 1), lambda b, h, c: (h, 0, 0)),
                  pl.BlockSpec((1, 1, 1), lambda b, h, c: (h, 0, 0)),
                  pl.BlockSpec((1, 1, dk, dv), lambda b, h, c: (b, h, 0, 0))],
        out_specs=[pl.BlockSpec((1, tb, dv), lambda b, h, c: (b, c, h)),
                   pl.BlockSpec((1, 1, dk, dv), lambda b, h, c: (b, h, 0, 0))],
        out_shape=[jax.ShapeDtypeStruct((B, T, H * dv), BF16),
                   jax.ShapeDtypeStruct((B, H, dk, dv), F32)],
        compiler_params=_cparams(("parallel", "parallel", "arbitrary")), name="retention",
    )(proj, proj, proj, proj, gain.reshape(1, H * dv), intra, cross, into, cd, state)


def _ffn_up_body(h_ref, wg_ref, wu_ref, cw_ref, cb_ref, cs_ref, o_ref, tail_ref, ext_ref):
    t = pl.program_id(1)
    f = pl.program_id(2)
    bb, tm, D = h_ref.shape
    tf = wg_ref.shape[1]
    pad = SUBLANE

    @pl.when(t == 0)
    def _():
        tail_ref[f] = cs_ref[...]

    ext_ref[:, pad - 2:pad, :] = tail_ref[f]
    cw = cw_ref[...]
    rc = _pick(tm, (ROW_CHUNK,))
    for r0 in range(0, tm, rc):
        h = h_ref[:, r0:r0 + rc, :].reshape(bb * rc, D)
        u = jnp.dot(h, wg_ref[...], preferred_element_type=F32).reshape(bb, rc, tf)
        up = jnp.dot(h, wu_ref[...], preferred_element_type=F32).reshape(bb, rc, tf)
        ext_ref[:, pad + r0:pad + r0 + rc, :] = u
        conv = (cb_ref[...] + ext_ref[:, pad - 2 + r0:pad - 2 + r0 + rc, :] * cw[0:1]
                + ext_ref[:, pad - 1 + r0:pad - 1 + r0 + rc, :] * cw[1:2] + u * cw[2:3])
        o_ref[:, r0:r0 + rc, :] = (_silu(conv) * up).astype(o_ref.dtype)
    tail_ref[f] = ext_ref[:, pad + tm - 2:pad + tm, :]


def _ffn(h, x, gate, wg, wu, wd, conv_w, conv_b, conv_state):
    B, T, D = x.shape
    Fd = wg.shape[1]
    bb, tm = _token_tiles(B, T)
    tf = _pick(Fd, (512, 256, 128))
    nf = Fd // tf
    act, tails = pl.pallas_call(
        _ffn_up_body,
        grid=(B // bb, T // tm, nf),
        in_specs=[pl.BlockSpec((bb, tm, D), lambda b, t, f: (b, t, 0)),
                  pl.BlockSpec((D, tf), lambda b, t, f: (0, f)),
                  pl.BlockSpec((D, tf), lambda b, t, f: (0, f)),
                  pl.BlockSpec((CONV_W, tf), lambda b, t, f: (0, f)),
                  pl.BlockSpec((1, tf), lambda b, t, f: (0, f)),
                  pl.BlockSpec((bb, CONV_W - 1, tf), lambda b, t, f: (b, 0, f))],
        out_specs=[pl.BlockSpec((bb, tm, tf), lambda b, t, f: (b, t, f)),
                   pl.BlockSpec((nf, bb, CONV_W - 1, tf), lambda b, t, f: (0, b, 0, 0))],
        out_shape=[jax.ShapeDtypeStruct((B, T, Fd), BF16),
                   jax.ShapeDtypeStruct((nf, B, CONV_W - 1, tf), F32)],
        scratch_shapes=[pltpu.VMEM((bb, tm + SUBLANE, tf), F32)],
        compiler_params=_cparams(("parallel", "arbitrary", "arbitrary")), name="ffn_up",
    )(h, wg, wu, conv_w, conv_b.reshape(1, Fd), conv_state)
    y = _mm_res(act, wd, x, gate, name="ffn_down")
    return y, tails.transpose(1, 2, 0, 3).reshape(B, CONV_W - 1, Fd)


def _rwkv_mix_body(x_ref, sh_ref, sc_ref, mu_ref, st_ref, w1_ref, a1_ref, g1_ref,
                   xr_ref, xk_ref, xv_ref, tw_ref, ta_ref, tg_ref, last_ref, hbuf_ref, carry_ref):
    t = pl.program_id(1)
    tm = x_ref.shape[1]
    pad = SUBLANE
    x = x_ref[0]
    xn = x * lax.rsqrt(jnp.mean(x * x, -1, keepdims=True) + NORM_EPS)
    h = xn * (1.0 + sc_ref[0]) + sh_ref[0]

    @pl.when(t == 0)
    def _():
        carry_ref[...] = st_ref[0]

    hbuf_ref[pad - 1:pad, :] = carry_ref[...]
    hbuf_ref[pad:, :] = h
    xx = hbuf_ref[pad - 1:pad - 1 + tm, :] - h
    def mix(n):
        return (h + xx * mu_ref[n:n + 1, :]).astype(BF16)

    xr_ref[0] = mix(0)
    xk_ref[0] = mix(2)
    xv_ref[0] = mix(3)
    tw_ref[0] = jnp.tanh(jnp.dot(mix(1), w1_ref[...], preferred_element_type=F32)).astype(tw_ref.dtype)
    ta_ref[0] = jnp.dot(mix(4), a1_ref[...], preferred_element_type=F32).astype(ta_ref.dtype)
    tg_ref[0] = jax.nn.sigmoid(jnp.dot(mix(5), g1_ref[...], preferred_element_type=F32)).astype(tg_ref.dtype)
    last = hbuf_ref[pad + tm - 1:pad + tm, :]
    carry_ref[...] = last
    last_ref[0] = last


def _rwkv_mix(x, shift, scale, mu, shift_state, w1, a1, g1):
    B, T, D = x.shape
    tm = _pick(T, (512, 256, 128))
    xs = pl.BlockSpec((1, tm, D), lambda b, t: (b, t, 0))
    ms = pl.BlockSpec((1, 1, D), lambda b, t: (b, 0, 0))
    loras = (w1, a1, g1)
    outs = pl.pallas_call(
        _rwkv_mix_body, grid=(B, T // tm),
        in_specs=[xs, ms, ms, pl.BlockSpec((6, D), lambda b, t: (0, 0)), ms]
        + [pl.BlockSpec(w.shape, lambda b, t: (0, 0)) for w in loras],
        out_specs=[xs] * 3 + [pl.BlockSpec((1, tm, w.shape[1]), lambda b, t: (b, t, 0)) for w in loras] + [ms],
        out_shape=[jax.ShapeDtypeStruct((B, T, D), BF16)] * 3
        + [jax.ShapeDtypeStruct((B, T, w.shape[1]), BF16) for w in loras]
        + [jax.ShapeDtypeStruct((B, 1, D), F32)],
        scratch_shapes=[pltpu.VMEM((tm + SUBLANE, D), F32), pltpu.VMEM((1, D), F32)],
        compiler_params=_cparams(("parallel", "arbitrary")), name="rwkv_mix",
    )(x, shift, scale, mu, shift_state.reshape(B, 1, D), *loras)
    return outs[:6], outs[6].reshape(B, D)


def _dot(a, b):
    return jnp.dot(a.astype(BF16), b.astype(BF16), preferred_element_type=F32)


def _dot_nt(a, b):
    return lax.dot_general(a.astype(BF16), b.astype(BF16), (((1,), (1,)), ((), ())),
                           preferred_element_type=F32)


def _dot_tn(a, b):
    return lax.dot_general(a.astype(BF16), b.astype(BF16), (((0,), (0,)), ((), ())),
                           preferred_element_type=F32)


def _rwkv_rec_body(r_ref, k_ref, v_ref, wl_ref, al_ref, g_ref, par_ref, s0_ref, y_ref, s_ref,
                   *, L, npair, group, n_valid):
    c = pl.program_id(1)
    N = RWKV_HEAD
    P = 2 * N

    lane_l = lax.broadcasted_iota(jnp.int32, (L, P), 1)
    head0_l = lane_l < N
    r2 = lax.broadcasted_iota(jnp.int32, (2 * L, 2 * L), 0)
    c2 = lax.broadcasted_iota(jnp.int32, (2 * L, 2 * L), 1)
    same = (r2 >= L) == (c2 >= L)
    strict = same & (c2 < r2)
    incl = same & (c2 <= r2)
    rowhead = (lax.broadcasted_iota(jnp.int32, (2 * L, P), 0) >= L) == \
              (lax.broadcasted_iota(jnp.int32, (2 * L, P), 1) >= N)
    rl = lax.broadcasted_iota(jnp.int32, (L, L), 0)
    cl_ = lax.broadcasted_iota(jnp.int32, (L, L), 1)
    tri = (cl_ <= rl).astype(BF16)
    ip = lax.broadcasted_iota(jnp.int32, (P, P), 0)
    jp = lax.broadcasted_iota(jnp.int32, (P, P), 1)
    blockdiag = (ip >= N) == (jp >= N)
    seg = blockdiag.astype(BF16)
    live = lax.broadcasted_iota(jnp.int32, (L, P), 0) < n_valid

    def segsum(x):
        return jnp.dot(x.astype(BF16), seg, preferred_element_type=F32)

    def cumsum_rows(x):
        h1 = x.astype(BF16)
        r1 = x - h1.astype(F32)
        h2 = r1.astype(BF16)
        h3 = (r1 - h2.astype(F32)).astype(BF16)
        return (jnp.dot(tri, h1, preferred_element_type=F32) + jnp.dot(tri, h2, preferred_element_type=F32)
                + jnp.dot(tri, h3, preferred_element_type=F32))

    def stack_heads(x):
        return jnp.concatenate([jnp.where(head0_l, x, 0.0), jnp.where(head0_l, 0.0, x)], axis=0)

    def fold_heads(x2):
        return x2[:L] + x2[L:]

    def twice(x):
        return jnp.concatenate([x, x], axis=0)

    @pl.when(c == 0)
    def _():
        s_ref[...] = s0_ref[...]

    def segsum_all(xs):
        out = segsum(jnp.concatenate(xs, axis=0))
        return [out[i * L:(i + 1) * L] for i in range(len(xs))]

    for g0 in range(0, npair, group):
        pairs = range(g0, min(g0 + group, npair))
        lanes = [slice(p * P, (p + 1) * P) for p in pairs]
        n = len(lanes)
        par = [par_ref[:, ln] for ln in lanes]
        r = [r_ref[0, :, ln] for ln in lanes]
        k = [k_ref[0, :, ln] for ln in lanes]
        v = [v_ref[0, :, ln] for ln in lanes]
        lnd, a, kkraw, kmod = [], [], [], []
        for i in range(n):
            w0, a0, k_k, k_a = (par[i][j:j + 1] for j in range(4))
            z = -(w0 + wl_ref[0, :, lanes[i]])
            softplus = jnp.maximum(z, 0.0) + jnp.log1p(jnp.exp(-jnp.abs(z)))
            lnd.append(-jnp.exp(-softplus - 0.5))
            a.append(jax.nn.sigmoid(a0 + al_ref[0, :, lanes[i]]))
            kkraw.append(k[i] * k_k)
            kmod.append(k[i] * (1.0 + (a[i] - 1.0) * k_a))
        sums = segsum_all([x * x for x in kkraw] + [r[i] * kmod[i] * par[i][4:5] for i in range(n)])
        kk = [kkraw[i] / jnp.maximum(jnp.sqrt(sums[i]), 1e-12) for i in range(n)]
        rk = sums[n:]
        if n_valid < L:
            lnd = [jnp.where(live, x, 0.0) for x in lnd]
            kk = [jnp.where(live, x, 0.0) for x in kk]
            kmod = [jnp.where(live, x, 0.0) for x in kmod]
            v = [jnp.where(live, x, 0.0) for x in v]
        cum = [cumsum_rows(x) for x in lnd]
        beta, kappa, rho, a2, lhs = [], [], [], [], []
        for i in range(n):
            e_neg = jnp.exp(-cum[i])
            beta.append((kk[i] * a[i] * e_neg).astype(BF16))
            kappa.append((kmod[i] * e_neg).astype(BF16))
            rho.append(r[i] * jnp.exp(cum[i]))
            a2.append(stack_heads(kk[i] * jnp.exp(cum[i] - lnd[i])))
            lhs.append(jnp.concatenate([a2[i], stack_heads(rho[i])], axis=0).astype(BF16))
        gb = [_dot_nt(lhs[i], twice(beta[i])) for i in range(n)]
        gk = [_dot_nt(lhs[i], twice(kappa[i])) for i in range(n)]
        nmat = [jnp.where(strict, x[:2 * L], 0.0) for x in gb]
        pb = [jnp.where(incl, x[2 * L:], 0.0).astype(BF16) for x in gb]
        v2 = [twice(x).astype(BF16) for x in v]
        akv = [jnp.where(rowhead, _dot(jnp.where(strict, gk[i][:2 * L], 0.0), v2[i]), 0.0)
               for i in range(n)]
        pkv = [_dot(jnp.where(incl, gk[i][2 * L:], 0.0), v2[i]) for i in range(n)]

        corr = [-x for x in nmat]
        pw = [x.astype(BF16) for x in nmat]
        for _ in range(max(L.bit_length() - 2, 0)):
            sq = [_dot(x, x) for x in pw]
            pw = [x.astype(BF16) for x in sq]
            corr = [corr[i] + sq[i] + _dot(corr[i], pw[i]) for i in range(n)]

        both = [jnp.concatenate([a2[i], akv[i]], axis=1) for i in range(n)]
        both = [both[i] + _dot(corr[i], both[i]) for i in range(n)]
        at = [fold_heads(x[:, :P]) for x in both]
        w0_ = [fold_heads(x[:, P:]) for x in both]
        pbx = [_dot(pb[i], twice(jnp.concatenate([at[i], w0_[i]], axis=1))) for i in range(n)]
        rho_t = [rho[i] - fold_heads(jnp.where(rowhead, pbx[i][:, :P], 0.0)) for i in range(n)]
        o0 = [fold_heads(jnp.where(rowhead, pkv[i] - pbx[i][:, P:], 0.0)) for i in range(n)]
        gmat = [jnp.where(blockdiag, _dot_tn(at[i], beta[i]), 0.0) for i in range(n)]
        umat = [jnp.where(blockdiag, _dot_tn(jnp.concatenate([v[i], -w0_[i]], axis=0),
                                             jnp.concatenate([kappa[i], beta[i]], axis=0)), 0.0)
                for i in range(n)]

        S = [s_ref[0, p] for p in pairs]
        Sb = [x.astype(BF16) for x in S]
        o = [_dot_nt(rho_t[i], Sb[i]) + o0[i] for i in range(n)]
        for i, p in enumerate(pairs):
            s_ref[0, p] = (S[i] - _dot(Sb[i], gmat[i]) + umat[i]) * jnp.exp(cum[i][L - 1:L, :])

        mu = segsum_all(o)
        d = [o[i] - mu[i] * (1.0 / N) for i in range(n)]
        var = segsum_all([x * x for x in d])
        for i in range(n):
            yn = d[i] * lax.rsqrt(var[i] * (1.0 / N) + RWKV_GN_EPS) * par[i][5:6]
            y_ref[0, :, lanes[i]] = ((yn + rk[i] * v[i]) * g_ref[0, :, lanes[i]]).astype(y_ref.dtype)


def _rwkv_rec(r, k, v, wl, al, g, par, s0_blk, n_valid):
    B, T, D = r.shape
    P = 2 * RWKV_HEAD
    L = RWKV_CHUNK
    assert T % L == 0 and (n_valid == T or T == L)
    npair = D // P
    ts = pl.BlockSpec((1, L, D), lambda b, c: (b, c, 0))
    ss = pl.BlockSpec((1, npair, P, P), lambda b, c: (b, 0, 0, 0))
    return pl.pallas_call(
        functools.partial(_rwkv_rec_body, L=L, npair=npair, group=RWKV_GROUP, n_valid=min(n_valid, L)),
        grid=(B, T // L),
        in_specs=[ts] * 6 + [pl.BlockSpec((SUBLANE, D), lambda b, c: (0, 0)), ss],
        out_specs=[ts, ss],
        out_shape=[jax.ShapeDtypeStruct((B, T, D), BF16),
                   jax.ShapeDtypeStruct((B, npair, P, P), F32)],
        compiler_params=_cparams(("parallel", "arbitrary")), name="rwkv_rec",
    )(r, k, v, wl, al, g, par, s0_blk)


def _pair_blockdiag(s):
    B, H, N, _ = s.shape
    s = s.reshape(B, H // 2, 2, N, N)
    z = jnp.zeros_like(s[:, :, 0])
    top = jnp.concatenate([s[:, :, 0], z], axis=-1)
    bot = jnp.concatenate([z, s[:, :, 1]], axis=-1)
    return jnp.concatenate([top, bot], axis=-2)


def _pair_unblock(sb):
    B, Pn, P, _ = sb.shape
    N = P // 2
    return jnp.stack([sb[:, :, :N, :N], sb[:, :, N:, N:]], axis=2).reshape(B, 2 * Pn, N, N)


def _pad_time(a, T2):
    return jnp.pad(a, ((0, 0), (0, T2 - a.shape[1]), (0, 0)))


def _retention_layer(x, mods, pos0, state, w_in, w_out, gn_gain):
    B, T, D = x.shape
    H, dk, dv = state.shape[1:]
    shift_m, scale_m, gate_m = mods
    h = _norm_mod(x, shift_m, scale_m, BF16)
    half = dk // 2
    inv = ROPE_BASE ** (-jnp.arange(half, dtype=F32) / half)
    ang = (pos0 + jnp.arange(T)).astype(F32)[:, None] * inv[None, :]
    proj = _ret_proj(h, w_in, jnp.cos(ang), jnp.sin(ang), H * dk, dk)
    Tp = T if T % RET_CHUNK == 0 else -(-T // 64) * 64
    assert Tp == T or Tp <= RET_CHUNK
    y, s_new = _retention(_pad_time(proj, Tp), state, gn_gain, H, dk, dv, T)
    x = _mm_res(y[:, :T], w_out, x, gate_m, name="ret_out")
    return x, s_new


def _rwkv_layer(x, mods, shift_state, wkv_state, p):
    B, T, D = x.shape
    shift_m, scale_m, gate_m = mods
    (xr, xk, xv, tw, ta, tg), last = _rwkv_mix(x, shift_m, scale_m, p["mu"], shift_state,
                                               p["w1"], p["a1"], p["g1"])
    r = _mm(xr, p["w_r"], F32, name="rwkv_r")
    k = _mm(xk, p["w_k"], F32, name="rwkv_k")
    v = _mm(xv, p["w_v"], F32, name="rwkv_v")
    wl = _mm(tw, p["w2"], F32, name="rwkv_w2")
    al = _mm(ta, p["a2"], F32, name="rwkv_a2")
    g = _mm(tg, p["g2"], F32, name="rwkv_g2")
    Tp = -(-T // RWKV_CHUNK) * RWKV_CHUNK
    ins = [_pad_time(t, Tp) for t in (r, k, v, wl, al, g)]
    y, s_blk = _rwkv_rec(*ins, p["par"], _pair_blockdiag(wkv_state), T)
    x = _mm_res(y[:, :T], p["w_o"], x, gate_m, name="rwkv_out")
    return x, _pair_unblock(s_blk), last


def _pad_cols(w, n):
    return jnp.pad(w, ((0, 0), (0, n - w.shape[1])))


def _pad_rows(w, n):
    return jnp.pad(w, ((0, n - w.shape[0]), (0, 0)))


def _run_group(x, mod, pos0, st_ret, st_wkv, st_shift, st_conv, w):
    depth = mod.shape[0]
    new_ret, new_wkv, new_shift, new_conv = [], [], [], []
    for i in range(depth):
        m = [mod[i, :, n][:, None, :] for n in range(6)]
        j = i // 2
        if i % 2 == 0:
            x, s = _retention_layer(x, m[:3], pos0, st_ret[j], w["ret_w_in"][j], w["ret_w_out"][j],
                                    w["ret_gn_gain"][j])
            new_ret.append(s)
        else:
            x, s, last = _rwkv_layer(x, m[:3], st_shift[j], st_wkv[j], w["rwkv"][j])
            new_wkv.append(s)
            new_shift.append(last)
        h = _norm_mod(x, m[3], m[4], BF16)
        x, cs = _ffn(h, x, m[5], w["ffn_w_gate"][i], w["ffn_w_up"][i], w["ffn_w_down"][i],
                     w["ffn_conv_w"][i], w["ffn_conv_b"][i], st_conv[i])
        new_conv.append(cs)
    out = _norm_gain(x, w["final_gain"])
    return out, jnp.stack(new_ret), jnp.stack(new_wkv), jnp.stack(new_shift), jnp.stack(new_conv)


def kernel(x_prompt, x_sample, c_prompt, c_sample, state_ret, state_rwkv_wkv, state_rwkv_shift, state_ffn_conv, ada_w, ada_b, ret_w_in, ret_w_out, ret_gn_gain, rwkv_mu, rwkv_w_r, rwkv_w_k, rwkv_w_v, rwkv_w_o, rwkv_w0, rwkv_w1, rwkv_w2, rwkv_a0, rwkv_a1, rwkv_a2, rwkv_g1, rwkv_g2, rwkv_k_k, rwkv_k_a, rwkv_r_k, rwkv_gn_gain, ffn_w_gate, ffn_w_up, ffn_conv_w, ffn_conv_b, ffn_w_down, final_gain):
    B, T, D = x_prompt.shape
    Bs = x_sample.shape[0]
    depth = ada_w.shape[0]
    n_rwkv = rwkv_mu.shape[0]

    rows = -(-(B + Bs) // SUBLANE) * SUBLANE
    c_all = jnp.pad(jnp.concatenate([c_prompt, c_sample], axis=0), ((0, rows - B - Bs), (0, 0)))
    mod = _ada(c_all, ada_w, ada_b).reshape(depth, rows, 6, D)

    bf = lambda a: a.astype(BF16)
    rwkv = []
    for j in range(n_rwkv):
        lw = -(-rwkv_w1.shape[2] // LANE) * LANE
        la = -(-rwkv_a1.shape[2] // LANE) * LANE
        par = jnp.stack([rwkv_w0[j], rwkv_a0[j], rwkv_k_k[j], rwkv_k_a[j], rwkv_r_k[j].reshape(D),
                         rwkv_gn_gain[j], jnp.zeros((D,), F32), jnp.zeros((D,), F32)])
        rwkv.append(dict(
            mu=rwkv_mu[j], w_r=bf(rwkv_w_r[j]), w_k=bf(rwkv_w_k[j]), w_v=bf(rwkv_w_v[j]),
            w_o=bf(rwkv_w_o[j]),
            w1=bf(_pad_cols(rwkv_w1[j], lw)), w2=bf(_pad_rows(rwkv_w2[j], lw)),
            a1=bf(_pad_cols(rwkv_a1[j], la)), a2=bf(_pad_rows(rwkv_a2[j], la)),
            g1=bf(rwkv_g1[j]), g2=bf(rwkv_g2[j]), par=par))
    per_layer = lambda a: [bf(a[i]) for i in range(a.shape[0])]
    w = dict(ret_w_in=per_layer(ret_w_in), ret_w_out=per_layer(ret_w_out), ret_gn_gain=ret_gn_gain,
             rwkv=rwkv, ffn_w_gate=per_layer(ffn_w_gate), ffn_w_up=per_layer(ffn_w_up),
             ffn_w_down=per_layer(ffn_w_down), ffn_conv_w=ffn_conv_w, ffn_conv_b=ffn_conv_b,
             final_gain=final_gain)

    n_ret = state_ret.shape[0]
    F_ = ffn_w_gate.shape[2]
    z_ret = jnp.zeros((n_ret, B) + state_ret.shape[2:], F32)
    z_wkv = jnp.zeros((n_rwkv, B) + state_rwkv_wkv.shape[2:], F32)
    z_shift = jnp.zeros((n_rwkv, B, D), F32)
    z_conv = jnp.zeros((depth, B, CONV_W - 1, F_), F32)
    y_p, p_ret, p_wkv, p_shift, p_conv = _run_group(
        x_prompt, mod[:, :B], 0, z_ret, z_wkv, z_shift, z_conv, w)
    y_s, s_ret, s_wkv, s_shift, s_conv = _run_group(
        x_sample, mod[:, B:B + Bs], PAST_LEN, state_ret, state_rwkv_wkv, state_rwkv_shift,
        state_ffn_conv, w)
    return (y_p, y_s, p_ret, p_wkv, p_shift, p_conv, s_ret, s_wkv, s_shift, s_conv)
```

```python
import functools

import jax
import jax.numpy as jnp
from jax import lax
from jax.experimental import pallas as pl
from jax.experimental.pallas import tpu as pltpu

F32 = jnp.float32
BF16 = jnp.bfloat16

NORM_EPS = 1e-6
RET_GN_EPS = 1e-5
RWKV_GN_EPS = 64e-5
ROPE_BASE = 10000.0
PAST_LEN = 4096
CONV_W = 3

LANE = 128
SUBLANE = 8
VMEM_LIMIT_MB = 56

RET_CHUNK = 256
ROW_CHUNK = 256
RET_TBLOCK = 1024
RWKV_CHUNK = 64
RWKV_HEAD = 64
RWKV_GROUP = 16


def _pick(n, cands):
    for c in cands:
        if n % c == 0:
            return c
    return n


def _cparams(sem):
    return pltpu.CompilerParams(dimension_semantics=sem, vmem_limit_bytes=VMEM_LIMIT_MB << 20)


def _token_tiles(B, T):
    if T >= 512:
        return 1, _pick(T, (1024, 512, 256, 128))
    return B, T


def _silu(x):
    return x * jax.nn.sigmoid(x)


def _ada_body(c_ref, w_ref, b_ref, o_ref):
    c = c_ref[...]
    s = _silu(c).astype(BF16)
    o_ref[0] = jnp.dot(s, w_ref[0].astype(BF16), preferred_element_type=F32) + b_ref[0]


def _ada(c_all, ada_w, ada_b):
    depth, D, N = ada_w.shape
    R = c_all.shape[0]
    tn = _pick(N, (1024, 512, 256, 128))
    return pl.pallas_call(
        _ada_body,
        grid=(depth, N // tn),
        in_specs=[pl.BlockSpec((R, D), lambda l, j: (0, 0)),
                  pl.BlockSpec((1, D, tn), lambda l, j: (l, 0, j)),
                  pl.BlockSpec((1, 1, tn), lambda l, j: (l, 0, j))],
        out_specs=pl.BlockSpec((1, R, tn), lambda l, j: (l, 0, j)),
        out_shape=jax.ShapeDtypeStruct((depth, R, N), F32),
        compiler_params=_cparams(("parallel", "parallel")),
        name="ada_mod",
    )(c_all, ada_w, ada_b.reshape(depth, 1, N))


def _norm_mod_body(x_ref, sh_ref, sc_ref, o_ref):
    x = x_ref[...]
    xn = x * lax.rsqrt(jnp.mean(x * x, -1, keepdims=True) + NORM_EPS)
    o_ref[...] = (xn * (1.0 + sc_ref[...]) + sh_ref[...]).astype(o_ref.dtype)


def _norm_gain_body(x_ref, g_ref, o_ref):
    x = x_ref[...]
    xn = x * lax.rsqrt(jnp.mean(x * x, -1, keepdims=True) + NORM_EPS)
    o_ref[...] = (xn * g_ref[...]).astype(o_ref.dtype)


def _norm_mod(x, shift, scale, out_dtype):
    B, T, D = x.shape
    bb, tm = _token_tiles(B, T)
    xs = pl.BlockSpec((bb, tm, D), lambda b, t: (b, t, 0))
    ms = pl.BlockSpec((bb, 1, D), lambda b, t: (b, 0, 0))
    return pl.pallas_call(
        _norm_mod_body, grid=(B // bb, T // tm), in_specs=[xs, ms, ms], out_specs=xs,
        out_shape=jax.ShapeDtypeStruct((B, T, D), out_dtype),
        compiler_params=_cparams(("parallel", "parallel")), name="norm_mod",
    )(x, shift, scale)


def _norm_gain(x, gain):
    B, T, D = x.shape
    bb, tm = _token_tiles(B, T)
    xs = pl.BlockSpec((bb, tm, D), lambda b, t: (b, t, 0))
    return pl.pallas_call(
        _norm_gain_body, grid=(B // bb, T // tm),
        in_specs=[xs, pl.BlockSpec((1, 1, D), lambda b, t: (0, 0, 0))], out_specs=xs,
        out_shape=jax.ShapeDtypeStruct((B, T, D), F32),
        compiler_params=_cparams(("parallel", "parallel")), name="final_norm",
    )(x, gain.reshape(1, 1, D))


def _mm2_body(x_ref, w_ref, x2_ref, w2_ref, o_ref, o2_ref):
    bb, tm, K = x_ref.shape
    acc = jnp.dot(x_ref[...].reshape(bb * tm, K), w_ref[...], preferred_element_type=F32)
    o_ref[...] = acc.reshape(bb, tm, -1).astype(o_ref.dtype)
    acc2 = jnp.dot(x2_ref[...].reshape(bb * tm, x2_ref.shape[2]), w2_ref[...], preferred_element_type=F32)
    o2_ref[...] = acc2.reshape(bb, tm, -1).astype(o2_ref.dtype)


def _mm2(x, w, x2, w2, out_dtype, out2_dtype, name="mm2"):
    B, T, K = x.shape
    K2 = x2.shape[2]
    N = w.shape[1]
    assert w2.shape[1] == N
    bb, tm = _token_tiles(B, T)
    tn = _pick(N, (1024, 512, 256, 128))
    os_ = pl.BlockSpec((bb, tm, tn), lambda b, t, j: (b, t, j))
    return pl.pallas_call(
        _mm2_body,
        grid=(B // bb, T // tm, N // tn),
        in_specs=[pl.BlockSpec((bb, tm, K), lambda b, t, j: (b, t, 0)),
                  pl.BlockSpec((K, tn), lambda b, t, j: (0, j)),
                  pl.BlockSpec((bb, tm, K2), lambda b, t, j: (b, t, 0)),
                  pl.BlockSpec((K2, tn), lambda b, t, j: (0, j))],
        out_specs=[os_, os_],
        out_shape=[jax.ShapeDtypeStruct((B, T, N), out_dtype), jax.ShapeDtypeStruct((B, T, N), out2_dtype)],
        compiler_params=_cparams(("parallel", "parallel", "parallel")), name=name,
    )(x, w, x2, w2)


def _mm_res_body(y_ref, w_ref, x_ref, gate_ref, o_ref):
    bb, tm, K = y_ref.shape
    acc = jnp.dot(y_ref[...].reshape(bb * tm, K), w_ref[...], preferred_element_type=F32)
    o_ref[...] = x_ref[...] + gate_ref[...] * acc.reshape(bb, tm, -1)


def _mm_res(y, w, x, gate, name="mm_res"):
    B, T, K = y.shape
    N = w.shape[1]
    bb, tm = _token_tiles(B, T)
    tn = _pick(N, (512, 256, 128))
    return pl.pallas_call(
        _mm_res_body,
        grid=(B // bb, T // tm, N // tn),
        in_specs=[pl.BlockSpec((bb, tm, K), lambda b, t, j: (b, t, 0)),
                  pl.BlockSpec((K, tn), lambda b, t, j: (0, j)),
                  pl.BlockSpec((bb, tm, tn), lambda b, t, j: (b, t, j)),
                  pl.BlockSpec((bb, 1, tn), lambda b, t, j: (b, 0, j))],
        out_specs=pl.BlockSpec((bb, tm, tn), lambda b, t, j: (b, t, j)),
        out_shape=jax.ShapeDtypeStruct((B, T, N), F32),
        compiler_params=_cparams(("parallel", "parallel", "parallel")), name=name,
    )(y, w, x, gate)


def _ret_proj_body(h_ref, w_ref, cos_ref, sin_ref, o_ref, *, n_rot, n_q, dk):
    j = pl.program_id(2)
    bb, tm, K = h_ref.shape
    tn = w_ref.shape[1]

    @pl.when(j < n_rot)
    def _():
        sc = jnp.where(j >= n_q, dk ** -0.5, 1.0).astype(F32)
        half = dk // 2
        rc = _pick(tm, (ROW_CHUNK,))
        for r0 in range(0, tm, rc):
            acc = jnp.dot(h_ref[:, r0:r0 + rc, :].reshape(bb * rc, K), w_ref[...],
                          preferred_element_type=F32).reshape(bb, rc, tn)
            cos = cos_ref[r0:r0 + rc, :][None] * sc
            sin = sin_ref[r0:r0 + rc, :][None] * sc
            for hh in range(tn // dk):
                lo = hh * dk
                x1 = acc[:, :, lo:lo + half]
                x2 = acc[:, :, lo + half:lo + dk]
                o_ref[:, r0:r0 + rc, lo:lo + half] = (x1 * cos - x2 * sin).astype(o_ref.dtype)
                o_ref[:, r0:r0 + rc, lo + half:lo + dk] = (x1 * sin + x2 * cos).astype(o_ref.dtype)

    @pl.when(j >= n_rot)
    def _():
        acc = jnp.dot(h_ref[...].reshape(bb * tm, K), w_ref[...], preferred_element_type=F32)
        o_ref[...] = acc.reshape(bb, tm, tn).astype(o_ref.dtype)


def _ret_proj(h, w_in, cos, sin, qk_dim, dk):
    B, T, K = h.shape
    N = w_in.shape[1]
    bb, tm = _token_tiles(B, T)
    tn = _pick(qk_dim, (1024, 512, 256))
    half = dk // 2
    return pl.pallas_call(
        functools.partial(_ret_proj_body, n_rot=2 * qk_dim // tn, n_q=qk_dim // tn, dk=dk),
        grid=(B // bb, T // tm, N // tn),
        in_specs=[pl.BlockSpec((bb, tm, K), lambda b, t, j: (b, t, 0)),
                  pl.BlockSpec((K, tn), lambda b, t, j: (0, j)),
                  pl.BlockSpec((tm, half), lambda b, t, j: (t, 0)),
                  pl.BlockSpec((tm, half), lambda b, t, j: (t, 0))],
        out_specs=pl.BlockSpec((bb, tm, tn), lambda b, t, j: (b, t, j)),
        out_shape=jax.ShapeDtypeStruct((B, T, N), BF16),
        compiler_params=_cparams(("parallel", "parallel", "parallel")), name="ret_proj",
    )(h, w_in, cos, sin)


def _ret_body(q_ref, k_ref, v_ref, g_ref, gain_ref, intra_ref, cross_ref, into_ref, cd_ref, s0_ref,
              y_ref, s_ref):
    c = pl.program_id(2)
    L = intra_ref.shape[1]

    @pl.when(c == 0)
    def _():
        s_ref[...] = s0_ref[...]

    chunks = [slice(ci * L, (ci + 1) * L) for ci in range(q_ref.shape[1] // L)]
    q = [q_ref[0, rows, :] for rows in chunks]
    v = [v_ref[0, rows, :] for rows in chunks]
    scores = [(lax.dot_general(q[i], k_ref[0, rows, :], (((1,), (1,)), ((), ())),
                               preferred_element_type=F32) * intra_ref[0]).astype(BF16)
              for i, rows in enumerate(chunks)]
    kv = [lax.dot_general((k_ref[0, rows, :].astype(F32) * into_ref[0]).astype(BF16), v[i],
                          (((0,), (0,)), ((), ())), preferred_element_type=F32)
          for i, rows in enumerate(chunks)]
    pv = [jnp.dot(scores[i], v[i], preferred_element_type=F32) for i in range(len(chunks))]
    for i, rows in enumerate(chunks):
        S = s_ref[0, 0]
        o = pv[i] + jnp.dot(q[i], S.astype(BF16), preferred_element_type=F32) * cross_ref[0]
        s_ref[0, 0] = S * cd_ref[0] + kv[i]
        mu = jnp.mean(o, -1, keepdims=True)
        d = o - mu
        var = jnp.mean(d * d, -1, keepdims=True)
        g = g_ref[0, rows, :].astype(F32)
        y_ref[0, rows, :] = (d * lax.rsqrt(var + RET_GN_EPS) * gain_ref[...] * _silu(g)).astype(y_ref.dtype)


def _retention(proj, state, gain, H, dk, dv, n_valid):
    B, T, _ = proj.shape
    L = min(RET_CHUNK, T)
    nv = min(n_valid, L)
    log_g = jnp.log1p(-(2.0 ** (-5.0 - jnp.arange(H, dtype=F32))))
    idx = jnp.arange(L, dtype=F32)
    diff = idx[:, None] - idx[None, :]
    intra = jnp.where(diff >= 0, jnp.exp(log_g[:, None, None] * jnp.maximum(diff, 0.0)), 0.0)
    cross = jnp.exp(log_g[:, None] * (idx[None, :] + 1.0))[:, :, None]
    into = jnp.exp(log_g[:, None] * (nv - 1.0 - idx[None, :]))[:, :, None]
    cd = jnp.exp(log_g * nv)[:, None, None]
    tb = _pick(T, (RET_TBLOCK, L))
    kq = (H * dk) // dk
    vq = (2 * H * dk) // dv
    return pl.pallas_call(
        _ret_body,
        grid=(B, H, T // tb),
        in_specs=[pl.BlockSpec((1, tb, dk), lambda b, h, c: (b, c, h)),
                  pl.BlockSpec((1, tb, dk), lambda b, h, c: (b, c, kq + h)),
                  pl.BlockSpec((1, tb, dv), lambda b, h, c: (b, c, vq + h)),
                  pl.BlockSpec((1, tb, dv), lambda b, h, c: (b, c, vq + H + h)),
                  pl.BlockSpec((1, dv), lambda b, h, c: (0, h)),
                  pl.BlockSpec((1, L, L), lambda b, h, c: (h, 0, 0)),
                  pl.BlockSpec((1, L, 1), lambda b, h, c: (h, 0, 0)),
                  pl.BlockSpec((1, L, 1), lambda b, h, c: (h, 0, 0)),
                  pl.BlockSpec((1, 1, 1), lambda b, h, c: (h, 0, 0)),
                  pl.BlockSpec((1, 1, dk, dv), lambda b, h, c: (b, h, 0, 0))],
        out_specs=[pl.BlockSpec((1, tb, dv), lambda b, h, c: (b, c, h)),
                   pl.BlockSpec((1, 1, dk, dv), lambda b, h, c: (b, h, 0, 0))],
        out_shape=[jax.ShapeDtypeStruct((B, T, H * dv), BF16),
                   jax.ShapeDtypeStruct((B, H, dk, dv), F32)],
        compiler_params=_cparams(("parallel", "parallel", "arbitrary")), name="retention",
    )(proj, proj, proj, proj, gain.reshape(1, H * dv), intra, cross, into, cd, state)


def _ffn_up_body(h_ref, wg_ref, wu_ref, cw_ref, cb_ref, cs_ref, o_ref, tail_ref, ext_ref):
    t = pl.program_id(1)
    f = pl.program_id(2)
    bb, tm, D = h_ref.shape
    tf = wg_ref.shape[1]
    pad = SUBLANE

    @pl.when(t == 0)
    def _():
        tail_ref[f] = cs_ref[...]

    ext_ref[:, pad - 2:pad, :] = tail_ref[f]
    cw = cw_ref[...]
    rc = _pick(tm, (ROW_CHUNK,))
    for r0 in range(0, tm, rc):
        h = h_ref[:, r0:r0 + rc, :].reshape(bb * rc, D)
        u = jnp.dot(h, wg_ref[...], preferred_element_type=F32).reshape(bb, rc, tf)
        up = jnp.dot(h, wu_ref[...], preferred_element_type=F32).reshape(bb, rc, tf)
        ext_ref[:, pad + r0:pad + r0 + rc, :] = u
        conv = (cb_ref[...] + ext_ref[:, pad - 2 + r0:pad - 2 + r0 + rc, :] * cw[0:1]
                + ext_ref[:, pad - 1 + r0:pad - 1 + r0 + rc, :] * cw[1:2] + u * cw[2:3])
        o_ref[:, r0:r0 + rc, :] = (_silu(conv) * up).astype(o_ref.dtype)
    tail_ref[f] = ext_ref[:, pad + tm - 2:pad + tm, :]


def _ffn(h, x, gate, wg, wu, wd, conv_w, conv_b, conv_state):
    B, T, D = x.shape
    Fd = wg.shape[1]
    bb, tm = _token_tiles(B, T)
    tf = _pick(Fd, (512, 256, 128))
    nf = Fd // tf
    act, tails = pl.pallas_call(
        _ffn_up_body,
        grid=(B // bb, T // tm, nf),
        in_specs=[pl.BlockSpec((bb, tm, D), lambda b, t, f: (b, t, 0)),
                  pl.BlockSpec((D, tf), lambda b, t, f: (0, f)),
                  pl.BlockSpec((D, tf), lambda b, t, f: (0, f)),
                  pl.BlockSpec((CONV_W, tf), lambda b, t, f: (0, f)),
                  pl.BlockSpec((1, tf), lambda b, t, f: (0, f)),
                  pl.BlockSpec((bb, CONV_W - 1, tf), lambda b, t, f: (b, 0, f))],
        out_specs=[pl.BlockSpec((bb, tm, tf), lambda b, t, f: (b, t, f)),
                   pl.BlockSpec((nf, bb, CONV_W - 1, tf), lambda b, t, f: (0, b, 0, 0))],
        out_shape=[jax.ShapeDtypeStruct((B, T, Fd), BF16),
                   jax.ShapeDtypeStruct((nf, B, CONV_W - 1, tf), F32)],
        scratch_shapes=[pltpu.VMEM((bb, tm + SUBLANE, tf), F32)],
        compiler_params=_cparams(("parallel", "arbitrary", "arbitrary")), name="ffn_up",
    )(h, wg, wu, conv_w, conv_b.reshape(1, Fd), conv_state)
    y = _mm_res(act, wd, x, gate, name="ffn_down")
    return y, tails.transpose(1, 2, 0, 3).reshape(B, CONV_W - 1, Fd)


def _rwkv_mix_body(x_ref, sh_ref, sc_ref, mu_ref, st_ref, w1_ref, a1_ref, g1_ref,
                   xr_ref, xk_ref, xv_ref, tw_ref, ta_ref, tg_ref, last_ref, hbuf_ref, carry_ref):
    t = pl.program_id(1)
    tm = x_ref.shape[1]
    pad = SUBLANE
    x = x_ref[0]
    xn = x * lax.rsqrt(jnp.mean(x * x, -1, keepdims=True) + NORM_EPS)
    h = xn * (1.0 + sc_ref[0]) + sh_ref[0]

    @pl.when(t == 0)
    def _():
        carry_ref[...] = st_ref[0]

    hbuf_ref[pad - 1:pad, :] = carry_ref[...]
    hbuf_ref[pad:, :] = h
    xx = hbuf_ref[pad - 1:pad - 1 + tm, :] - h
    def mix(n):
        return (h + xx * mu_ref[n:n + 1, :]).astype(BF16)

    xr_ref[0] = mix(0)
    xk_ref[0] = mix(2)
    xv_ref[0] = mix(3)
    tw_ref[0] = jnp.tanh(jnp.dot(mix(1), w1_ref[...], preferred_element_type=F32)).astype(tw_ref.dtype)
    ta_ref[0] = jnp.dot(mix(4), a1_ref[...], preferred_element_type=F32).astype(ta_ref.dtype)
    tg_ref[0] = jax.nn.sigmoid(jnp.dot(mix(5), g1_ref[...], preferred_element_type=F32)).astype(tg_ref.dtype)
    last = hbuf_ref[pad + tm - 1:pad + tm, :]
    carry_ref[...] = last
    last_ref[0] = last


def _rwkv_mix(x, shift, scale, mu, shift_state, w1, a1, g1):
    B, T, D = x.shape
    tm = _pick(T, (512, 256, 128))
    xs = pl.BlockSpec((1, tm, D), lambda b, t: (b, t, 0))
    ms = pl.BlockSpec((1, 1, D), lambda b, t: (b, 0, 0))
    loras = (w1, a1, g1)
    outs = pl.pallas_call(
        _rwkv_mix_body, grid=(B, T // tm),
        in_specs=[xs, ms, ms, pl.BlockSpec((6, D), lambda b, t: (0, 0)), ms]
        + [pl.BlockSpec(w.shape, lambda b, t: (0, 0)) for w in loras],
        out_specs=[xs] * 3 + [pl.BlockSpec((1, tm, w.shape[1]), lambda b, t: (b, t, 0)) for w in loras] + [ms],
        out_shape=[jax.ShapeDtypeStruct((B, T, D), BF16)] * 3
        + [jax.ShapeDtypeStruct((B, T, w.shape[1]), BF16) for w in loras]
        + [jax.ShapeDtypeStruct((B, 1, D), F32)],
        scratch_shapes=[pltpu.VMEM((tm + SUBLANE, D), F32), pltpu.VMEM((1, D), F32)],
        compiler_params=_cparams(("parallel", "arbitrary")), name="rwkv_mix",
    )(x, shift, scale, mu, shift_state.reshape(B, 1, D), *loras)
    return outs[:6], outs[6].reshape(B, D)


def _dot(a, b):
    return jnp.dot(a.astype(BF16), b.astype(BF16), preferred_element_type=F32)


def _dot_nt(a, b):
    return lax.dot_general(a.astype(BF16), b.astype(BF16), (((1,), (1,)), ((), ())),
                           preferred_element_type=F32)


def _dot_tn(a, b):
    return lax.dot_general(a.astype(BF16), b.astype(BF16), (((0,), (0,)), ((), ())),
                           preferred_element_type=F32)


def _rwkv_rec_body(r_ref, k_ref, v_ref, wl_ref, al_ref, g_ref, par_ref, s0_ref, y_ref, s_ref,
                   *, L, npair, group, n_valid):
    c = pl.program_id(1)
    N = RWKV_HEAD
    P = 2 * N

    lane_l = lax.broadcasted_iota(jnp.int32, (L, P), 1)
    head0_l = lane_l < N
    r2 = lax.broadcasted_iota(jnp.int32, (2 * L, 2 * L), 0)
    c2 = lax.broadcasted_iota(jnp.int32, (2 * L, 2 * L), 1)
    same = (r2 >= L) == (c2 >= L)
    strict = same & (c2 < r2)
    incl = same & (c2 <= r2)
    rowhead = (lax.broadcasted_iota(jnp.int32, (2 * L, P), 0) >= L) == \
              (lax.broadcasted_iota(jnp.int32, (2 * L, P), 1) >= N)
    ip =lax.broadcasted_iota(jnp.int32, (P, P), 0)
    jp = lax.broadcasted_iota(jnp.int32, (P, P), 1)
    blockdiag = (ip >= N) == (jp >= N)
    seg = blockdiag.astype(BF16)
    live = lax.broadcasted_iota(jnp.int32, (L, P), 0) < n_valid

    def segsum(x):
        return jnp.dot(x.astype(BF16), seg, preferred_element_type=F32)

    row_l = lax.broadcasted_iota(jnp.int32, (L, P), 0)

    def cumsum_rows(x):
        s = 1
        while s < L:
            if s < SUBLANE:
                x = x + jnp.where(row_l >= s, pltpu.roll(x, s, 0), 0.0)
            else:
                x = x + jnp.concatenate([jnp.zeros((s, P), F32), x[:L - s]], axis=0)
            s *= 2
        return x

    def stack_heads(x):
        return jnp.concatenate([jnp.where(head0_l, x, 0.0), jnp.where(head0_l, 0.0, x)], axis=0)

    def fold_heads(x2):
        return x2[:L] + x2[L:]

    def twice(x):
        return jnp.concatenate([x, x], axis=0)

    @pl.when(c == 0)
    def _():
        s_ref[...] = s0_ref[...]

    def segsum_all(xs):
        out = segsum(jnp.concatenate(xs, axis=0))
        return [out[i * L:(i + 1) * L] for i in range(len(xs))]

    for g0 in range(0, npair, group):
        pairs = range(g0, min(g0 + group, npair))
        lanes = [slice(p * P, (p + 1) * P) for p in pairs]
        n = len(lanes)
        par = [par_ref[:, ln] for ln in lanes]
        r = [r_ref[0, :, ln] for ln in lanes]
        k = [k_ref[0, :, ln] for ln in lanes]
        v = [v_ref[0, :, ln] for ln in lanes]
        lnd, a, kkraw, kmod = [], [], [], []
        for i in range(n):
            w0, a0, k_k, k_a = (par[i][j:j + 1] for j in range(4))
            z = -(w0 + wl_ref[0, :, lanes[i]])
            softplus = jnp.maximum(z, 0.0) + jnp.log1p(jnp.exp(-jnp.abs(z)))
            lnd.append(-jnp.exp(-softplus - 0.5))
            a.append(jax.nn.sigmoid(a0 + al_ref[0, :, lanes[i]]))
            kkraw.append(k[i] * k_k)
            kmod.append(k[i] * (1.0 + (a[i] - 1.0) * k_a))
        sums = segsum_all([x * x for x in kkraw] + [r[i] * kmod[i] * par[i][4:5] for i in range(n)])
        kk = [kkraw[i] / jnp.maximum(jnp.sqrt(sums[i]), 1e-12) for i in range(n)]
        rk = sums[n:]
        if n_valid < L:
            lnd = [jnp.where(live, x, 0.0) for x in lnd]
            kk = [jnp.where(live, x, 0.0) for x in kk]
            kmod = [jnp.where(live, x, 0.0) for x in kmod]
            v = [jnp.where(live, x, 0.0) for x in v]
        cum = [cumsum_rows(x) for x in lnd]
        beta, kappa, rho, a2, lhs = [], [], [], [], []
        for i in range(n):
            e_neg = jnp.exp(-cum[i])
            beta.append((kk[i] * a[i] * e_neg).astype(BF16))
            kappa.append((kmod[i] * e_neg).astype(BF16))
            rho.append(r[i] * jnp.exp(cum[i]))
            a2.append(stack_heads(kk[i] * jnp.exp(cum[i] - lnd[i])))
            lhs.append(jnp.concatenate([a2[i], stack_heads(rho[i])], axis=0).astype(BF16))
        gb = [_dot_nt(lhs[i], twice(beta[i])) for i in range(n)]
        gk = [_dot_nt(lhs[i], twice(kappa[i])) for i in range(n)]
        nmat = [jnp.where(strict, x[:2 * L], 0.0) for x in gb]
        pb = [jnp.where(incl, x[2 * L:], 0.0).astype(BF16) for x in gb]
        v2 = [twice(x).astype(BF16) for x in v]
        kv = [_dot(jnp.concatenate([jnp.where(strict, gk[i][:2 * L], 0.0),
                                    jnp.where(incl, gk[i][2 * L:], 0.0)], axis=0), v2[i]) for i in range(n)]
        akv = [jnp.where(rowhead, x[:2 * L], 0.0) for x in kv]
        pkv = [x[2 * L:] for x in kv]

        nsq = max(L.bit_length() - 2, 0)
        corr = [-x for x in nmat]
        pw = [x.astype(BF16) for x in nmat]
        pw = [_dot(x, x) for x in pw]
        for it in range(nsq):
            pwb = [x.astype(BF16) for x in pw]
            if it < nsq - 1:
                res = [_dot(jnp.concatenate([pwb[i], corr[i].astype(BF16)], axis=0), pwb[i]) for i in range(n)]
                corr = [corr[i] + pw[i] + res[i][2 * L:] for i in range(n)]
                pw = [x[:2 * L] for x in res]
            else:
                corr = [corr[i] + pw[i] + _dot(corr[i], pwb[i]) for i in range(n)]

        both = [jnp.concatenate([a2[i], akv[i]], axis=1) for i in range(n)]
        both = [both[i] + _dot(corr[i], both[i]) for i in range(n)]
        at = [fold_heads(x[:, :P]) for x in both]
        w0_ = [fold_heads(x[:, P:]) for x in both]
        pbx = [_dot(pb[i], twice(jnp.concatenate([at[i], w0_[i]], axis=1))) for i in range(n)]
        rho_t = [rho[i] - fold_heads(jnp.where(rowhead, pbx[i][:, :P], 0.0)) for i in range(n)]
        o0 = [fold_heads(jnp.where(rowhead, pkv[i] - pbx[i][:, P:], 0.0)) for i in range(n)]
        gmat = [jnp.where(blockdiag, _dot_tn(at[i], beta[i]), 0.0) for i in range(n)]
        umat = [jnp.where(blockdiag, _dot_tn(jnp.concatenate([v[i], -w0_[i]], axis=0),
                                             jnp.concatenate([kappa[i], beta[i]], axis=0)), 0.0)
                for i in range(n)]

        S = [s_ref[0, p] for p in pairs]
        Sb = [x.astype(BF16) for x in S]
        o = [_dot_nt(rho_t[i], Sb[i]) + o0[i] for i in range(n)]
        for i, p in enumerate(pairs):
            s_ref[0, p] = (S[i] - _dot(Sb[i], gmat[i]) + umat[i]) * jnp.exp(cum[i][L - 1:L, :])

        mu = segsum_all(o)
        d = [o[i] - mu[i] * (1.0 / N) for i in range(n)]
        var = segsum_all([x * x for x in d])
        for i in range(n):
            yn = d[i] * lax.rsqrt(var[i] * (1.0 / N) + RWKV_GN_EPS) * par[i][5:6]
            y_ref[0, :, lanes[i]] = ((yn + rk[i] * v[i]) * g_ref[0, :, lanes[i]]).astype(y_ref.dtype)


def _rwkv_rec(r, k, v, wl, al, g, par, s0_blk, n_valid):
    B, T, D = r.shape
    P = 2 * RWKV_HEAD
    L = RWKV_CHUNK
    assert T % L == 0 and (n_valid == T or T == L)
    npair = D // P
    ts = pl.BlockSpec((1, L, D), lambda b, c: (b, c, 0))
    ss = pl.BlockSpec((1, npair, P, P), lambda b, c: (b, 0, 0, 0))
    return pl.pallas_call(
        functools.partial(_rwkv_rec_body, L=L, npair=npair, group=RWKV_GROUP, n_valid=min(n_valid, L)),
        grid=(B, T // L),
        in_specs=[ts] * 6 + [pl.BlockSpec((SUBLANE, D), lambda b, c: (0, 0)), ss],
        out_specs=[ts, ss],
        out_shape=[jax.ShapeDtypeStruct((B, T, D), BF16),
                   jax.ShapeDtypeStruct((B, npair, P, P), F32)],
        compiler_params=_cparams(("parallel", "arbitrary")), name="rwkv_rec",
    )(r, k, v, wl, al, g, par, s0_blk)


def _pair_blockdiag(s):
    B, H, N, _ = s.shape
    s = s.reshape(B, H // 2, 2, N, N)
    z = jnp.zeros_like(s[:, :, 0])
    top = jnp.concatenate([s[:, :, 0], z], axis=-1)
    bot = jnp.concatenate([z, s[:, :, 1]], axis=-1)
    return jnp.concatenate([top, bot], axis=-2)


def _pair_unblock(sb):
    B, Pn, P, _ = sb.shape
    N = P // 2
    return jnp.stack([sb[:, :, :N, :N], sb[:, :, N:, N:]], axis=2).reshape(B, 2 * Pn, N, N)


def _pad_time(a, T2):
    return jnp.pad(a, ((0, 0), (0, T2 - a.shape[1]), (0, 0)))


def _retention_layer(x, mods, pos0, state, w_in, w_out, gn_gain):
    B, T, D = x.shape
    H, dk, dv = state.shape[1:]
    shift_m, scale_m, gate_m = mods
    h = _norm_mod(x, shift_m, scale_m, BF16)
    half = dk // 2
    inv = ROPE_BASE ** (-jnp.arange(half, dtype=F32) / half)
    ang = (pos0 + jnp.arange(T)).astype(F32)[:, None] * inv[None, :]
    proj = _ret_proj(h, w_in, jnp.cos(ang), jnp.sin(ang), H * dk, dk)
    Tp = T if T % RET_CHUNK == 0 else -(-T // 64) * 64
    assert Tp == T or Tp <= RET_CHUNK
    y, s_new = _retention(_pad_time(proj, Tp), state, gn_gain, H, dk, dv, T)
    x = _mm_res(y[:, :T], w_out, x, gate_m, name="ret_out")
    return x, s_new


def _rwkv_layer(x, mods, shift_state, wkv_state, p):
    B, T, D = x.shape
    shift_m, scale_m, gate_m = mods
    (xr, xk, xv, tw, ta, tg), last = _rwkv_mix(x, shift_m, scale_m, p["mu"], shift_state,
                                               p["w1"], p["a1"], p["g1"])
    r, wl = _mm2(xr, p["w_r"], tw, p["w2"], F32, F32, name="rwkv_r_w")
    k, al = _mm2(xk, p["w_k"], ta, p["a2"], F32, BF16, name="rwkv_k_a")
    v, g = _mm2(xv, p["w_v"], tg, p["g2"], F32, BF16, name="rwkv_v_g")
    Tp = -(-T // RWKV_CHUNK) * RWKV_CHUNK
    ins = [_pad_time(t, Tp) for t in (r, k, v, wl, al, g)]
    y, s_blk = _rwkv_rec(*ins, p["par"], _pair_blockdiag(wkv_state), T)
    x = _mm_res(y[:, :T], p["w_o"], x, gate_m, name="rwkv_out")
    return x, _pair_unblock(s_blk), last


def _pad_cols(w, n):
    return jnp.pad(w, ((0, 0), (0, n - w.shape[1])))


def _pad_rows(w, n):
    return jnp.pad(w, ((0, n - w.shape[0]), (0, 0)))


def _run_group(x, mod, pos0, st_ret, st_wkv, st_shift, st_conv, w):
    depth = mod.shape[0]
    new_ret, new_wkv, new_shift, new_conv = [], [], [], []
    for i in range(depth):
        m = [mod[i, :, n][:, None, :] for n in range(6)]
        j = i // 2
        if i % 2 == 0:
            x, s = _retention_layer(x, m[:3], pos0, st_ret[j], w["ret_w_in"][j], w["ret_w_out"][j],
                                    w["ret_gn_gain"][j])
            new_ret.append(s)
        else:
            x, s, last = _rwkv_layer(x, m[:3], st_shift[j], st_wkv[j], w["rwkv"][j])
            new_wkv.append(s)
            new_shift.append(last)
        h = _norm_mod(x, m[3], m[4], BF16)
        x, cs = _ffn(h, x, m[5], w["ffn_w_gate"][i], w["ffn_w_up"][i], w["ffn_w_down"][i],
                     w["ffn_conv_w"][i], w["ffn_conv_b"][i], st_conv[i])
        new_conv.append(cs)
    out = _norm_gain(x, w["final_gain"])
    return out, jnp.stack(new_ret), jnp.stack(new_wkv), jnp.stack(new_shift), jnp.stack(new_conv)


def kernel(x_prompt, x_sample, c_prompt, c_sample, state_ret, state_rwkv_wkv, state_rwkv_shift, state_ffn_conv, ada_w, ada_b, ret_w_in, ret_w_out, ret_gn_gain, rwkv_mu, rwkv_w_r, rwkv_w_k, rwkv_w_v, rwkv_w_o, rwkv_w0, rwkv_w1, rwkv_w2, rwkv_a0, rwkv_a1, rwkv_a2, rwkv_g1, rwkv_g2, rwkv_k_k, rwkv_k_a, rwkv_r_k, rwkv_gn_gain, ffn_w_gate, ffn_w_up, ffn_conv_w, ffn_conv_b, ffn_w_down, final_gain):
    B, T, D = x_prompt.shape
    Bs = x_sample.shape[0]
    depth = ada_w.shape[0]
    n_rwkv = rwkv_mu.shape[0]

    rows = -(-(B + Bs) // SUBLANE) * SUBLANE
    c_all = jnp.pad(jnp.concatenate([c_prompt, c_sample], axis=0), ((0, rows - B - Bs), (0, 0)))
    mod = _ada(c_all, ada_w, ada_b).reshape(depth, rows, 6, D)

    bf = lambda a: a.astype(BF16)
    rwkv = []
    for j in range(n_rwkv):
        lw = -(-rwkv_w1.shape[2] // LANE) * LANE
        la = -(-rwkv_a1.shape[2] // LANE) * LANE
        par = jnp.stack([rwkv_w0[j], rwkv_a0[j], rwkv_k_k[j], rwkv_k_a[j], rwkv_r_k[j].reshape(D),
                         rwkv_gn_gain[j], jnp.zeros((D,), F32), jnp.zeros((D,), F32)])
        rwkv.append(dict(
            mu=rwkv_mu[j], w_r=bf(rwkv_w_r[j]), w_k=bf(rwkv_w_k[j]), w_v=bf(rwkv_w_v[j]),
            w_o=bf(rwkv_w_o[j]),
            w1=bf(_pad_cols(rwkv_w1[j], lw)), w2=bf(_pad_rows(rwkv_w2[j], lw)),
            a1=bf(_pad_cols(rwkv_a1[j], la)), a2=bf(_pad_rows(rwkv_a2[j], la)),
            g1=bf(rwkv_g1[j]), g2=bf(rwkv_g2[j]), par=par))
    per_layer = lambda a: [bf(a[i]) for i in range(a.shape[0])]
    w = dict(ret_w_in=per_layer(ret_w_in), ret_w_out=per_layer(ret_w_out), ret_gn_gain=ret_gn_gain,
             rwkv=rwkv, ffn_w_gate=per_layer(ffn_w_gate), ffn_w_up=per_layer(ffn_w_up),
             ffn_w_down=per_layer(ffn_w_down), ffn_conv_w=ffn_conv_w, ffn_conv_b=ffn_conv_b,
             final_gain=final_gain)

    n_ret = state_ret.shape[0]
    F_ = ffn_w_gate.shape[2]
    z_ret = jnp.zeros((n_ret, B) + state_ret.shape[2:], F32)
    z_wkv = jnp.zeros((n_rwkv, B) + state_rwkv_wkv.shape[2:], F32)
    z_shift = jnp.zeros((n_rwkv, B, D), F32)
    z_conv = jnp.zeros((depth, B, CONV_W - 1, F_), F32)
    y_p, p_ret, p_wkv, p_shift, p_conv = _run_group(
        x_prompt, mod[:, :B], 0, z_ret, z_wkv, z_shift, z_conv, w)
    y_s, s_ret, s_wkv, s_shift, s_conv = _run_group(
        x_sample, mod[:, B:B + Bs], PAST_LEN, state_ret, state_rwkv_wkv, state_rwkv_shift,
        state_ffn_conv, w)
    return (y_p, y_s, p_ret, p_wkv, p_shift, p_conv, s_ret, s_wkv, s_shift, s_conv)
```

```python
import functools

import jax
import jax.numpy as jnp
from jax import lax
from jax.experimental import pallas as pl
from jax.experimental.pallas import tpu as pltpu

F32 = jnp.float32
BF16 = jnp.bfloat16

NORM_EPS = 1e-6
RET_GN_EPS = 1e-5
RWKV_GN_EPS = 64e-5
ROPE_BASE = 10000.0
PAST_LEN = 4096
CONV_W = 3

LANE = 128
SUBLANE = 8
VMEM_LIMIT_MB = 56

RET_CHUNK = 256
ROW_CHUNK = 256
RET_TBLOCK = 1024
RWKV_CHUNK = 64
RWKV_HEAD = 64
RWKV_CHUNKS_PER_STEP = 2
RWKV_GROUP = 16


def _pick(n, cands):
    for c in cands:
        if n % c == 0:
            return c
    return n


def _cparams(sem):
    return pltpu.CompilerParams(dimension_semantics=sem, vmem_limit_bytes=VMEM_LIMIT_MB << 20)


def _token_tiles(B, T):
    if T >= 512:
        return 1, _pick(T, (1024, 512, 256, 128))
    return B, T


def _silu(x):
    return x * jax.nn.sigmoid(x)


def _ada_body(c_ref, w_ref, b_ref, o_ref):
    c = c_ref[...]
    s = _silu(c).astype(BF16)
    o_ref[0] = jnp.dot(s, w_ref[0].astype(BF16), preferred_element_type=F32) + b_ref[0]


def _ada(c_all, ada_w, ada_b):
    depth, D, N = ada_w.shape
    R = c_all.shape[0]
    tn = _pick(N, (1024, 512, 256, 128))
    return pl.pallas_call(
        _ada_body,
        grid=(depth, N // tn),
        in_specs=[pl.BlockSpec((R, D), lambda l, j: (0, 0)),
                  pl.BlockSpec((1, D, tn), lambda l, j: (l, 0, j)),
                  pl.BlockSpec((1, 1, tn), lambda l, j: (l, 0, j))],
        out_specs=pl.BlockSpec((1, R, tn), lambda l, j: (l, 0, j)),
        out_shape=jax.ShapeDtypeStruct((depth, R, N), F32),
        compiler_params=_cparams(("parallel", "parallel")),
        name="ada_mod",
    )(c_all, ada_w, ada_b.reshape(depth, 1, N))


def _norm_gain_body(x_ref, g_ref, o_ref):
    x = x_ref[...]
    xn = x * lax.rsqrt(jnp.mean(x * x, -1, keepdims=True) + NORM_EPS)
    o_ref[...] = (xn * g_ref[...]).astype(o_ref.dtype)


def _norm_gain(x, gain):
    B, T, D = x.shape
    bb, tm = _token_tiles(B, T)
    xs = pl.BlockSpec((bb, tm, D), lambda b, t: (b, t, 0))
    return pl.pallas_call(
        _norm_gain_body, grid=(B // bb, T // tm),
        in_specs=[xs, pl.BlockSpec((1, 1, D), lambda b, t: (0, 0, 0))], out_specs=xs,
        out_shape=jax.ShapeDtypeStruct((B, T, D), F32),
        compiler_params=_cparams(("parallel", "parallel")), name="final_norm",
    )(x, gain.reshape(1, 1, D))


def _mm2_body(x_ref, w_ref, x2_ref, w2_ref, o_ref, o2_ref):
    bb, tm, K = x_ref.shape
    acc = jnp.dot(x_ref[...].reshape(bb * tm, K), w_ref[...], preferred_element_type=F32)
    o_ref[...] = acc.reshape(bb, tm, -1).astype(o_ref.dtype)
    acc2 = jnp.dot(x2_ref[...].reshape(bb * tm, x2_ref.shape[2]), w2_ref[...], preferred_element_type=F32)
    o2_ref[...] = acc2.reshape(bb, tm, -1).astype(o2_ref.dtype)


def _mm2(x, w, x2, w2, out_dtype, out2_dtype, name="mm2"):
    B, T, K = x.shape
    K2 = x2.shape[2]
    N = w.shape[1]
    assert w2.shape[1] == N
    bb, tm = _token_tiles(B, T)
    tn = _pick(N, (1024, 512, 256, 128))
    os_ = pl.BlockSpec((bb, tm, tn), lambda b, t, j: (b, t, j))
    return pl.pallas_call(
        _mm2_body,
        grid=(B // bb, T // tm, N // tn),
        in_specs=[pl.BlockSpec((bb, tm, K), lambda b, t, j: (b, t, 0)),
                  pl.BlockSpec((K, tn), lambda b, t, j: (0, j)),
                  pl.BlockSpec((bb, tm, K2), lambda b, t, j: (b, t, 0)),
                  pl.BlockSpec((K2, tn), lambda b, t, j: (0, j))],
        out_specs=[os_, os_],
        out_shape=[jax.ShapeDtypeStruct((B, T, N), out_dtype), jax.ShapeDtypeStruct((B, T, N), out2_dtype)],
        compiler_params=_cparams(("parallel", "parallel", "parallel")), name=name,
    )(x, w, x2, w2)


def _mm_res_body(y_ref, w_ref, x_ref, gate_ref, o_ref):
    bb, tm, K = y_ref.shape
    acc = jnp.dot(y_ref[...].reshape(bb * tm, K), w_ref[...], preferred_element_type=F32)
    o_ref[...] = x_ref[...] + gate_ref[...] * acc.reshape(bb, tm, -1)


def _mm_res(y, w, x, gate, name="mm_res"):
    B, T, K = y.shape
    N = w.shape[1]
    bb, tm = _token_tiles(B, T)
    tn = _pick(N, (1024, 512, 256, 128) if K <= 2048 else (512, 256, 128))
    return pl.pallas_call(
        _mm_res_body,
        grid=(B // bb, T // tm, N // tn),
        in_specs=[pl.BlockSpec((bb, tm, K), lambda b, t, j: (b, t, 0)),
                  pl.BlockSpec((K, tn), lambda b, t, j: (0, j)),
                  pl.BlockSpec((bb, tm, tn), lambda b, t, j: (b, t, j)),
                  pl.BlockSpec((bb, 1, tn), lambda b, t, j: (b, 0, j))],
        out_specs=pl.BlockSpec((bb, tm, tn), lambda b, t, j: (b, t, j)),
        out_shape=jax.ShapeDtypeStruct((B, T, N), F32),
        compiler_params=_cparams(("parallel", "parallel", "parallel")), name=name,
    )(y, w, x, gate)


def _norm_mod_rows(x, shift, scale):
    xn = x * lax.rsqrt(jnp.mean(x * x, -1, keepdims=True) + NORM_EPS)
    return xn * (1.0 + scale) + shift


def _ret_proj_body(x_ref, sh_ref, sc_ref, w_ref, cos_ref, sin_ref, o_ref, h_ref, *, n_rot, n_q, dk):
    j = pl.program_id(2)
    bb, tm, K = x_ref.shape
    tn = w_ref.shape[1]

    def rotary_tile(first):
        sc = jnp.where(j >= n_q, dk ** -0.5, 1.0).astype(F32)
        half = dk // 2
        rc = _pick(tm, (ROW_CHUNK,))
        for r0 in range(0, tm, rc):
            if first:
                h = _norm_mod_rows(x_ref[:, r0:r0 + rc, :], sh_ref[...], sc_ref[...]).astype(BF16)
                h_ref[:, r0:r0 + rc, :] = h
            else:
                h = h_ref[:, r0:r0 + rc, :]
            acc = jnp.dot(h.reshape(bb * rc, K), w_ref[...],
                          preferred_element_type=F32).reshape(bb, rc, tn)
            cos = cos_ref[r0:r0 + rc, :][None] * sc
            sin = sin_ref[r0:r0 + rc, :][None] * sc
            for hh in range(tn // dk):
                lo = hh * dk
                x1 = acc[:, :, lo:lo + half]
                x2 = acc[:, :, lo + half:lo + dk]
                o_ref[:, r0:r0 + rc, lo:lo + half] = (x1 * cos - x2 * sin).astype(o_ref.dtype)
                o_ref[:, r0:r0 + rc, lo + half:lo + dk] = (x1 * sin + x2 * cos).astype(o_ref.dtype)

    pl.when(j == 0)(functools.partial(rotary_tile, True))
    pl.when((j > 0) & (j < n_rot))(functools.partial(rotary_tile, False))

    @pl.when(j >= n_rot)
    def _():
        acc = jnp.dot(h_ref[...].reshape(bb * tm, K), w_ref[...], preferred_element_type=F32)
        o_ref[...] = acc.reshape(bb, tm, tn).astype(o_ref.dtype)


def _ret_proj(x, shift, scale, w_in, cos, sin, qk_dim, dk):
    B, T, K = x.shape
    N = w_in.shape[1]
    bb, tm = _token_tiles(B, T)
    tn = _pick(qk_dim, (1024, 512, 256))
    half = dk // 2
    ms = pl.BlockSpec((bb, 1, K), lambda b, t, j: (b, 0, 0))
    return pl.pallas_call(
        functools.partial(_ret_proj_body, n_rot=2 * qk_dim // tn, n_q=qk_dim // tn, dk=dk),
        grid=(B // bb, T // tm, N // tn),
        in_specs=[pl.BlockSpec((bb, tm, K), lambda b, t, j: (b, t, 0)), ms, ms,
                  pl.BlockSpec((K, tn), lambda b, t, j: (0, j)),
                  pl.BlockSpec((tm, half), lambda b, t, j: (t, 0)),
                  pl.BlockSpec((tm, half), lambda b, t, j: (t, 0))],
        out_specs=pl.BlockSpec((bb, tm, tn), lambda b, t, j: (b, t, j)),
        out_shape=jax.ShapeDtypeStruct((B, T, N), BF16),
        scratch_shapes=[pltpu.VMEM((bb, tm, K), BF16)],
        compiler_params=_cparams(("parallel", "parallel", "arbitrary")), name="ret_proj",
    )(x, shift, scale, w_in, cos, sin)


def _ret_body(q_ref, k_ref, v_ref, g_ref, gain_ref, intra_ref, cross_ref, into_ref, cd_ref, s0_ref,
              y_ref, s_ref):
    c = pl.program_id(2)
    L = intra_ref.shape[1]

    @pl.when(c == 0)
    def _():
        s_ref[...] = s0_ref[...]

    chunks = [slice(ci * L, (ci + 1) * L) for ci in range(q_ref.shape[1] // L)]
    q = [q_ref[0, rows, :] for rows in chunks]
    v = [v_ref[0, rows, :] for rows in chunks]
    scores = [(lax.dot_general(q[i], k_ref[0, rows, :], (((1,), (1,)), ((), ())),
                               preferred_element_type=F32) * intra_ref[0]).astype(BF16)
              for i, rows in enumerate(chunks)]
    kv = [lax.dot_general((k_ref[0, rows, :].astype(F32) * into_ref[0]).astype(BF16), v[i],
                          (((0,), (0,)), ((), ())), preferred_element_type=F32)
          for i, rows in enumerate(chunks)]
    pv = [jnp.dot(scores[i], v[i], preferred_element_type=F32) for i in range(len(chunks))]
    for i, rows in enumerate(chunks):
        S = s_ref[0, 0]
        o = pv[i] + jnp.dot(q[i], S.astype(BF16), preferred_element_type=F32) * cross_ref[0]
        s_ref[0, 0] = S * cd_ref[0] + kv[i]
        mu = jnp.mean(o, -1, keepdims=True)
        d = o - mu
        var = jnp.mean(d * d, -1, keepdims=True)
        g = g_ref[0, rows, :].astype(F32)
        y_ref[0, rows, :] = (d * lax.rsqrt(var + RET_GN_EPS) * gain_ref[...] * _silu(g)).astype(y_ref.dtype)


def _retention(proj, state, gain, H, dk, dv, n_valid):
    B, T, _ = proj.shape
    L = min(RET_CHUNK, T)
    nv = min(n_valid, L)
    log_g = jnp.log1p(-(2.0 ** (-5.0 - jnp.arange(H, dtype=F32))))
    idx = jnp.arange(L, dtype=F32)
    diff = idx[:, None] - idx[None, :]
    intra = jnp.where(diff >= 0, jnp.exp(log_g[:, None, None] * jnp.maximum(diff, 0.0)), 0.0)
    cross = jnp.exp(log_g[:, None] * (idx[None, :] + 1.0))[:, :, None]
    into = jnp.exp(log_g[:, None] * (nv - 1.0 - idx[None, :]))[:, :, None]
    cd = jnp.exp(log_g * nv)[:, None, None]
    tb = _pick(T, (RET_TBLOCK, L))
    kq = (H * dk) // dk
    vq = (2 * H * dk) // dv
    return pl.pallas_call(
        _ret_body,
        grid=(B, H, T // tb),
        in_specs=[pl.BlockSpec((1, tb, dk), lambda b, h, c: (b, c, h)),
                  pl.BlockSpec((1, tb, dk), lambda b, h, c: (b, c, kq + h)),
                  pl.BlockSpec((1, tb, dv), lambda b, h, c: (b, c, vq + h)),
                  pl.BlockSpec((1, tb, dv), lambda b, h, c: (b, c, vq + H + h)),
                  pl.BlockSpec((1, dv), lambda b, h, c: (0, h)),
                  pl.BlockSpec((1, L, L), lambda b, h, c: (h, 0, 0)),
                  pl.BlockSpec((1, L, 1), lambda b, h, c: (h, 0, 0)),
                  pl.BlockSpec((1, L, 1), lambda b, h, c: (h, 0, 0)),
                  pl.BlockSpec((1, 1, 1), lambda b, h, c: (h, 0, 0)),
                  pl.BlockSpec((1, 1, dk, dv), lambda b, h, c: (b, h, 0, 0))],
        out_specs=[pl.BlockSpec((1, tb, dv), lambda b, h, c: (b, c, h)),
                   pl.BlockSpec((1, 1, dk, dv), lambda b, h, c: (b, h, 0, 0))],
        out_shape=[jax.ShapeDtypeStruct((B, T, H * dv), BF16),
                   jax.ShapeDtypeStruct((B, H, dk, dv), F32)],
        compiler_params=_cparams(("parallel", "parallel", "arbitrary")), name="retention",
    )(proj, proj, proj, proj, gain.reshape(1, H * dv), intra, cross, into, cd, state)


def _ffn_up_body(x_ref, sh_ref, sc_ref, wg_ref, wu_ref, cw_ref, cb_ref, cs_ref, o_ref, tail_ref,
                 ext_ref, h_ref):
    t = pl.program_id(1)
    f = pl.program_id(2)
    bb, tm, D = x_ref.shape
    tf = wg_ref.shape[1]
    pad = SUBLANE

    @pl.when(t == 0)
    def _():
        tail_ref[f] = cs_ref[...]

    ext_ref[:, pad - 2:pad, :] = tail_ref[f]

    def column_tile(first):
        cw = cw_ref[...]
        rc = _pick(tm, (ROW_CHUNK,))
        for r0 in range(0, tm, rc):
            if first:
                h = _norm_mod_rows(x_ref[:, r0:r0 + rc, :], sh_ref[...], sc_ref[...]).astype(BF16)
                h_ref[:, r0:r0 + rc, :] = h
            else:
                h = h_ref[:, r0:r0 + rc, :]
            h = h.reshape(bb * rc, D)
            u = jnp.dot(h, wg_ref[...], preferred_element_type=F32).reshape(bb, rc, tf)
            up = jnp.dot(h, wu_ref[...], preferred_element_type=F32).reshape(bb, rc, tf)
            ext_ref[:, pad + r0:pad + r0 + rc, :] = u
            conv = (cb_ref[...] + ext_ref[:, pad - 2 + r0:pad - 2 + r0 + rc, :] * cw[0:1]
                    + ext_ref[:, pad - 1 + r0:pad - 1 + r0 + rc, :] * cw[1:2] + u * cw[2:3])
            o_ref[:, r0:r0 + rc, :] = (_silu(conv) * up).astype(o_ref.dtype)

    pl.when(f == 0)(functools.partial(column_tile, True))
    pl.when(f > 0)(functools.partial(column_tile, False))
    tail_ref[f] = ext_ref[:, pad + tm - 2:pad + tm, :]


def _ffn(x, shift, scale, gate, wg, wu, wd, conv_w, conv_b, conv_state):
    B, T, D = x.shape
    Fd = wg.shape[1]
    bb, tm = _token_tiles(B, T)
    tf = _pick(Fd, (512, 256, 128))
    nf = Fd // tf
    ms = pl.BlockSpec((bb, 1, D), lambda b, t, f: (b, 0, 0))
    act, tails = pl.pallas_call(
        _ffn_up_body,
        grid=(B // bb, T // tm, nf),
        in_specs=[pl.BlockSpec((bb, tm, D), lambda b, t, f: (b, t, 0)), ms, ms,
                  pl.BlockSpec((D, tf), lambda b, t, f: (0, f)),
                  pl.BlockSpec((D, tf), lambda b, t, f: (0, f)),
                  pl.BlockSpec((CONV_W, tf), lambda b, t, f: (0, f)),
                  pl.BlockSpec((1, tf), lambda b, t, f: (0, f)),
                  pl.BlockSpec((bb, CONV_W - 1, tf), lambda b, t, f: (b, 0, f))],
        out_specs=[pl.BlockSpec((bb, tm, tf), lambda b, t, f: (b, t, f)),
                   pl.BlockSpec((nf, bb, CONV_W - 1, tf), lambda b, t, f: (0, b, 0, 0))],
        out_shape=[jax.ShapeDtypeStruct((B, T, Fd), BF16),
                   jax.ShapeDtypeStruct((nf, B, CONV_W - 1, tf), F32)],
        scratch_shapes=[pltpu.VMEM((bb, tm + SUBLANE, tf), F32), pltpu.VMEM((bb, tm, D), BF16)],
        compiler_params=_cparams(("parallel", "arbitrary", "arbitrary")), name="ffn_up",
    )(x, shift, scale, wg, wu, conv_w, conv_b.reshape(1, Fd), conv_state)
    y = _mm_res(act, wd, x, gate, name="ffn_down")
    return y, tails.transpose(1, 2, 0, 3).reshape(B, CONV_W - 1, Fd)


def _rwkv_mix_body(x_ref, sh_ref, sc_ref, mu_ref, st_ref, w1_ref, a1_ref, g1_ref,
                   xr_ref, xk_ref, xv_ref, tw_ref, ta_ref, tg_ref, last_ref, hbuf_ref, carry_ref):
    t = pl.program_id(1)
    tm = x_ref.shape[1]
    pad = SUBLANE
    x = x_ref[0]
    xn = x * lax.rsqrt(jnp.mean(x * x, -1, keepdims=True) + NORM_EPS)
    h = xn * (1.0 + sc_ref[0]) + sh_ref[0]

    @pl.when(t == 0)
    def _():
        carry_ref[...] = st_ref[0]

    hbuf_ref[pad - 1:pad, :] = carry_ref[...]
    hbuf_ref[pad:, :] = h
    xx = hbuf_ref[pad - 1:pad - 1 + tm, :] - h
    def mix(n):
        return (h + xx * mu_ref[n:n + 1, :]).astype(BF16)

    xr_ref[0] = mix(0)
    xk_ref[0] = mix(2)
    xv_ref[0] = mix(3)
    tw_ref[0] = jnp.tanh(jnp.dot(mix(1), w1_ref[...], preferred_element_type=F32)).astype(tw_ref.dtype)
    ta_ref[0] = jnp.dot(mix(4), a1_ref[...], preferred_element_type=F32).astype(ta_ref.dtype)
    tg_ref[0] = jax.nn.sigmoid(jnp.dot(mix(5), g1_ref[...], preferred_element_type=F32)).astype(tg_ref.dtype)
    last = hbuf_ref[pad + tm - 1:pad + tm, :]
    carry_ref[...] = last
    last_ref[0] = last


def _rwkv_mix(x, shift, scale, mu, shift_state, w1, a1, g1):
    B, T, D = x.shape
    tm = _pick(T, (512, 256, 128))
    xs = pl.BlockSpec((1, tm, D), lambda b, t: (b, t, 0))
    ms = pl.BlockSpec((1, 1, D), lambda b, t: (b, 0, 0))
    loras = (w1, a1, g1)
    outs = pl.pallas_call(
        _rwkv_mix_body, grid=(B, T // tm),
        in_specs=[xs, ms, ms, pl.BlockSpec((6, D), lambda b, t: (0, 0)), ms]
        + [pl.BlockSpec(w.shape, lambda b, t: (0, 0)) for w in loras],
        out_specs=[xs] * 3 + [pl.BlockSpec((1, tm, w.shape[1]), lambda b, t: (b, t, 0)) for w in loras] + [ms],
        out_shape=[jax.ShapeDtypeStruct((B, T, D), BF16)] * 3
        + [jax.ShapeDtypeStruct((B, T, w.shape[1]), BF16) for w in loras]
        + [jax.ShapeDtypeStruct((B, 1, D), F32)],
        scratch_shapes=[pltpu.VMEM((tm + SUBLANE, D), F32), pltpu.VMEM((1, D), F32)],
        compiler_params=_cparams(("parallel", "arbitrary")), name="rwkv_mix",
    )(x, shift, scale, mu, shift_state.reshape(B, 1, D), *loras)
    return outs[:6], outs[6].reshape(B, D)


def _dot(a, b):
    return jnp.dot(a.astype(BF16), b.astype(BF16), preferred_element_type=F32)


def _dot_nt(a, b):
    return lax.dot_general(a.astype(BF16), b.astype(BF16), (((1,), (1,)), ((), ())),
                           preferred_element_type=F32)


def _dot_tn(a, b):
    return lax.dot_general(a.astype(BF16), b.astype(BF16), (((0,), (0,)), ((), ())),
                           preferred_element_type=F32)


def _rwkv_rec_body(r_ref, k_ref, v_ref, wl_ref, al_ref, g_ref, par_ref, s0_ref, y_ref, s_ref,
                   *, L, npair, group, n_valid):
    c = pl.program_id(1)
    N = RWKV_HEAD
    P = 2 * N

    lane_l = lax.broadcasted_iota(jnp.int32, (L, P), 1)
    head0_l = lane_l < N
    r2 = lax.broadcasted_iota(jnp.int32, (2 * L, 2 * L), 0)
    c2 = lax.broadcasted_iota(jnp.int32, (2 * L, 2 * L), 1)
    same = (r2 >= L) == (c2 >= L)
    strict = same & (c2 < r2)
    incl = same & (c2 <= r2)
    rowhead = (lax.broadcasted_iota(jnp.int32, (2 * L, P), 0) >= L) == \
              (lax.broadcasted_iota(jnp.int32, (2 * L, P), 1) >= N)
    ip =lax.broadcasted_iota(jnp.int32, (P, P), 0)
    jp = lax.broadcasted_iota(jnp.int32, (P, P), 1)
    blockdiag = (ip >= N) == (jp >= N)
    seg = blockdiag.astype(BF16)
    live = lax.broadcasted_iota(jnp.int32, (L, P), 0) < n_valid

    def segsum(x):
        return jnp.dot(x.astype(BF16), seg, preferred_element_type=F32)

    row_l = lax.broadcasted_iota(jnp.int32, (L, P), 0)

    def cumsum_rows(x):
        s = 1
        while s < L:
            if s < SUBLANE:
                x = x + jnp.where(row_l >= s, pltpu.roll(x, s, 0), 0.0)
            else:
                x = x + jnp.concatenate([jnp.zeros((s, P), F32), x[:L - s]], axis=0)
            s *= 2
        return x

    def stack_heads(x):
        return jnp.concatenate([jnp.where(head0_l, x, 0.0), jnp.where(head0_l, 0.0, x)], axis=0)

    def fold_heads(x2):
        return x2[:L] + x2[L:]

    def twice(x):
        return jnp.concatenate([x, x], axis=0)

    @pl.when(c == 0)
    def _():
        s_ref[...] = s0_ref[...]

    def segsum_all(xs):
        out = segsum(jnp.concatenate(xs, axis=0))
        return [out[i * L:(i + 1) * L] for i in range(len(xs))]

    nchunk = r_ref.shape[1] // L
    for g0 in range(0, npair, group):
        items = [(ci, p) for ci in range(nchunk) for p in range(g0, min(g0 + group, npair))]
        rows = [slice(ci * L, (ci + 1) * L) for ci, _ in items]
        lanes = [slice(p * P, (p + 1) * P) for _, p in items]
        n = len(items)
        par = [par_ref[:, ln] for ln in lanes]
        r = [r_ref[0, rows[i], lanes[i]] for i in range(n)]
        k = [k_ref[0, rows[i], lanes[i]] for i in range(n)]
        v = [v_ref[0, rows[i], lanes[i]] for i in range(n)]
        lnd, a, kkraw, kmod = [], [], [], []
        for i in range(n):
            w0, a0, k_k, k_a = (par[i][j:j + 1] for j in range(4))
            z = -(w0 + wl_ref[0, rows[i], lanes[i]])
            softplus = jnp.maximum(z, 0.0) + jnp.log1p(jnp.exp(-jnp.abs(z)))
            lnd.append(-jnp.exp(-softplus - 0.5))
            a.append(jax.nn.sigmoid(a0 + al_ref[0, rows[i], lanes[i]]))
            kkraw.append(k[i] * k_k)
            kmod.append(k[i] * (1.0 + (a[i] - 1.0) * k_a))
        sums = segsum_all([x * x for x in kkraw] + [r[i] * kmod[i] * par[i][4:5] for i in range(n)])
        kk = [kkraw[i] / jnp.maximum(jnp.sqrt(sums[i]), 1e-12) for i in range(n)]
        rk = sums[n:]
        if n_valid < L:
            lnd = [jnp.where(live, x, 0.0) for x in lnd]
            kk = [jnp.where(live, x, 0.0) for x in kk]
            kmod = [jnp.where(live, x, 0.0) for x in kmod]
            v = [jnp.where(live, x, 0.0) for x in v]
        cum = [cumsum_rows(x) for x in lnd]
        beta, kappa, rho, a2, lhs = [], [], [], [], []
        for i in range(n):
            e_neg = jnp.exp(-cum[i])
            beta.append((kk[i] * a[i] * e_neg).astype(BF16))
            kappa.append((kmod[i] * e_neg).astype(BF16))
            rho.append(r[i] * jnp.exp(cum[i]))
            a2.append(stack_heads(kk[i] * jnp.exp(cum[i] - lnd[i])))
            lhs.append(jnp.concatenate([a2[i], stack_heads(rho[i])], axis=0).astype(BF16))
        gb = [_dot_nt(lhs[i], twice(beta[i])) for i in range(n)]
        gk = [_dot_nt(lhs[i], twice(kappa[i])) for i in range(n)]
        nmat = [jnp.where(strict, x[:2 * L], 0.0) for x in gb]
        pb = [jnp.where(incl, x[2 * L:], 0.0).astype(BF16) for x in gb]
        v2 = [twice(x).astype(BF16) for x in v]
        kv = [_dot(jnp.concatenate([jnp.where(strict, gk[i][:2 * L], 0.0),
                                    jnp.where(incl, gk[i][2 * L:], 0.0)], axis=0), v2[i]) for i in range(n)]
        akv = [jnp.where(rowhead, x[:2 * L], 0.0) for x in kv]
        pkv = [x[2 * L:] for x in kv]

        nsq = max(L.bit_length() - 2, 0)
        corr = [-x for x in nmat]
        pw = [x.astype(BF16) for x in nmat]
        pw = [_dot(x, x) for x in pw]
        for it in range(nsq):
            pwb = [x.astype(BF16) for x in pw]
            if it < nsq - 1:
                res = [_dot(jnp.concatenate([pwb[i], corr[i].astype(BF16)], axis=0), pwb[i]) for i in range(n)]
                corr = [corr[i] + pw[i] + res[i][2 * L:] for i in range(n)]
                pw = [x[:2 * L] for x in res]
            else:
                corr = [corr[i] + pw[i] + _dot(corr[i], pwb[i]) for i in range(n)]

        both = [jnp.concatenate([a2[i], akv[i]], axis=1) for i in range(n)]
        both = [both[i] + _dot(corr[i], both[i]) for i in range(n)]
        at = [fold_heads(x[:, :P]) for x in both]
        w0_ = [fold_heads(x[:, P:]) for x in both]
        pbx = [_dot(pb[i], twice(jnp.concatenate([at[i], w0_[i]], axis=1))) for i in range(n)]
        rho_t = [rho[i] - fold_heads(jnp.where(rowhead, pbx[i][:, :P], 0.0)) for i in range(n)]
        o0 = [fold_heads(jnp.where(rowhead, pkv[i] - pbx[i][:, P:], 0.0)) for i in range(n)]
        gmat = [jnp.where(blockdiag, _dot_tn(at[i], beta[i]), 0.0) for i in range(n)]
        umat = [jnp.where(blockdiag, _dot_tn(jnp.concatenate([v[i], -w0_[i]], axis=0),
                                             jnp.concatenate([kappa[i], beta[i]], axis=0)), 0.0)
                for i in range(n)]

        o = [None] * n
        for ci in range(nchunk):
            sel = [i for i in range(n) if items[i][0] == ci]
            S = {i: s_ref[0, items[i][1]] for i in sel}
            Sb = {i: S[i].astype(BF16) for i in sel}
            for i in sel:
                o[i] = _dot_nt(rho_t[i], Sb[i]) + o0[i]
            for i in sel:
                s_ref[0, items[i][1]] = ((S[i] - _dot(Sb[i], gmat[i]) + umat[i])
                                         * jnp.exp(cum[i][L - 1:L, :]))

        mu = segsum_all(o)
        d = [o[i] - mu[i] * (1.0 / N) for i in range(n)]
        var = segsum_all([x * x for x in d])
        for i in range(n):
            yn = d[i] * lax.rsqrt(var[i] * (1.0 / N) + RWKV_GN_EPS) * par[i][5:6]
            y_ref[0, rows[i], lanes[i]] = ((yn + rk[i] * v[i])
                                           * g_ref[0, rows[i], lanes[i]]).astype(y_ref.dtype)


def _rwkv_rec(r, k, v, wl, al, g, par, s0_blk, n_valid):
    B, T, D = r.shape
    P = 2 * RWKV_HEAD
    L = RWKV_CHUNK
    assert T % L == 0 and (n_valid == T or T == L)
    npair = D // P
    tb = _pick(T, (RWKV_CHUNKS_PER_STEP * L, L))
    ts = pl.BlockSpec((1, tb, D), lambda b, c: (b, c, 0))
    ss = pl.BlockSpec((1, npair, P, P), lambda b, c: (b, 0, 0, 0))
    return pl.pallas_call(
        functools.partial(_rwkv_rec_body, L=L, npair=npair, group=RWKV_GROUP, n_valid=min(n_valid, L)),
        grid=(B, T // tb),
        in_specs=[ts] * 6 + [pl.BlockSpec((SUBLANE, D), lambda b, c: (0, 0)), ss],
        out_specs=[ts, ss],
        out_shape=[jax.ShapeDtypeStruct((B, T, D), BF16),
                   jax.ShapeDtypeStruct((B, npair, P, P), F32)],
        compiler_params=_cparams(("parallel", "arbitrary")), name="rwkv_rec",
    )(r, k, v, wl, al, g, par, s0_blk)


def _pair_blockdiag(s):
    B, H, N, _ = s.shape
    s = s.reshape(B, H // 2, 2, N, N)
    z = jnp.zeros_like(s[:, :, 0])
    top = jnp.concatenate([s[:, :, 0], z], axis=-1)
    bot = jnp.concatenate([z, s[:, :, 1]], axis=-1)
    return jnp.concatenate([top, bot], axis=-2)


def _pair_unblock(sb):
    B, Pn, P, _ = sb.shape
    N = P // 2
    return jnp.stack([sb[:, :, :N, :N], sb[:, :, N:, N:]], axis=2).reshape(B, 2 * Pn, N, N)


def _pad_time(a, T2):
    return jnp.pad(a, ((0, 0), (0, T2 - a.shape[1]), (0, 0)))


def _retention_layer(x, mods, pos0, state, w_in, w_out, gn_gain):
    B, T, D = x.shape
    H, dk, dv = state.shape[1:]
    shift_m, scale_m, gate_m = mods
    half = dk // 2
    inv = ROPE_BASE ** (-jnp.arange(half, dtype=F32) / half)
    ang = (pos0 + jnp.arange(T)).astype(F32)[:, None] * inv[None, :]
    proj = _ret_proj(x, shift_m, scale_m, w_in, jnp.cos(ang), jnp.sin(ang), H * dk, dk)
    Tp = T if T % RET_CHUNK == 0 else -(-T // 64) * 64
    assert Tp == T or Tp <= RET_CHUNK
    y, s_new = _retention(_pad_time(proj, Tp), state, gn_gain, H, dk, dv, T)
    x = _mm_res(y[:, :T], w_out, x, gate_m, name="ret_out")
    return x, s_new


def _rwkv_layer(x, mods, shift_state, wkv_state, p):
    B, T, D = x.shape
    shift_m, scale_m, gate_m = mods
    (xr, xk, xv, tw, ta, tg), last = _rwkv_mix(x, shift_m, scale_m, p["mu"], shift_state,
                                               p["w1"], p["a1"], p["g1"])
    r, wl = _mm2(xr, p["w_r"], tw, p["w2"], F32, F32, name="rwkv_r_w")
    k, al = _mm2(xk, p["w_k"], ta, p["a2"], F32, BF16, name="rwkv_k_a")
    v, g = _mm2(xv, p["w_v"], tg, p["g2"], F32, BF16, name="rwkv_v_g")
    Tp = -(-T // RWKV_CHUNK) * RWKV_CHUNK
    ins = [_pad_time(t, Tp) for t in (r, k, v, wl, al, g)]
    y, s_blk = _rwkv_rec(*ins, p["par"], _pair_blockdiag(wkv_state), T)
    x = _mm_res(y[:, :T], p["w_o"], x, gate_m, name="rwkv_out")
    return x, _pair_unblock(s_blk), last


def _pad_cols(w, n):
    return jnp.pad(w, ((0, 0), (0, n - w.shape[1])))


def _pad_rows(w, n):
    return jnp.pad(w, ((0, n - w.shape[0]), (0, 0)))


def _run_group(x, mod, pos0, st_ret, st_wkv, st_shift, st_conv, w):
    depth = mod.shape[0]
    new_ret, new_wkv, new_shift, new_conv = [], [], [], []
    for i in range(depth):
        m = [mod[i, :, n][:, None, :] for n in range(6)]
        j = i // 2
        if i % 2 == 0:
            x, s = _retention_layer(x, m[:3], pos0, st_ret[j], w["ret_w_in"][j], w["ret_w_out"][j],
                                    w["ret_gn_gain"][j])
            new_ret.append(s)
        else:
            x, s, last = _rwkv_layer(x, m[:3], st_shift[j], st_wkv[j], w["rwkv"][j])
            new_wkv.append(s)
            new_shift.append(last)
        x, cs = _ffn(x, m[3], m[4], m[5], w["ffn_w_gate"][i], w["ffn_w_up"][i], w["ffn_w_down"][i],
                     w["ffn_conv_w"][i], w["ffn_conv_b"][i], st_conv[i])
        new_conv.append(cs)
    out = _norm_gain(x, w["final_gain"])
    return out, jnp.stack(new_ret), jnp.stack(new_wkv), jnp.stack(new_shift), jnp.stack(new_conv)


def kernel(x_prompt, x_sample, c_prompt, c_sample, state_ret, state_rwkv_wkv, state_rwkv_shift, state_ffn_conv, ada_w, ada_b, ret_w_in, ret_w_out, ret_gn_gain, rwkv_mu, rwkv_w_r, rwkv_w_k, rwkv_w_v, rwkv_w_o, rwkv_w0, rwkv_w1, rwkv_w2, rwkv_a0, rwkv_a1, rwkv_a2, rwkv_g1, rwkv_g2, rwkv_k_k, rwkv_k_a, rwkv_r_k, rwkv_gn_gain, ffn_w_gate, ffn_w_up, ffn_conv_w, ffn_conv_b, ffn_w_down, final_gain):
    B, T, D = x_prompt.shape
    Bs = x_sample.shape[0]
    depth = ada_w.shape[0]
    n_rwkv = rwkv_mu.shape[0]

    rows = -(-(B + Bs) // SUBLANE) * SUBLANE
    c_all = jnp.pad(jnp.concatenate([c_prompt, c_sample], axis=0), ((0, rows - B - Bs), (0, 0)))
    mod = _ada(c_all, ada_w, ada_b).reshape(depth, rows, 6, D)

    bf = lambda a: a.astype(BF16)
    rwkv = []
    for j in range(n_rwkv):
        lw = -(-rwkv_w1.shape[2] // LANE) * LANE
        la = -(-rwkv_a1.shape[2] // LANE) * LANE
        par = jnp.stack([rwkv_w0[j], rwkv_a0[j], rwkv_k_k[j], rwkv_k_a[j], rwkv_r_k[j].reshape(D),
                         rwkv_gn_gain[j], jnp.zeros((D,), F32), jnp.zeros((D,), F32)])
        rwkv.append(dict(
            mu=rwkv_mu[j], w_r=bf(rwkv_w_r[j]), w_k=bf(rwkv_w_k[j]), w_v=bf(rwkv_w_v[j]),
            w_o=bf(rwkv_w_o[j]),
            w1=bf(_pad_cols(rwkv_w1[j], lw)), w2=bf(_pad_rows(rwkv_w2[j], lw)),
            a1=bf(_pad_cols(rwkv_a1[j], la)), a2=bf(_pad_rows(rwkv_a2[j], la)),
            g1=bf(rwkv_g1[j]), g2=bf(rwkv_g2[j]), par=par))
    per_layer = lambda a: [bf(a[i]) for i in range(a.shape[0])]
    w = dict(ret_w_in=per_layer(ret_w_in), ret_w_out=per_layer(ret_w_out), ret_gn_gain=ret_gn_gain,
             rwkv=rwkv, ffn_w_gate=per_layer(ffn_w_gate), ffn_w_up=per_layer(ffn_w_up),
             ffn_w_down=per_layer(ffn_w_down), ffn_conv_w=ffn_conv_w, ffn_conv_b=ffn_conv_b,
             final_gain=final_gain)

    n_ret = state_ret.shape[0]
    F_ = ffn_w_gate.shape[2]
    z_ret = jnp.zeros((n_ret, B) + state_ret.shape[2:], F32)
    z_wkv = jnp.zeros((n_rwkv, B) + state_rwkv_wkv.shape[2:], F32)
    z_shift = jnp.zeros((n_rwkv, B, D), F32)
    z_conv = jnp.zeros((depth, B, CONV_W - 1, F_), F32)
    y_p, p_ret, p_wkv, p_shift, p_conv = _run_group(
        x_prompt, mod[:, :B], 0, z_ret, z_wkv, z_shift, z_conv, w)
    y_s, s_ret, s_wkv, s_shift, s_conv = _run_group(
        x_sample, mod[:, B:B + Bs], PAST_LEN, state_ret, state_rwkv_wkv, state_rwkv_shift,
        state_ffn_conv, w)
    return (y_p, y_s, p_ret, p_wkv, p_shift, p_conv, s_ret, s_wkv, s_shift, s_conv)
```

```python
import functools

import jax
import jax.numpy as jnp
from jax import lax
from jax.experimental import pallas as pl
from jax.experimental.pallas import tpu as pltpu

F32 = jnp.float32
BF16 = jnp.bfloat16

NORM_EPS = 1e-6
RET_GN_EPS = 1e-5
RWKV_GN_EPS = 64e-5
ROPE_BASE = 10000.0
PAST_LEN = 4096
CONV_W = 3

LANE = 128
SUBLANE = 8
VMEM_LIMIT_MB = 56

RET_CHUNK = 256
ROW_CHUNK = 256
RET_TBLOCK = 1024
RWKV_CHUNK = 64
RWKV_HEAD = 64
RWKV_CHUNKS_PER_STEP = 2
DECAY_SCALE_LOG2 = 0.6065306597126334 * 1.4426950408889634


def _pick(n, cands):
    for c in cands:
        if n % c == 0:
            return c
    return n


def _cparams(sem):
    return pltpu.CompilerParams(dimension_semantics=sem, vmem_limit_bytes=VMEM_LIMIT_MB << 20)


def _token_tiles(B, T):
    if T >= 512:
        return 1, _pick(T, (1024, 512, 256, 128))
    return B, T


def _silu(x):
    return x * jax.nn.sigmoid(x)


def _rows8(p):
    return jnp.broadcast_to(p[:, None, :], (p.shape[0], SUBLANE, p.shape[1]))


def _ada_body(c_ref, w_ref, b_ref, o_ref):
    c = c_ref[...]
    s = _silu(c).astype(BF16)
    o_ref[0] = jnp.dot(s, w_ref[0].astype(BF16), preferred_element_type=F32) + b_ref[0]


def _ada(c_all, ada_w, ada_b):
    depth, D, N = ada_w.shape
    R = c_all.shape[0]
    tn = _pick(N, (1024, 512, 256, 128))
    return pl.pallas_call(
        _ada_body,
        grid=(depth, N // tn),
        in_specs=[pl.BlockSpec((R, D), lambda l, j: (0, 0)),
                  pl.BlockSpec((1, D, tn), lambda l, j: (l, 0, j)),
                  pl.BlockSpec((1, 1, tn), lambda l, j: (l, 0, j))],
        out_specs=pl.BlockSpec((1, R, tn), lambda l, j: (l, 0, j)),
        out_shape=jax.ShapeDtypeStruct((depth, R, N), F32),
        compiler_params=_cparams(("parallel", "parallel")),
        name="ada_mod",
    )(c_all, ada_w, ada_b.reshape(depth, 1, N))


def _norm_gain_body(x_ref, g_ref, o_ref):
    x = x_ref[...]
    xn = x * lax.rsqrt(jnp.mean(x * x, -1, keepdims=True) + NORM_EPS)
    o_ref[...] = (xn * g_ref[...]).astype(o_ref.dtype)


def _norm_gain(x, gain):
    B, T, D = x.shape
    bb, tm = _token_tiles(B, T)
    xs = pl.BlockSpec((bb, tm, D), lambda b, t: (b, t, 0))
    return pl.pallas_call(
        _norm_gain_body, grid=(B // bb, T // tm),
        in_specs=[xs, pl.BlockSpec((1, 1, D), lambda b, t: (0, 0, 0))], out_specs=xs,
        out_shape=jax.ShapeDtypeStruct((B, T, D), F32),
        compiler_params=_cparams(("parallel", "parallel")), name="final_norm",
    )(x, gain.reshape(1, 1, D))


def _mm2_body(x_ref, w_ref, x2_ref, w2_ref, o_ref, o2_ref):
    bb, tm, K = x_ref.shape
    acc = jnp.dot(x_ref[...].reshape(bb * tm, K), w_ref[...], preferred_element_type=F32)
    o_ref[...] = acc.reshape(bb, tm, -1).astype(o_ref.dtype)
    acc2 = jnp.dot(x2_ref[...].reshape(bb * tm, x2_ref.shape[2]), w2_ref[...], preferred_element_type=F32)
    o2_ref[...] = acc2.reshape(bb, tm, -1).astype(o2_ref.dtype)


def _mm2(x, w, x2, w2, out_dtype, out2_dtype, name="mm2"):
    B, T, K = x.shape
    K2 = x2.shape[2]
    N = w.shape[1]
    assert w2.shape[1] == N
    bb, tm = _token_tiles(B, T)
    tn = _pick(N, (1024, 512, 256, 128))
    os_ = pl.BlockSpec((bb, tm, tn), lambda b, t, j: (b, t, j))
    return pl.pallas_call(
        _mm2_body,
        grid=(B // bb, T // tm, N // tn),
        in_specs=[pl.BlockSpec((bb, tm, K), lambda b, t, j: (b, t, 0)),
                  pl.BlockSpec((K, tn), lambda b, t, j: (0, j)),
                  pl.BlockSpec((bb, tm, K2), lambda b, t, j: (b, t, 0)),
                  pl.BlockSpec((K2, tn), lambda b, t, j: (0, j))],
        out_specs=[os_, os_],
        out_shape=[jax.ShapeDtypeStruct((B, T, N), out_dtype), jax.ShapeDtypeStruct((B, T, N), out2_dtype)],
        compiler_params=_cparams(("parallel", "parallel", "parallel")), name=name,
    )(x, w, x2, w2)


def _mm_res_body(y_ref, w_ref, x_ref, gate_ref, o_ref):
    bb, tm, K = y_ref.shape
    acc = jnp.dot(y_ref[...].reshape(bb * tm, K), w_ref[...], preferred_element_type=F32)
    o_ref[...] = x_ref[...] + gate_ref[...] * acc.reshape(bb, tm, -1)


def _mm_res(y, w, x, gate, name="mm_res"):
    B, T, K = y.shape
    N = w.shape[1]
    bb, tm = _token_tiles(B, T)
    tn = _pick(N, (1024, 512, 256, 128) if K <= 2048 else (512, 256, 128))
    return pl.pallas_call(
        _mm_res_body,
        grid=(B // bb, T // tm, N // tn),
        in_specs=[pl.BlockSpec((bb, tm, K), lambda b, t, j: (b, t, 0)),
                  pl.BlockSpec((K, tn), lambda b, t, j: (0, j)),
                  pl.BlockSpec((bb, tm, tn), lambda b, t, j: (b, t, j)),
                  pl.BlockSpec((bb, 1, tn), lambda b, t, j: (b, 0, j))],
        out_specs=pl.BlockSpec((bb, tm, tn), lambda b, t, j: (b, t, j)),
        out_shape=jax.ShapeDtypeStruct((B, T, N), F32),
        compiler_params=_cparams(("parallel", "parallel", "parallel")), name=name,
    )(y, w, x, gate)


def _norm_mod_rows(x, shift, scale):
    xn = x * lax.rsqrt(jnp.mean(x * x, -1, keepdims=True) + NORM_EPS)
    return xn * (1.0 + scale) + shift


def _ret_proj_body(x_ref, sh_ref, sc_ref, w_ref, cos_ref, sin_ref, o_ref, h_ref, *, n_rot, n_q, dk):
    j = pl.program_id(2)
    bb, tm, K = x_ref.shape
    tn = w_ref.shape[1]

    def rotary_tile(first):
        sc = jnp.where(j >= n_q, dk ** -0.5, 1.0).astype(F32)
        half = dk // 2
        rc = _pick(tm, (ROW_CHUNK,))
        for r0 in range(0, tm, rc):
            if first:
                h = _norm_mod_rows(x_ref[:, r0:r0 + rc, :], sh_ref[...], sc_ref[...]).astype(BF16)
                h_ref[:, r0:r0 + rc, :] = h
            else:
                h = h_ref[:, r0:r0 + rc, :]
            acc = jnp.dot(h.reshape(bb * rc, K), w_ref[...],
                          preferred_element_type=F32).reshape(bb, rc, tn)
            cos = cos_ref[r0:r0 + rc, :][None] * sc
            sin = sin_ref[r0:r0 + rc, :][None] * sc
            for hh in range(tn // dk):
                lo = hh * dk
                x1 = acc[:, :, lo:lo + half]
                x2 = acc[:, :, lo + half:lo + dk]
                o_ref[:, r0:r0 + rc, lo:lo + half] = (x1 * cos - x2 * sin).astype(o_ref.dtype)
                o_ref[:, r0:r0 + rc, lo + half:lo + dk] = (x1 * sin + x2 * cos).astype(o_ref.dtype)

    pl.when(j == 0)(functools.partial(rotary_tile, True))
    pl.when((j > 0) & (j < n_rot))(functools.partial(rotary_tile, False))

    @pl.when(j >= n_rot)
    def _():
        acc = jnp.dot(h_ref[...].reshape(bb * tm, K), w_ref[...], preferred_element_type=F32)
        o_ref[...] = acc.reshape(bb, tm, tn).astype(o_ref.dtype)


def _ret_proj(x, shift, scale, w_in, cos, sin, qk_dim, dk):
    B, T, K = x.shape
    N = w_in.shape[1]
    bb, tm = _token_tiles(B, T)
    tn = _pick(qk_dim, (1024, 512, 256))
    half = dk // 2
    ms = pl.BlockSpec((bb, 1, K), lambda b, t, j: (b, 0, 0))
    return pl.pallas_call(
        functools.partial(_ret_proj_body, n_rot=2 * qk_dim // tn, n_q=qk_dim // tn, dk=dk),
        grid=(B // bb, T // tm, N // tn),
        in_specs=[pl.BlockSpec((bb, tm, K), lambda b, t, j: (b, t, 0)), ms, ms,
                  pl.BlockSpec((K, tn), lambda b, t, j: (0, j)),
                  pl.BlockSpec((tm, half), lambda b, t, j: (t, 0)),
                  pl.BlockSpec((tm, half), lambda b, t, j: (t, 0))],
        out_specs=pl.BlockSpec((bb, tm, tn), lambda b, t, j: (b, t, j)),
        out_shape=jax.ShapeDtypeStruct((B, T, N), BF16),
        scratch_shapes=[pltpu.VMEM((bb, tm, K), BF16)],
        compiler_params=_cparams(("parallel", "parallel", "arbitrary")), name="ret_proj",
    )(x, shift, scale, w_in, cos, sin)


def _ret_body(q_ref, k_ref, v_ref, g_ref, gain_ref, intra_ref, cross_ref, into_ref, cd_ref, s0_ref,
              y_ref, s_ref):
    c = pl.program_id(2)
    L = intra_ref.shape[1]

    @pl.when(c == 0)
    def _():
        s_ref[...] = s0_ref[...]

    chunks = [slice(ci * L, (ci + 1) * L) for ci in range(q_ref.shape[1] // L)]
    q = [q_ref[0, rows, :] for rows in chunks]
    v = [v_ref[0, rows, :] for rows in chunks]
    scores = [(lax.dot_general(q[i], k_ref[0, rows, :], (((1,), (1,)), ((), ())),
                               preferred_element_type=F32) * intra_ref[0]).astype(BF16)
              for i, rows in enumerate(chunks)]
    kv = [lax.dot_general((k_ref[0, rows, :].astype(F32) * into_ref[0]).astype(BF16), v[i],
                          (((0,), (0,)), ((), ())), preferred_element_type=F32)
          for i, rows in enumerate(chunks)]
    pv = [jnp.dot(scores[i], v[i], preferred_element_type=F32) for i in range(len(chunks))]
    for i, rows in enumerate(chunks):
        S = s_ref[0, 0]
        o = pv[i] + jnp.dot(q[i], S.astype(BF16), preferred_element_type=F32) * cross_ref[0]
        s_ref[0, 0] = S * cd_ref[0] + kv[i]
        mu = jnp.mean(o, -1, keepdims=True)
        d = o - mu
        var = jnp.mean(d * d, -1, keepdims=True)
        g = g_ref[0, rows, :].astype(F32)
        y_ref[0, rows, :] = (d * lax.rsqrt(var + RET_GN_EPS) * gain_ref[...] * _silu(g)).astype(y_ref.dtype)


def _retention(proj, state, gain, H, dk, dv, n_valid):
    B, T, _ = proj.shape
    L = min(RET_CHUNK, T)
    nv = min(n_valid, L)
    log_g = jnp.log1p(-(2.0 ** (-5.0 - jnp.arange(H, dtype=F32))))
    idx = jnp.arange(L, dtype=F32)
    diff = idx[:, None] - idx[None, :]
    intra = jnp.where(diff >= 0, jnp.exp(log_g[:, None, None] * jnp.maximum(diff, 0.0)), 0.0)
    cross = jnp.exp(log_g[:, None] * (idx[None, :] + 1.0))[:, :, None]
    into = jnp.exp(log_g[:, None] * (nv - 1.0 - idx[None, :]))[:, :, None]
    cd = jnp.exp(log_g * nv)[:, None, None]
    tb = _pick(T, (RET_TBLOCK, L))
    kq = (H * dk) // dk
    vq = (2 * H * dk) // dv
    return pl.pallas_call(
        _ret_body,
        grid=(B, H, T // tb),
        in_specs=[pl.BlockSpec((1, tb, dk), lambda b, h, c: (b, c, h)),
                  pl.BlockSpec((1, tb, dk), lambda b, h, c: (b, c, kq + h)),
                  pl.BlockSpec((1, tb, dv), lambda b, h, c: (b, c, vq + h)),
                  pl.BlockSpec((1, tb, dv), lambda b, h, c: (b, c, vq + H + h)),
                  pl.BlockSpec((1, dv), lambda b, h, c: (0, h)),
                  pl.BlockSpec((1, L, L), lambda b, h, c: (h, 0, 0)),
                  pl.BlockSpec((1, L, 1), lambda b, h, c: (h, 0, 0)),
                  pl.BlockSpec((1, L, 1), lambda b, h, c: (h, 0, 0)),
                  pl.BlockSpec((1, 1, 1), lambda b, h, c: (h, 0, 0)),
                  pl.BlockSpec((1, 1, dk, dv), lambda b, h, c: (b, h, 0, 0))],
        out_specs=[pl.BlockSpec((1, tb, dv), lambda b, h, c: (b, c, h)),
                   pl.BlockSpec((1, 1, dk, dv), lambda b, h, c: (b, h, 0, 0))],
        out_shape=[jax.ShapeDtypeStruct((B, T, H * dv), BF16),
                   jax.ShapeDtypeStruct((B, H, dk, dv), F32)],
        compiler_params=_cparams(("parallel", "parallel", "arbitrary")), name="retention",
    )(proj, proj, proj, proj, gain.reshape(1, H * dv), intra, cross, into, cd, state)


def _ffn_up_body(x_ref, sh_ref, sc_ref, wg_ref, wu_ref, cw_ref, cb_ref, cs_ref, o_ref, tail_ref,
                 ext_ref, h_ref):
    t = pl.program_id(1)
    f = pl.program_id(2)
    bb, tm, D = x_ref.shape
    tf = wg_ref.shape[1]
    pad = SUBLANE

    @pl.when(t == 0)
    def _():
        tail_ref[f] = cs_ref[...]

    ext_ref[:, pad - 2:pad, :] = tail_ref[f]

    def tiles(a):
        return a.reshape(bb, a.shape[1] // SUBLANE, SUBLANE, tf)

    def column_tile(first):
        cw = cw_ref[...]
        rc = _pick(tm, (ROW_CHUNK,))
        for r0 in range(0, tm, rc):
            if first:
                h = _norm_mod_rows(x_ref[:, r0:r0 + rc, :], sh_ref[...], sc_ref[...]).astype(BF16)
                h_ref[:, r0:r0 + rc, :] = h
            else:
                h = h_ref[:, r0:r0 + rc, :]
            h = h.reshape(bb * rc, D)
            u = jnp.dot(h, wg_ref[...], preferred_element_type=F32).reshape(bb, rc, tf)
            up = jnp.dot(h, wu_ref[...], preferred_element_type=F32).reshape(bb, rc, tf)
            ext_ref[:, pad + r0:pad + r0 + rc, :] = u
            conv = (cb_ref[...] + tiles(ext_ref[:, pad - 2 + r0:pad - 2 + r0 + rc, :]) * cw[0]
                    + tiles(ext_ref[:, pad - 1 + r0:pad - 1 + r0 + rc, :]) * cw[1] + tiles(u) * cw[2])
            o_ref[:, r0:r0 + rc, :] = (_silu(conv) * tiles(up)).reshape(bb, rc, tf).astype(o_ref.dtype)

    pl.when(f == 0)(functools.partial(column_tile, True))
    pl.when(f > 0)(functools.partial(column_tile, False))
    tail_ref[f] = ext_ref[:, pad + tm - 2:pad + tm, :]


def _ffn(x, shift, scale, gate, wg, wu, wd, conv_w, conv_b, conv_state):
    B, T, D = x.shape
    Fd = wg.shape[1]
    bb, tm = _token_tiles(B, T)
    tf = _pick(Fd, (512, 256, 128))
    nf = Fd // tf
    ms = pl.BlockSpec((bb, 1, D), lambda b, t, f: (b, 0, 0))
    act, tails = pl.pallas_call(
        _ffn_up_body,
        grid=(B // bb, T // tm, nf),
        in_specs=[pl.BlockSpec((bb, tm, D), lambda b, t, f: (b, t, 0)), ms, ms,
                  pl.BlockSpec((D, tf), lambda b, t, f: (0, f)),
                  pl.BlockSpec((D, tf), lambda b, t, f: (0, f)),
                  pl.BlockSpec((CONV_W, SUBLANE, tf), lambda b, t, f: (0, 0, f)),
                  pl.BlockSpec((SUBLANE, tf), lambda b, t, f: (0, f)),
                  pl.BlockSpec((bb, CONV_W - 1, tf), lambda b, t, f: (b, 0, f))],
        out_specs=[pl.BlockSpec((bb, tm, tf), lambda b, t, f: (b, t, f)),
                   pl.BlockSpec((nf, bb, CONV_W - 1, tf), lambda b, t, f: (0, b, 0, 0))],
        out_shape=[jax.ShapeDtypeStruct((B, T, Fd), BF16),
                   jax.ShapeDtypeStruct((nf, B, CONV_W - 1, tf), F32)],
        scratch_shapes=[pltpu.VMEM((bb, tm + SUBLANE, tf), F32), pltpu.VMEM((bb, tm, D), BF16)],
        compiler_params=_cparams(("parallel", "arbitrary", "arbitrary")), name="ffn_up",
    )(x, shift, scale, wg, wu, _rows8(conv_w), _rows8(conv_b.reshape(1, Fd))[0], conv_state)
    y = _mm_res(act, wd, x, gate, name="ffn_down")
    return y, tails.transpose(1, 2, 0, 3).reshape(B, CONV_W - 1, Fd)


def _rwkv_mix_body(x_ref, sh_ref, sc_ref, mu_ref, st_ref, w1_ref, a1_ref, g1_ref,
                   xr_ref, xk_ref, xv_ref, tw_ref, ta_ref, tg_ref, last_ref, hbuf_ref, carry_ref):
    t = pl.program_id(1)
    tm = x_ref.shape[1]
    pad = SUBLANE
    x = x_ref[0]
    xn = x * lax.rsqrt(jnp.mean(x * x, -1, keepdims=True) + NORM_EPS)
    h = xn * (1.0 + sc_ref[0]) + sh_ref[0]

    @pl.when(t == 0)
    def _():
        carry_ref[...] = st_ref[0]

    hbuf_ref[pad - 1:pad, :] = carry_ref[...]
    hbuf_ref[pad:, :] = h
    xx = hbuf_ref[pad - 1:pad - 1 + tm, :] - h
    D = h.shape[1]
    h3 = h.reshape(tm // SUBLANE, SUBLANE, D)
    xx3 = xx.reshape(tm // SUBLANE, SUBLANE, D)

    def mix(n):
        return (h3 + xx3 * mu_ref[n][None]).reshape(tm, D).astype(BF16)

    xr_ref[0] = mix(0)
    xk_ref[0] = mix(2)
    xv_ref[0] = mix(3)
    tw_ref[0] = jnp.tanh(jnp.dot(mix(1), w1_ref[...], preferred_element_type=F32)).astype(tw_ref.dtype)
    ta_ref[0] = jnp.dot(mix(4), a1_ref[...], preferred_element_type=F32).astype(ta_ref.dtype)
    tg_ref[0] = jax.nn.sigmoid(jnp.dot(mix(5), g1_ref[...], preferred_element_type=F32)).astype(tg_ref.dtype)
    last = hbuf_ref[pad + tm - 1:pad + tm, :]
    carry_ref[...] = last
    last_ref[0] = last


def _rwkv_mix(x, shift, scale, mu, shift_state, w1, a1, g1):
    B, T, D = x.shape
    tm = _pick(T, (512, 256, 128))
    xs = pl.BlockSpec((1, tm, D), lambda b, t: (b, t, 0))
    ms = pl.BlockSpec((1, 1, D), lambda b, t: (b, 0, 0))
    loras = (w1, a1, g1)
    outs = pl.pallas_call(
        _rwkv_mix_body, grid=(B, T // tm),
        in_specs=[xs, ms, ms, pl.BlockSpec((6, SUBLANE, D), lambda b, t: (0, 0, 0)), ms]
        + [pl.BlockSpec(w.shape, lambda b, t: (0, 0)) for w in loras],
        out_specs=[xs] * 3 + [pl.BlockSpec((1, tm, w.shape[1]), lambda b, t: (b, t, 0)) for w in loras] + [ms],
        out_shape=[jax.ShapeDtypeStruct((B, T, D), BF16)] * 3
        + [jax.ShapeDtypeStruct((B, T, w.shape[1]), BF16) for w in loras]
        + [jax.ShapeDtypeStruct((B, 1, D), F32)],
        scratch_shapes=[pltpu.VMEM((tm + SUBLANE, D), F32), pltpu.VMEM((1, D), F32)],
        compiler_params=_cparams(("parallel", "arbitrary")), name="rwkv_mix",
    )(x, shift, scale, _rows8(mu), shift_state.reshape(B, 1, D), *loras)
    return outs[:6], outs[6].reshape(B, D)


def _dot(a, b):
    return jnp.dot(a.astype(BF16), b.astype(BF16), preferred_element_type=F32)


def _dot_nt(a, b):
    return lax.dot_general(a.astype(BF16), b.astype(BF16), (((1,), (1,)), ((), ())),
                           preferred_element_type=F32)


def _dot_tn(a, b):
    return lax.dot_general(a.astype(BF16), b.astype(BF16), (((0,), (0,)), ((), ())),
                           preferred_element_type=F32)


def _rwkv_rec_body(r_ref, k_ref, v_ref, wl_ref, al_ref, g_ref, par_ref, s0_ref, y_ref, s_ref,
                   *, L, npair, n_valid):
    c = pl.program_id(1)
    N = RWKV_HEAD
    P = 2 * N

    lane_l = lax.broadcasted_iota(jnp.int32, (L, P), 1)
    head0_l = lane_l < N
    r2 = lax.broadcasted_iota(jnp.int32, (2 * L, 2 * L), 0)
    c2 = lax.broadcasted_iota(jnp.int32, (2 * L, 2 * L), 1)
    same = (r2 >= L) == (c2 >= L)
    strict = same & (c2 < r2)
    incl = same & (c2 <= r2)
    rowhead = (lax.broadcasted_iota(jnp.int32, (2 * L, P), 0) >= L) == \
              (lax.broadcasted_iota(jnp.int32, (2 * L, P), 1) >= N)
    ip = lax.broadcasted_iota(jnp.int32, (P, P), 0)
    jp = lax.broadcasted_iota(jnp.int32, (P, P), 1)
    blockdiag = (ip >= N) == (jp >= N)
    seg = blockdiag.astype(BF16)
    live = lax.broadcasted_iota(jnp.int32, (L, P), 0) < n_valid

    def segsum(x):
        return jnp.dot(x.astype(BF16), seg, preferred_element_type=F32)

    row_l = lax.broadcasted_iota(jnp.int32, (L, P), 0)

    def cumsum_rows(x):
        s = 1
        while s < L:
            if s < SUBLANE:
                x = x + jnp.where(row_l >= s, pltpu.roll(x, s, 0), 0.0)
            else:
                x = x + jnp.concatenate([jnp.zeros((s, P), F32), x[:L - s]], axis=0)
            s *= 2
        return x

    def stack_heads(x):
        return jnp.concatenate([jnp.where(head0_l, x, 0.0), jnp.where(head0_l, 0.0, x)], axis=0)

    def fold_heads(x2):
        return x2[:L] + x2[L:]

    def twice(x):
        return jnp.concatenate([x, x], axis=0)

    @pl.when(c == 0)
    def _():
        s_ref[...] = s0_ref[...]

    def segsum_all(xs, per=4):
        outs = []
        for j in range(0, len(xs), per):
            part = xs[j:j + per]
            out = segsum(jnp.concatenate(part, axis=0))
            outs += [out[i * L:(i + 1) * L] for i in range(len(part))]
        return outs

    n = npair
    lanes = [slice(p * P, (p + 1) * P) for p in range(n)]
    par = [par_ref[:, ln] for ln in lanes]

    def stage_gram(ci):
        rows = slice(ci * L, (ci + 1) * L)
        r = [r_ref[0, rows, ln] for ln in lanes]
        k = [k_ref[0, rows, ln] for ln in lanes]
        v = [v_ref[0, rows, ln] for ln in lanes]
        lnd, a, kkraw, kmod = [], [], [], []
        for i in range(n):
            w0, a0, k_k, k_a = (par[i][j:j + 1] for j in range(4))
            lnd.append(-DECAY_SCALE_LOG2 * jax.nn.sigmoid(w0 + wl_ref[0, rows, lanes[i]]))
            a.append(jax.nn.sigmoid(a0 + al_ref[0, rows, lanes[i]]))
            kkraw.append(k[i] * k_k)
            kmod.append(k[i] * ((1.0 - k_a) + a[i] * k_a))
        sums = segsum_all([x * x for x in kkraw])
        kk = [kkraw[i] * lax.rsqrt(jnp.maximum(sums[i], 1e-24)) for i in range(n)]
        rkr = [r[i] * kmod[i] * par[i][4:5] for i in range(n)]
        if n_valid < L:
            lnd = [jnp.where(live, x, 0.0) for x in lnd]
            kk = [jnp.where(live, x, 0.0) for x in kk]
            kmod = [jnp.where(live, x, 0.0) for x in kmod]
            v = [jnp.where(live, x, 0.0) for x in v]
        cum = [cumsum_rows(x) for x in lnd]
        beta, kappa, rho, a2, lhs = [], [], [], [], []
        for i in range(n):
            e_neg = jnp.exp2(-cum[i])
            beta.append((kk[i] * a[i] * e_neg).astype(BF16))
            kappa.append((kmod[i] * e_neg).astype(BF16))
            rho.append(r[i] * jnp.exp2(cum[i]))
            a2.append(stack_heads(kk[i] * jnp.exp2(cum[i] - lnd[i])))
            lhs.append(jnp.concatenate([a2[i], stack_heads(rho[i])], axis=0).astype(BF16))
        gb = [_dot_nt(lhs[i], twice(beta[i])) for i in range(n)]
        gk = [_dot_nt(lhs[i], twice(kappa[i])) for i in range(n)]
        nmat = [jnp.where(strict, x[:2 * L], 0.0) for x in gb]
        pb = [jnp.where(incl, x[2 * L:], 0.0).astype(BF16) for x in gb]
        v2 = [twice(x).astype(BF16) for x in v]
        kv = [_dot(jnp.concatenate([jnp.where(strict, gk[i][:2 * L], 0.0),
                                    jnp.where(incl, gk[i][2 * L:], 0.0)], axis=0), v2[i]) for i in range(n)]
        akv = [jnp.where(rowhead, x[:2 * L], 0.0) for x in kv]
        pkv = [x[2 * L:] for x in kv]
        rk = segsum_all(rkr)
        return dict(rows=rows, v=v, cum=cum, beta=beta, kappa=kappa, rho=rho, a2=a2, nmat=nmat, pb=pb,
                    akv=akv, pkv=pkv, rk=rk)

    def stage_inverse(cx):
        nsq = max(L.bit_length() - 2, 0)
        corr = [-x for x in cx["nmat"]]
        pw = [x.astype(BF16) for x in cx["nmat"]]
        pw = [_dot(x, x) for x in pw]
        for it in range(nsq):
            pwb = [x.astype(BF16) for x in pw]
            if it < nsq - 1:
                res = [_dot(jnp.concatenate([pwb[i], corr[i].astype(BF16)], axis=0), pwb[i]) for i in range(n)]
                corr = [corr[i] + pw[i] + res[i][2 * L:] for i in range(n)]
                pw = [x[:2 * L] for x in res]
            else:
                corr = [corr[i] + pw[i] + _dot(corr[i], pwb[i]) for i in range(n)]
        cx["corr"] = corr

    def stage_apply(cx):
        v, beta, kappa = cx["v"], cx["beta"], cx["kappa"]
        both = [jnp.concatenate([cx["a2"][i], cx["akv"][i]], axis=1) for i in range(n)]
        both = [both[i] + _dot(cx["corr"][i], both[i]) for i in range(n)]
        at = [fold_heads(x[:, :P]) for x in both]
        w0_ = [fold_heads(x[:, P:]) for x in both]
        pbx = [_dot(cx["pb"][i], twice(jnp.concatenate([at[i], w0_[i]], axis=1))) for i in range(n)]
        cx["rho_t"] = [cx["rho"][i] - fold_heads(jnp.where(rowhead, pbx[i][:, :P], 0.0)) for i in range(n)]
        cx["o0"] = [fold_heads(jnp.where(rowhead, cx["pkv"][i] - pbx[i][:, P:], 0.0)) for i in range(n)]
        cx["gmat"] = [jnp.where(blockdiag, _dot_tn(at[i], beta[i]), 0.0) for i in range(n)]
        cx["umat"] = [jnp.where(blockdiag, _dot_tn(jnp.concatenate([v[i], -w0_[i]], axis=0),
                                                   jnp.concatenate([kappa[i], beta[i]], axis=0)), 0.0)
                      for i in range(n)]

    def stage_state(cx):
        S = [s_ref[0, p] for p in range(n)]
        Sb = [x.astype(BF16) for x in S]
        cx["o"] = [_dot_nt(cx["rho_t"][i], Sb[i]) + cx["o0"][i] for i in range(n)]
        for i in range(n):
            s_ref[0, i] = ((S[i] - _dot(Sb[i], cx["gmat"][i]) + cx["umat"][i])
                           * jnp.exp2(cx["cum"][i][L - 1:L, :]))

    def stage_mean(cx):
        mu = segsum_all(cx["o"])
        cx["d"] = [cx["o"][i] - mu[i] * (1.0 / N) for i in range(n)]

    def stage_out(cx):
        d, rows = cx["d"], cx["rows"]
        var = segsum_all([x * x for x in d])
        for i in range(n):
            yn = d[i] * lax.rsqrt(var[i] * (1.0 / N) + RWKV_GN_EPS) * par[i][5:6]
            y_ref[0, rows, lanes[i]] = ((yn + cx["rk"][i] * cx["v"][i])
                                        * g_ref[0, rows, lanes[i]]).astype(y_ref.dtype)

    prev = None
    for ci in range(r_ref.shape[1] // L):
        cx = stage_gram(ci)
        if prev is not None:
            stage_mean(prev)
        stage_inverse(cx)
        if prev is not None:
            stage_out(prev)
        stage_apply(cx)
        stage_state(cx)
        prev = cx
    stage_mean(prev)
    stage_out(prev)


def _rwkv_rec(r, k, v, wl, al, g, par, s0_blk, n_valid):
    B, T, D = r.shape
    P = 2 * RWKV_HEAD
    L = RWKV_CHUNK
    assert T % L == 0 and (n_valid == T or T == L)
    npair = D // P
    tb = _pick(T, (RWKV_CHUNKS_PER_STEP * L, L))
    ts = pl.BlockSpec((1, tb, D), lambda b, c: (b, c, 0))
    ss = pl.BlockSpec((1, npair, P, P), lambda b, c: (b, 0, 0, 0))
    return pl.pallas_call(
        functools.partial(_rwkv_rec_body, L=L, npair=npair, n_valid=min(n_valid, L)),
        grid=(B, T // tb),
        in_specs=[ts] * 6 + [pl.BlockSpec((SUBLANE, D), lambda b, c: (0, 0)), ss],
        out_specs=[ts, ss],
        out_shape=[jax.ShapeDtypeStruct((B, T, D), BF16),
                   jax.ShapeDtypeStruct((B, npair, P, P), F32)],
        compiler_params=_cparams(("parallel", "arbitrary")), name="rwkv_rec",
    )(r, k, v, wl, al, g, par, s0_blk)


def _pair_blockdiag(s):
    B, H, N, _ = s.shape
    s = s.reshape(B, H // 2, 2, N, N)
    z = jnp.zeros_like(s[:, :, 0])
    top = jnp.concatenate([s[:, :, 0], z], axis=-1)
    bot = jnp.concatenate([z, s[:, :, 1]], axis=-1)
    return jnp.concatenate([top, bot], axis=-2)


def _pair_unblock(sb):
    B, Pn, P, _ = sb.shape
    N = P // 2
    return jnp.stack([sb[:, :, :N, :N], sb[:, :, N:, N:]], axis=2).reshape(B, 2 * Pn, N, N)


def _pad_time(a, T2):
    return jnp.pad(a, ((0, 0), (0, T2 - a.shape[1]), (0, 0)))


def _retention_layer(x, mods, pos0, state, w_in, w_out, gn_gain):
    B, T, D = x.shape
    H, dk, dv = state.shape[1:]
    shift_m, scale_m, gate_m = mods
    half = dk // 2
    inv = ROPE_BASE ** (-jnp.arange(half, dtype=F32) / half)
    ang = (pos0 + jnp.arange(T)).astype(F32)[:, None] * inv[None, :]
    proj = _ret_proj(x, shift_m, scale_m, w_in, jnp.cos(ang), jnp.sin(ang), H * dk, dk)
    Tp = T if T % RET_CHUNK == 0 else -(-T // 64) * 64
    assert Tp == T or Tp <= RET_CHUNK
    y, s_new = _retention(_pad_time(proj, Tp), state, gn_gain, H, dk, dv, T)
    x = _mm_res(y[:, :T], w_out, x, gate_m, name="ret_out")
    return x, s_new


def _rwkv_layer(x, mods, shift_state, wkv_state, p):
    B, T, D = x.shape
    shift_m, scale_m, gate_m = mods
    (xr, xk, xv, tw, ta, tg), last = _rwkv_mix(x, shift_m, scale_m, p["mu"], shift_state,
                                               p["w1"], p["a1"], p["g1"])
    r, wl = _mm2(xr, p["w_r"], tw, p["w2"], F32, F32, name="rwkv_r_w")
    k, al = _mm2(xk, p["w_k"], ta, p["a2"], F32, BF16, name="rwkv_k_a")
    v, g = _mm2(xv, p["w_v"], tg, p["g2"], F32, BF16, name="rwkv_v_g")
    Tp = -(-T // RWKV_CHUNK) * RWKV_CHUNK
    ins = [_pad_time(t, Tp) for t in (r, k, v, wl, al, g)]
    y, s_blk = _rwkv_rec(*ins, p["par"], _pair_blockdiag(wkv_state), T)
    x = _mm_res(y[:, :T], p["w_o"], x, gate_m, name="rwkv_out")
    return x, _pair_unblock(s_blk), last


def _pad_cols(w, n):
    return jnp.pad(w, ((0, 0), (0, n - w.shape[1])))


def _pad_rows(w, n):
    return jnp.pad(w, ((0, n - w.shape[0]), (0, 0)))


def _run_group(x, mod, pos0, st_ret, st_wkv, st_shift, st_conv, w):
    depth = mod.shape[0]
    new_ret, new_wkv, new_shift, new_conv = [], [], [], []
    for i in range(depth):
        m = [mod[i, :, n][:, None, :] for n in range(6)]
        j = i // 2
        if i % 2 == 0:
            x, s = _retention_layer(x, m[:3], pos0, st_ret[j], w["ret_w_in"][j], w["ret_w_out"][j],
                                    w["ret_gn_gain"][j])
            new_ret.append(s)
        else:
            x, s, last = _rwkv_layer(x, m[:3], st_shift[j], st_wkv[j], w["rwkv"][j])
            new_wkv.append(s)
            new_shift.append(last)
        x, cs = _ffn(x, m[3], m[4], m[5], w["ffn_w_gate"][i], w["ffn_w_up"][i], w["ffn_w_down"][i],
                     w["ffn_conv_w"][i], w["ffn_conv_b"][i], st_conv[i])
        new_conv.append(cs)
    out = _norm_gain(x, w["final_gain"])
    return out, jnp.stack(new_ret), jnp.stack(new_wkv), jnp.stack(new_shift), jnp.stack(new_conv)


def kernel(x_prompt, x_sample, c_prompt, c_sample, state_ret, state_rwkv_wkv, state_rwkv_shift, state_ffn_conv, ada_w, ada_b, ret_w_in, ret_w_out, ret_gn_gain, rwkv_mu, rwkv_w_r, rwkv_w_k, rwkv_w_v, rwkv_w_o, rwkv_w0, rwkv_w1, rwkv_w2, rwkv_a0, rwkv_a1, rwkv_a2, rwkv_g1, rwkv_g2, rwkv_k_k, rwkv_k_a, rwkv_r_k, rwkv_gn_gain, ffn_w_gate, ffn_w_up, ffn_conv_w, ffn_conv_b, ffn_w_down, final_gain):
    B, T, D = x_prompt.shape
    Bs = x_sample.shape[0]
    depth = ada_w.shape[0]
    n_rwkv = rwkv_mu.shape[0]

    rows = -(-(B + Bs) // SUBLANE) * SUBLANE
    c_all = jnp.pad(jnp.concatenate([c_prompt, c_sample], axis=0), ((0, rows - B - Bs), (0, 0)))
    mod = _ada(c_all, ada_w, ada_b).reshape(depth, rows, 6, D)

    bf = lambda a: a.astype(BF16)
    rwkv = []
    for j in range(n_rwkv):
        lw = -(-rwkv_w1.shape[2] // LANE) * LANE
        la = -(-rwkv_a1.shape[2] // LANE) * LANE
        par = jnp.stack([rwkv_w0[j], rwkv_a0[j], rwkv_k_k[j], rwkv_k_a[j], rwkv_r_k[j].reshape(D),
                         rwkv_gn_gain[j], jnp.zeros((D,), F32), jnp.zeros((D,), F32)])
        rwkv.append(dict(
            mu=rwkv_mu[j], w_r=bf(rwkv_w_r[j]), w_k=bf(rwkv_w_k[j]), w_v=bf(rwkv_w_v[j]),
            w_o=bf(rwkv_w_o[j]),
            w1=bf(_pad_cols(rwkv_w1[j], lw)), w2=bf(_pad_rows(rwkv_w2[j], lw)),
            a1=bf(_pad_cols(rwkv_a1[j], la)), a2=bf(_pad_rows(rwkv_a2[j], la)),
            g1=bf(rwkv_g1[j]), g2=bf(rwkv_g2[j]), par=par))
    per_layer = lambda a: [bf(a[i]) for i in range(a.shape[0])]
    w = dict(ret_w_in=per_layer(ret_w_in), ret_w_out=per_layer(ret_w_out), ret_gn_gain=ret_gn_gain,
             rwkv=rwkv, ffn_w_gate=per_layer(ffn_w_gate), ffn_w_up=per_layer(ffn_w_up),
             ffn_w_down=per_layer(ffn_w_down), ffn_conv_w=ffn_conv_w, ffn_conv_b=ffn_conv_b,
             final_gain=final_gain)

    n_ret = state_ret.shape[0]
    F_ = ffn_w_gate.shape[2]
    z_ret = jnp.zeros((n_ret, B) + state_ret.shape[2:], F32)
    z_wkv = jnp.zeros((n_rwkv, B) + state_rwkv_wkv.shape[2:], F32)
    z_shift = jnp.zeros((n_rwkv, B, D), F32)
    z_conv = jnp.zeros((depth, B, CONV_W - 1, F_), F32)
    y_p, p_ret, p_wkv, p_shift, p_conv = _run_group(
        x_prompt, mod[:, :B], 0, z_ret, z_wkv, z_shift, z_conv, w)
    y_s, s_ret, s_wkv, s_shift, s_conv = _run_group(
        x_sample, mod[:, B:B + Bs], PAST_LEN, state_ret, state_rwkv_wkv, state_rwkv_shift,
        state_ffn_conv, w)
    return (y_p, y_s, p_ret, p_wkv, p_shift, p_conv, s_ret, s_wkv, s_shift, s_conv)
```

```python
import functools

import jax
import jax.numpy as jnp
from jax import lax
from jax.experimental import pallas as pl
from jax.experimental.pallas import tpu as pltpu

F32 = jnp.float32
BF16 = jnp.bfloat16

NORM_EPS = 1e-6
RET_GN_EPS = 1e-5
RWKV_GN_EPS = 64e-5
ROPE_BASE = 10000.0
PAST_LEN = 4096
CONV_W = 3

LANE = 128
SUBLANE = 8
VMEM_LIMIT_MB = 56

RET_CHUNK = 256
ROW_CHUNK = 256
RET_TBLOCK = 1024
RWKV_CHUNK = 64
RWKV_HEAD = 64
RWKV_CHUNKS_PER_STEP = 2
DECAY_SCALE_LOG2 = 0.6065306597126334 * 1.4426950408889634


def _pick(n, cands):
    for c in cands:
        if n % c == 0:
            return c
    return n


def _cparams(sem):
    return pltpu.CompilerParams(dimension_semantics=sem, vmem_limit_bytes=VMEM_LIMIT_MB << 20)


def _token_tiles(B, T):
    if T >= 512:
        return 1, _pick(T, (1024, 512, 256, 128))
    return B, T


def _silu(x):
    return x * jax.nn.sigmoid(x)


def _rows8(p):
    return jnp.broadcast_to(p[:, None, :], (p.shape[0], SUBLANE, p.shape[1]))


def _ada_body(c_ref, w_ref, b_ref, o_ref):
    c = c_ref[...]
    s = _silu(c).astype(BF16)
    o_ref[0] = jnp.dot(s, w_ref[0].astype(BF16), preferred_element_type=F32) + b_ref[0]


def _ada(c_all, ada_w, ada_b):
    depth, D, N = ada_w.shape
    R = c_all.shape[0]
    tn = _pick(N, (1024, 512, 256, 128))
    return pl.pallas_call(
        _ada_body,
        grid=(depth, N // tn),
        in_specs=[pl.BlockSpec((R, D), lambda l, j: (0, 0)),
                  pl.BlockSpec((1, D, tn), lambda l, j: (l, 0, j)),
                  pl.BlockSpec((1, 1, tn), lambda l, j: (l, 0, j))],
        out_specs=pl.BlockSpec((1, R, tn), lambda l, j: (l, 0, j)),
        out_shape=jax.ShapeDtypeStruct((depth, R, N), F32),
        compiler_params=_cparams(("parallel", "parallel")),
        name="ada_mod",
    )(c_all, ada_w, ada_b.reshape(depth, 1, N))


def _norm_gain_body(x_ref, g_ref, o_ref):
    x = x_ref[...]
    xn = x * lax.rsqrt(jnp.mean(x * x, -1, keepdims=True) + NORM_EPS)
    o_ref[...] = (xn * g_ref[...]).astype(o_ref.dtype)


def _norm_gain(x, gain):
    B, T, D = x.shape
    bb, tm = _token_tiles(B, T)
    xs = pl.BlockSpec((bb, tm, D), lambda b, t: (b, t, 0))
    return pl.pallas_call(
        _norm_gain_body, grid=(B // bb, T // tm),
        in_specs=[xs, pl.BlockSpec((1, 1, D), lambda b, t: (0, 0, 0))], out_specs=xs,
        out_shape=jax.ShapeDtypeStruct((B, T, D), F32),
        compiler_params=_cparams(("parallel", "parallel")), name="final_norm",
    )(x, gain.reshape(1, 1, D))


def _mm2_body(x_ref, w_ref, x2_ref, w2_ref, b2_ref, o_ref, o2_ref, *, post):
    bb, tm, K = x_ref.shape
    K2 = x2_ref.shape[2]
    rc = _pick(tm, (ROW_CHUNK,))
    for r0 in range(0, tm, rc):
        acc2 = jnp.dot(x2_ref[:, r0:r0 + rc, :].reshape(bb * rc, K2), w2_ref[...], preferred_element_type=F32)
        acc2 = acc2.reshape(bb, rc // SUBLANE, SUBLANE, -1)
        if post == "log2_decay":
            acc2 = -DECAY_SCALE_LOG2 * jax.nn.sigmoid(acc2 + b2_ref[...])
        elif post == "sigmoid":
            acc2 = jax.nn.sigmoid(acc2 + b2_ref[...])
        o2_ref[:, r0:r0 + rc, :] = acc2.reshape(bb, rc, -1).astype(o2_ref.dtype)
        acc = jnp.dot(x_ref[:, r0:r0 + rc, :].reshape(bb * rc, K), w_ref[...], preferred_element_type=F32)
        o_ref[:, r0:r0 + rc, :] = acc.reshape(bb, rc, -1).astype(o_ref.dtype)


def _mm2(x, w, x2, w2, bias2, post, out_dtype, out2_dtype, name="mm2"):
    B, T, K = x.shape
    K2 = x2.shape[2]
    N = w.shape[1]
    assert w2.shape[1] == N
    bb, tm = _token_tiles(B, T)
    tn = _pick(N, (1024, 512, 256, 128))
    os_ = pl.BlockSpec((bb, tm, tn), lambda b, t, j: (b, t, j))
    return pl.pallas_call(
        functools.partial(_mm2_body, post=post),
        grid=(B // bb, T // tm, N // tn),
        in_specs=[pl.BlockSpec((bb, tm, K), lambda b, t, j: (b, t, 0)),
                  pl.BlockSpec((K, tn), lambda b, t, j: (0, j)),
                  pl.BlockSpec((bb, tm, K2), lambda b, t, j: (b, t, 0)),
                  pl.BlockSpec((K2, tn), lambda b, t, j: (0, j)),
                  pl.BlockSpec((SUBLANE, tn), lambda b, t, j: (0, j))],
        out_specs=[os_, os_],
        out_shape=[jax.ShapeDtypeStruct((B, T, N), out_dtype), jax.ShapeDtypeStruct((B, T, N), out2_dtype)],
        compiler_params=_cparams(("parallel", "parallel", "parallel")), name=name,
    )(x, w, x2, w2, _rows8(bias2.reshape(1, N))[0])


def _mm_res_body(y_ref, w_ref, x_ref, gate_ref, o_ref):
    bb, tm, K = y_ref.shape
    acc = jnp.dot(y_ref[...].reshape(bb * tm, K), w_ref[...], preferred_element_type=F32)
    o_ref[...] = x_ref[...] + gate_ref[...] * acc.reshape(bb, tm, -1)


def _mm_res(y, w, x, gate, name="mm_res"):
    B, T, K = y.shape
    N = w.shape[1]
    bb, tm = _token_tiles(B, T)
    tn = _pick(N, (1024, 512, 256, 128) if K <= 2048 else (512, 256, 128))
    return pl.pallas_call(
        _mm_res_body,
        grid=(B // bb, T // tm, N // tn),
        in_specs=[pl.BlockSpec((bb, tm, K), lambda b, t, j: (b, t, 0)),
                  pl.BlockSpec((K, tn), lambda b, t, j: (0, j)),
                  pl.BlockSpec((bb, tm, tn), lambda b, t, j: (b, t, j)),
                  pl.BlockSpec((bb, 1, tn), lambda b, t, j: (b, 0, j))],
        out_specs=pl.BlockSpec((bb, tm, tn), lambda b, t, j: (b, t, j)),
        out_shape=jax.ShapeDtypeStruct((B, T, N), F32),
        compiler_params=_cparams(("parallel", "parallel", "parallel")), name=name,
    )(y, w, x, gate)


def _norm_mod_rows(x, shift, scale):
    xn = x * lax.rsqrt(jnp.mean(x * x, -1, keepdims=True) + NORM_EPS)
    return xn * (1.0 + scale) + shift


def _ret_proj_body(x_ref, sh_ref, sc_ref, w_ref, cos_ref, sin_ref, o_ref, h_ref, *, n_rot, n_q, dk):
    j = pl.program_id(2)
    bb, tm, K = x_ref.shape
    tn = w_ref.shape[1]

    def rotary_tile(first):
        sc = jnp.where(j >= n_q, dk ** -0.5, 1.0).astype(F32)
        half = dk // 2
        rc = _pick(tm, (ROW_CHUNK,))
        for r0 in range(0, tm, rc):
            if first:
                h = _norm_mod_rows(x_ref[:, r0:r0 + rc, :], sh_ref[...], sc_ref[...]).astype(BF16)
                h_ref[:, r0:r0 + rc, :] = h
            else:
                h = h_ref[:, r0:r0 + rc, :]
            acc = jnp.dot(h.reshape(bb * rc, K), w_ref[...],
                          preferred_element_type=F32).reshape(bb, rc, tn)
            cos = cos_ref[r0:r0 + rc, :][None] * sc
            sin = sin_ref[r0:r0 + rc, :][None] * sc
            for hh in range(tn // dk):
                lo = hh * dk
                x1 = acc[:, :, lo:lo + half]
                x2 = acc[:, :, lo + half:lo + dk]
                o_ref[:, r0:r0 + rc, lo:lo + half] = (x1 * cos - x2 * sin).astype(o_ref.dtype)
                o_ref[:, r0:r0 + rc, lo + half:lo + dk] = (x1 * sin + x2 * cos).astype(o_ref.dtype)

    pl.when(j == 0)(functools.partial(rotary_tile, True))
    pl.when((j > 0) & (j < n_rot))(functools.partial(rotary_tile, False))

    @pl.when(j >= n_rot)
    def _():
        acc = jnp.dot(h_ref[...].reshape(bb * tm, K), w_ref[...], preferred_element_type=F32)
        o_ref[...] = acc.reshape(bb, tm, tn).astype(o_ref.dtype)


def _ret_proj(x, shift, scale, w_in, cos, sin, qk_dim, dk):
    B, T, K = x.shape
    N = w_in.shape[1]
    bb, tm = _token_tiles(B, T)
    tn = _pick(qk_dim, (1024, 512, 256))
    half = dk // 2
    ms = pl.BlockSpec((bb, 1, K), lambda b, t, j: (b, 0, 0))
    return pl.pallas_call(
        functools.partial(_ret_proj_body, n_rot=2 * qk_dim // tn, n_q=qk_dim // tn, dk=dk),
        grid=(B // bb, T // tm, N // tn),
        in_specs=[pl.BlockSpec((bb, tm, K), lambda b, t, j: (b, t, 0)), ms, ms,
                  pl.BlockSpec((K, tn), lambda b, t, j: (0, j)),
                  pl.BlockSpec((tm, half), lambda b, t, j: (t, 0)),
                  pl.BlockSpec((tm, half), lambda b, t, j: (t, 0))],
        out_specs=pl.BlockSpec((bb, tm, tn), lambda b, t, j: (b, t, j)),
        out_shape=jax.ShapeDtypeStruct((B, T, N), BF16),
        scratch_shapes=[pltpu.VMEM((bb, tm, K), BF16)],
        compiler_params=_cparams(("parallel", "parallel", "arbitrary")), name="ret_proj",
    )(x, shift, scale, w_in, cos, sin)


def _ret_body(q_ref, k_ref, v_ref, g_ref, gain_ref, intra_ref, cross_ref, into_ref, cd_ref, s0_ref,
              y_ref, s_ref):
    c = pl.program_id(2)
    L = intra_ref.shape[1]

    @pl.when(c == 0)
    def _():
        s_ref[...] = s0_ref[...]

    chunks = [slice(ci * L, (ci + 1) * L) for ci in range(q_ref.shape[1] // L)]
    q = [q_ref[0, rows, :] for rows in chunks]
    v = [v_ref[0, rows, :] for rows in chunks]
    scores = [(lax.dot_general(q[i], k_ref[0, rows, :], (((1,), (1,)), ((), ())),
                               preferred_element_type=F32) * intra_ref[0]).astype(BF16)
              for i, rows in enumerate(chunks)]
    kv = [lax.dot_general((k_ref[0, rows, :].astype(F32) * into_ref[0]).astype(BF16), v[i],
                          (((0,), (0,)), ((), ())), preferred_element_type=F32)
          for i, rows in enumerate(chunks)]
    pv = [jnp.dot(scores[i], v[i], preferred_element_type=F32) for i in range(len(chunks))]
    for i, rows in enumerate(chunks):
        S = s_ref[0, 0]
        o = pv[i] + jnp.dot(q[i], S.astype(BF16), preferred_element_type=F32) * cross_ref[0]
        s_ref[0, 0] = S * cd_ref[0] + kv[i]
        mu = jnp.mean(o, -1, keepdims=True)
        d = o - mu
        var = jnp.mean(d * d, -1, keepdims=True)
        g = g_ref[0, rows, :].astype(F32)
        y_ref[0, rows, :] = (d * lax.rsqrt(var + RET_GN_EPS) * gain_ref[...] * _silu(g)).astype(y_ref.dtype)


def _retention(proj, state, gain, H, dk, dv, n_valid):
    B, T, _ = proj.shape
    L = min(RET_CHUNK, T)
    nv = min(n_valid, L)
    log_g = jnp.log1p(-(2.0 ** (-5.0 - jnp.arange(H, dtype=F32))))
    idx = jnp.arange(L, dtype=F32)
    diff = idx[:, None] - idx[None, :]
    intra = jnp.where(diff >= 0, jnp.exp(log_g[:, None, None] * jnp.maximum(diff, 0.0)), 0.0)
    cross = jnp.exp(log_g[:, None] * (idx[None, :] + 1.0))[:, :, None]
    into = jnp.exp(log_g[:, None] * (nv - 1.0 - idx[None, :]))[:, :, None]
    cd = jnp.exp(log_g * nv)[:, None, None]
    tb = _pick(T, (RET_TBLOCK, L))
    kq = (H * dk) // dk
    vq = (2 * H * dk) // dv
    return pl.pallas_call(
        _ret_body,
        grid=(B, H, T // tb),
        in_specs=[pl.BlockSpec((1, tb, dk), lambda b, h, c: (b, c, h)),
                  pl.BlockSpec((1, tb, dk), lambda b, h, c: (b, c, kq + h)),
                  pl.BlockSpec((1, tb, dv), lambda b, h, c: (b, c, vq + h)),
                  pl.BlockSpec((1, tb, dv), lambda b, h, c: (b, c, vq + H + h)),
                  pl.BlockSpec((1, dv), lambda b, h, c: (0, h)),
                  pl.BlockSpec((1, L, L), lambda b, h, c: (h, 0, 0)),
                  pl.BlockSpec((1, L, 1), lambda b, h, c: (h, 0, 0)),
                  pl.BlockSpec((1, L, 1), lambda b, h, c: (h, 0, 0)),
                  pl.BlockSpec((1, 1, 1), lambda b, h, c: (h, 0, 0)),
                  pl.BlockSpec((1, 1, dk, dv), lambda b, h, c: (b, h, 0, 0))],
        out_specs=[pl.BlockSpec((1, tb, dv), lambda b, h, c: (b, c, h)),
                   pl.BlockSpec((1, 1, dk, dv), lambda b, h, c: (b, h, 0, 0))],
        out_shape=[jax.ShapeDtypeStruct((B, T, H * dv), BF16),
                   jax.ShapeDtypeStruct((B, H, dk, dv), F32)],
        compiler_params=_cparams(("parallel", "parallel", "arbitrary")), name="retention",
    )(proj, proj, proj, proj, gain.reshape(1, H * dv), intra, cross, into, cd, state)


def _ffn_up_body(x_ref, sh_ref, sc_ref, wg_ref, wu_ref, cw_ref, cb_ref, cs_ref, o_ref, tail_ref,
                 ext_ref, h_ref):
    t = pl.program_id(1)
    f = pl.program_id(2)
    bb, tm, D = x_ref.shape
    tf = wg_ref.shape[1]
    pad = SUBLANE

    @pl.when(t == 0)
    def _():
        tail_ref[f] = cs_ref[...]

    ext_ref[:, pad - 2:pad, :] = tail_ref[f]

    def tiles(a):
        return a.reshape(bb, a.shape[1] // SUBLANE, SUBLANE, tf)

    def column_tile(first):
        cw = cw_ref[...]
        rc = _pick(tm, (ROW_CHUNK,))
        for r0 in range(0, tm, rc):
            if first:
                h = _norm_mod_rows(x_ref[:, r0:r0 + rc, :], sh_ref[...], sc_ref[...]).astype(BF16)
                h_ref[:, r0:r0 + rc, :] = h
            else:
                h = h_ref[:, r0:r0 + rc, :]
            h = h.reshape(bb * rc, D)
            u = jnp.dot(h, wg_ref[...], preferred_element_type=F32).reshape(bb, rc, tf)
            up = jnp.dot(h, wu_ref[...], preferred_element_type=F32).reshape(bb, rc, tf)
            ext_ref[:, pad + r0:pad + r0 + rc, :] = u
            conv = (cb_ref[...] + tiles(ext_ref[:, pad - 2 + r0:pad - 2 + r0 + rc, :]) * cw[0]
                    + tiles(ext_ref[:, pad - 1 + r0:pad - 1 + r0 + rc, :]) * cw[1] + tiles(u) * cw[2])
            o_ref[:, r0:r0 + rc, :] = (_silu(conv) * tiles(up)).reshape(bb, rc, tf).astype(o_ref.dtype)

    pl.when(f == 0)(functools.partial(column_tile, True))
    pl.when(f > 0)(functools.partial(column_tile, False))
    tail_ref[f] = ext_ref[:, pad + tm - 2:pad + tm, :]


def _ffn(x, shift, scale, gate, wg, wu, wd, conv_w, conv_b, conv_state):
    B, T, D = x.shape
    Fd = wg.shape[1]
    bb, tm = _token_tiles(B, T)
    tf = _pick(Fd, (512, 256, 128))
    nf = Fd // tf
    ms = pl.BlockSpec((bb, 1, D), lambda b, t, f: (b, 0, 0))
    act, tails = pl.pallas_call(
        _ffn_up_body,
        grid=(B // bb, T // tm, nf),
        in_specs=[pl.BlockSpec((bb, tm, D), lambda b, t, f: (b, t, 0)), ms, ms,
                  pl.BlockSpec((D, tf), lambda b, t, f: (0, f)),
                  pl.BlockSpec((D, tf), lambda b, t, f: (0, f)),
                  pl.BlockSpec((CONV_W, SUBLANE, tf), lambda b, t, f: (0, 0, f)),
                  pl.BlockSpec((SUBLANE, tf), lambda b, t, f: (0, f)),
                  pl.BlockSpec((bb, CONV_W - 1, tf), lambda b, t, f: (b, 0, f))],
        out_specs=[pl.BlockSpec((bb, tm, tf), lambda b, t, f: (b, t, f)),
                   pl.BlockSpec((nf, bb, CONV_W - 1, tf), lambda b, t, f: (0, b, 0, 0))],
        out_shape=[jax.ShapeDtypeStruct((B, T, Fd), BF16),
                   jax.ShapeDtypeStruct((nf, B, CONV_W - 1, tf), F32)],
        scratch_shapes=[pltpu.VMEM((bb, tm + SUBLANE, tf), F32), pltpu.VMEM((bb, tm, D), BF16)],
        compiler_params=_cparams(("parallel", "arbitrary", "arbitrary")), name="ffn_up",
    )(x, shift, scale, wg, wu, _rows8(conv_w), _rows8(conv_b.reshape(1, Fd))[0], conv_state)
    y = _mm_res(act, wd, x, gate, name="ffn_down")
    return y, tails.transpose(1, 2, 0, 3).reshape(B, CONV_W - 1, Fd)


def _rwkv_mix_body(x_ref, sh_ref, sc_ref, mu_ref, st_ref, w1_ref, a1_ref, g1_ref,
                   xr_ref, xk_ref, xv_ref, tw_ref, ta_ref, tg_ref, last_ref, hbuf_ref, carry_ref):
    t = pl.program_id(1)
    tm = x_ref.shape[1]
    pad = SUBLANE
    x = x_ref[0]
    xn = x * lax.rsqrt(jnp.mean(x * x, -1, keepdims=True) + NORM_EPS)
    h = xn * (1.0 + sc_ref[0]) + sh_ref[0]

    @pl.when(t == 0)
    def _():
        carry_ref[...] = st_ref[0]

    hbuf_ref[pad - 1:pad, :] = carry_ref[...]
    hbuf_ref[pad:, :] = h
    xx = hbuf_ref[pad - 1:pad - 1 + tm, :] - h
    D = h.shape[1]
    h3 = h.reshape(tm // SUBLANE, SUBLANE, D)
    xx3 = xx.reshape(tm // SUBLANE, SUBLANE, D)

    def mix(n):
        return (h3 + xx3 * mu_ref[n][None]).reshape(tm, D).astype(BF16)

    xr_ref[0] = mix(0)
    xk_ref[0] = mix(2)
    xv_ref[0] = mix(3)
    tw_ref[0] = jnp.tanh(jnp.dot(mix(1), w1_ref[...], preferred_element_type=F32)).astype(tw_ref.dtype)
    ta_ref[0] = jnp.dot(mix(4), a1_ref[...], preferred_element_type=F32).astype(ta_ref.dtype)
    tg_ref[0] = jax.nn.sigmoid(jnp.dot(mix(5), g1_ref[...], preferred_element_type=F32)).astype(tg_ref.dtype)
    last = hbuf_ref[pad + tm - 1:pad + tm, :]
    carry_ref[...] = last
    last_ref[0] = last


def _rwkv_mix(x, shift, scale, mu, shift_state, w1, a1, g1):
    B, T, D = x.shape
    tm = _pick(T, (512, 256, 128))
    xs = pl.BlockSpec((1, tm, D), lambda b, t: (b, t, 0))
    ms = pl.BlockSpec((1, 1, D), lambda b, t: (b, 0, 0))
    loras = (w1, a1, g1)
    outs = pl.pallas_call(
        _rwkv_mix_body, grid=(B, T // tm),
        in_specs=[xs, ms, ms, pl.BlockSpec((6, SUBLANE, D), lambda b, t: (0, 0, 0)), ms]
        + [pl.BlockSpec(w.shape, lambda b, t: (0, 0)) for w in loras],
        out_specs=[xs] * 3 + [pl.BlockSpec((1, tm, w.shape[1]), lambda b, t: (b, t, 0)) for w in loras] + [ms],
        out_shape=[jax.ShapeDtypeStruct((B, T, D), BF16)] * 3
        + [jax.ShapeDtypeStruct((B, T, w.shape[1]), BF16) for w in loras]
        + [jax.ShapeDtypeStruct((B, 1, D), F32)],
        scratch_shapes=[pltpu.VMEM((tm + SUBLANE, D), F32), pltpu.VMEM((1, D), F32)],
        compiler_params=_cparams(("parallel", "arbitrary")), name="rwkv_mix",
    )(x, shift, scale, _rows8(mu), shift_state.reshape(B, 1, D), *loras)
    return outs[:6], outs[6].reshape(B, D)


def _dot(a, b):
    return jnp.dot(a.astype(BF16), b.astype(BF16), preferred_element_type=F32)


def _dot_nt(a, b):
    return lax.dot_general(a.astype(BF16), b.astype(BF16), (((1,), (1,)), ((), ())),
                           preferred_element_type=F32)


def _dot_tn(a, b):
    return lax.dot_general(a.astype(BF16), b.astype(BF16), (((0,), (0,)), ((), ())),
                           preferred_element_type=F32)


def _rwkv_rec_body(r_ref, k_ref, v_ref, ld_ref, a_ref, g_ref, par_ref, s0_ref, y_ref, s_ref,
                   *, L, npair, n_valid):
    c = pl.program_id(1)
    N = RWKV_HEAD
    P = 2 * N

    lane_l = lax.broadcasted_iota(jnp.int32, (L, P), 1)
    head0_l = lane_l < N
    r2 = lax.broadcasted_iota(jnp.int32, (2 * L, 2 * L), 0)
    c2 = lax.broadcasted_iota(jnp.int32, (2 * L, 2 * L), 1)
    same = (r2 >= L) == (c2 >= L)
    strict = same & (c2 < r2)
    incl = same & (c2 <= r2)
    rowhead = (lax.broadcasted_iota(jnp.int32, (2 * L, P), 0) >= L) == \
              (lax.broadcasted_iota(jnp.int32, (2 * L, P), 1) >= N)
    ip = lax.broadcasted_iota(jnp.int32, (P, P), 0)
    jp = lax.broadcasted_iota(jnp.int32, (P, P), 1)
    blockdiag = (ip >= N) == (jp >= N)
    seg = blockdiag.astype(BF16)
    live = lax.broadcasted_iota(jnp.int32, (L, P), 0) < n_valid

    def segsum(x):
        return jnp.dot(x.astype(BF16), seg, preferred_element_type=F32)

    row_l = lax.broadcasted_iota(jnp.int32, (L, P), 0)

    def cumsum_rows(x):
        s = 1
        while s < L:
            if s < SUBLANE:
                x = x + jnp.where(row_l >= s, pltpu.roll(x, s, 0), 0.0)
            else:
                x = x + jnp.concatenate([jnp.zeros((s, P), F32), x[:L - s]], axis=0)
            s *= 2
        return x

    def stack_heads(x):
        return jnp.concatenate([jnp.where(head0_l, x, 0.0), jnp.where(head0_l, 0.0, x)], axis=0)

    def fold_heads(x2):
        return x2[:L] + x2[L:]

    def twice(x):
        return jnp.concatenate([x, x], axis=0)

    @pl.when(c == 0)
    def _():
        s_ref[...] = s0_ref[...]

    def segsum_all(xs, per=4):
        outs = []
        for j in range(0, len(xs), per):
            part = xs[j:j + per]
            out = segsum(jnp.concatenate(part, axis=0))
            outs += [out[i * L:(i + 1) * L] for i in range(len(part))]
        return outs

    n = npair
    lanes = [slice(p * P, (p + 1) * P) for p in range(n)]
    par = [par_ref[:, ln] for ln in lanes]

    def stage_gram(ci):
        rows = slice(ci * L, (ci + 1) * L)
        r = [r_ref[0, rows, ln] for ln in lanes]
        k = [k_ref[0, rows, ln] for ln in lanes]
        v = [v_ref[0, rows, ln] for ln in lanes]
        lnd = [ld_ref[0, rows, ln] for ln in lanes]
        a = [a_ref[0, rows, ln] for ln in lanes]
        kkraw, kmod = [], []
        for i in range(n):
            k_k, k_a = par[i][0:1], par[i][1:2]
            kkraw.append(k[i] * k_k)
            kmod.append(k[i] * ((1.0 - k_a) + a[i] * k_a))
        sums = segsum_all([x * x for x in kkraw])
        kk = [kkraw[i] * lax.rsqrt(jnp.maximum(sums[i], 1e-24)) for i in range(n)]
        rkr = [r[i] * kmod[i] * par[i][2:3] for i in range(n)]
        if n_valid < L:
            lnd = [jnp.where(live, x, 0.0) for x in lnd]
            kk = [jnp.where(live, x, 0.0) for x in kk]
            kmod = [jnp.where(live, x, 0.0) for x in kmod]
            v = [jnp.where(live, x, 0.0) for x in v]
        cum = [cumsum_rows(x) for x in lnd]
        beta, kappa, rho, a2, lhs = [], [], [], [], []
        for i in range(n):
            e_neg = jnp.exp2(-cum[i])
            beta.append((kk[i] * a[i] * e_neg).astype(BF16))
            kappa.append((kmod[i] * e_neg).astype(BF16))
            rho.append(r[i] * jnp.exp2(cum[i]))
            a2.append(stack_heads(kk[i] * jnp.exp2(cum[i] - lnd[i])))
            lhs.append(jnp.concatenate([a2[i], stack_heads(rho[i])], axis=0).astype(BF16))
        gb = [_dot_nt(lhs[i], twice(beta[i])) for i in range(n)]
        gk = [_dot_nt(lhs[i], twice(kappa[i])) for i in range(n)]
        nmat = [jnp.where(strict, x[:2 * L], 0.0) for x in gb]
        pb = [jnp.where(incl, x[2 * L:], 0.0).astype(BF16) for x in gb]
        v2 = [twice(x).astype(BF16) for x in v]
        kv = [_dot(jnp.concatenate([jnp.where(strict, gk[i][:2 * L], 0.0),
                                    jnp.where(incl, gk[i][2 * L:], 0.0)], axis=0), v2[i]) for i in range(n)]
        akv = [jnp.where(rowhead, x[:2 * L], 0.0) for x in kv]
        pkv = [x[2 * L:] for x in kv]
        rk = segsum_all(rkr)
        return dict(rows=rows, v=v, cum=cum, beta=beta, kappa=kappa, rho=rho, a2=a2, nmat=nmat, pb=pb,
                    akv=akv, pkv=pkv, rk=rk)

    def stage_inverse(cx):
        nsq = max(L.bit_length() - 2, 0)
        corr = [-x for x in cx["nmat"]]
        pw = [x.astype(BF16) for x in cx["nmat"]]
        pw = [_dot(x, x) for x in pw]
        for it in range(nsq):
            pwb = [x.astype(BF16) for x in pw]
            if it < nsq - 1:
                res = [_dot(jnp.concatenate([pwb[i], corr[i].astype(BF16)], axis=0), pwb[i]) for i in range(n)]
                corr = [corr[i] + pw[i] + res[i][2 * L:] for i in range(n)]
                pw = [x[:2 * L] for x in res]
            else:
                corr = [corr[i] + pw[i] + _dot(corr[i], pwb[i]) for i in range(n)]
        cx["corr"] = corr

    def stage_apply(cx):
        v, beta, kappa = cx["v"], cx["beta"], cx["kappa"]
        both = [jnp.concatenate([cx["a2"][i], cx["akv"][i]], axis=1) for i in range(n)]
        both = [both[i] + _dot(cx["corr"][i], both[i]) for i in range(n)]
        at = [fold_heads(x[:, :P]) for x in both]
        w0_ = [fold_heads(x[:, P:]) for x in both]
        pbx = [_dot(cx["pb"][i], twice(jnp.concatenate([at[i], w0_[i]], axis=1))) for i in range(n)]
        cx["rho_t"] = [cx["rho"][i] - fold_heads(jnp.where(rowhead, pbx[i][:, :P], 0.0)) for i in range(n)]
        cx["o0"] = [fold_heads(jnp.where(rowhead, cx["pkv"][i] - pbx[i][:, P:], 0.0)) for i in range(n)]
        cx["gmat"] = [jnp.where(blockdiag, _dot_tn(at[i], beta[i]), 0.0) for i in range(n)]
        cx["umat"] = [jnp.where(blockdiag, _dot_tn(jnp.concatenate([v[i], -w0_[i]], axis=0),
                                                   jnp.concatenate([kappa[i], beta[i]], axis=0)), 0.0)
                      for i in range(n)]

    def stage_state(cx):
        S = [s_ref[0, p] for p in range(n)]
        Sb = [x.astype(BF16) for x in S]
        cx["o"] = [_dot_nt(cx["rho_t"][i], Sb[i]) + cx["o0"][i] for i in range(n)]
        for i in range(n):
            s_ref[0, i] = ((S[i] - _dot(Sb[i], cx["gmat"][i]) + cx["umat"][i])
                           * jnp.exp2(cx["cum"][i][L - 1:L, :]))

    def stage_mean(cx):
        mu = segsum_all(cx["o"])
        cx["d"] = [cx["o"][i] - mu[i] * (1.0 / N) for i in range(n)]

    def stage_out(cx):
        d, rows = cx["d"], cx["rows"]
        var = segsum_all([x * x for x in d])
        for i in range(n):
            yn = d[i] * lax.rsqrt(var[i] * (1.0 / N) + RWKV_GN_EPS) * par[i][3:4]
            y_ref[0, rows, lanes[i]] = ((yn + cx["rk"][i] * cx["v"][i])
                                        * g_ref[0, rows, lanes[i]]).astype(y_ref.dtype)

    prev = None
    for ci in range(r_ref.shape[1] // L):
        cx = stage_gram(ci)
        if prev is not None:
            stage_mean(prev)
        stage_inverse(cx)
        if prev is not None:
            stage_out(prev)
        stage_apply(cx)
        stage_state(cx)
        prev = cx
    stage_mean(prev)
    stage_out(prev)


def _rwkv_rec(r, k, v, wl, al, g, par, s0_blk, n_valid):
    B, T, D = r.shape
    P = 2 * RWKV_HEAD
    L = RWKV_CHUNK
    assert T % L == 0 and (n_valid == T or T == L)
    npair = D // P
    tb = _pick(T, (RWKV_CHUNKS_PER_STEP * L, L))
    ts = pl.BlockSpec((1, tb, D), lambda b, c: (b, c, 0))
    ss = pl.BlockSpec((1, npair, P, P), lambda b, c: (b, 0, 0, 0))
    return pl.pallas_call(
        functools.partial(_rwkv_rec_body, L=L, npair=npair, n_valid=min(n_valid, L)),
        grid=(B, T // tb),
        in_specs=[ts] * 6 + [pl.BlockSpec((SUBLANE, D), lambda b, c: (0, 0)), ss],
        out_specs=[ts, ss],
        out_shape=[jax.ShapeDtypeStruct((B, T, D), BF16),
                   jax.ShapeDtypeStruct((B, npair, P, P), F32)],
        compiler_params=_cparams(("parallel", "arbitrary")), name="rwkv_rec",
    )(r, k, v, wl, al, g, par, s0_blk)


def _pair_blockdiag(s):
    B, H, N, _ = s.shape
    s = s.reshape(B, H // 2, 2, N, N)
    z = jnp.zeros_like(s[:, :, 0])
    top = jnp.concatenate([s[:, :, 0], z], axis=-1)
    bot = jnp.concatenate([z, s[:, :, 1]], axis=-1)
    return jnp.concatenate([top, bot], axis=-2)


def _pair_unblock(sb):
    B, Pn, P, _ = sb.shape
    N = P // 2
    return jnp.stack([sb[:, :, :N, :N], sb[:, :, N:, N:]], axis=2).reshape(B, 2 * Pn, N, N)


def _pad_time(a, T2):
    if T2 == a.shape[1]:
        return a
    return jnp.pad(a, ((0, 0), (0, T2 - a.shape[1]), (0, 0)))


def _retention_layer(x, mods, pos0, state, w_in, w_out, gn_gain):
    B, T, D = x.shape
    H, dk, dv = state.shape[1:]
    shift_m, scale_m, gate_m = mods
    half = dk // 2
    inv = ROPE_BASE ** (-jnp.arange(half, dtype=F32) / half)
    ang = (pos0 + jnp.arange(T)).astype(F32)[:, None] * inv[None, :]
    proj = _ret_proj(x, shift_m, scale_m, w_in, jnp.cos(ang), jnp.sin(ang), H * dk, dk)
    Tp = T if T % RET_CHUNK == 0 else -(-T // 64) * 64
    assert Tp == T or Tp <= RET_CHUNK
    y, s_new = _retention(_pad_time(proj, Tp), state, gn_gain, H, dk, dv, T)
    x = _mm_res(y[:, :T], w_out, x, gate_m, name="ret_out")
    return x, s_new


def _rwkv_layer(x, mods, shift_state, wkv_state, p):
    B, T, D = x.shape
    shift_m, scale_m, gate_m = mods
    (xr, xk, xv, tw, ta, tg), last = _rwkv_mix(x, shift_m, scale_m, p["mu"], shift_state,
                                               p["w1"], p["a1"], p["g1"])
    zero = jnp.zeros((D,), F32)
    r, ld = _mm2(xr, p["w_r"], tw, p["w2"], p["w0"], "log2_decay", F32, F32, name="rwkv_r_w")
    k, a = _mm2(xk, p["w_k"], ta, p["a2"], p["a0"], "sigmoid", F32, F32, name="rwkv_k_a")
    v, g = _mm2(xv, p["w_v"], tg, p["g2"], zero, None, F32, BF16, name="rwkv_v_g")
    Tp = -(-T // RWKV_CHUNK) * RWKV_CHUNK
    ins = [_pad_time(t, Tp) for t in (r, k, v, ld, a, g)]
    y, s_blk = _rwkv_rec(*ins, p["par"], _pair_blockdiag(wkv_state), T)
    x = _mm_res(y[:, :T], p["w_o"], x, gate_m, name="rwkv_out")
    return x, _pair_unblock(s_blk), last


def _pad_cols(w, n):
    return jnp.pad(w, ((0, 0), (0, n - w.shape[1])))


def _pad_rows(w, n):
    return jnp.pad(w, ((0, n - w.shape[0]), (0, 0)))


def _run_group(x, mod, pos0, st_ret, st_wkv, st_shift, st_conv, w):
    depth = mod.shape[0]
    new_ret, new_wkv, new_shift, new_conv = [], [], [], []
    for i in range(depth):
        m = [mod[i, :, n][:, None, :] for n in range(6)]
        j = i // 2
        if i % 2 == 0:
            x, s = _retention_layer(x, m[:3], pos0, st_ret[j], w["ret_w_in"][j], w["ret_w_out"][j],
                                    w["ret_gn_gain"][j])
            new_ret.append(s)
        else:
            x, s, last = _rwkv_layer(x, m[:3], st_shift[j], st_wkv[j], w["rwkv"][j])
            new_wkv.append(s)
            new_shift.append(last)
        x, cs = _ffn(x, m[3], m[4], m[5], w["ffn_w_gate"][i], w["ffn_w_up"][i], w["ffn_w_down"][i],
                     w["ffn_conv_w"][i], w["ffn_conv_b"][i], st_conv[i])
        new_conv.append(cs)
    out = _norm_gain(x, w["final_gain"])
    return out, jnp.stack(new_ret), jnp.stack(new_wkv), jnp.stack(new_shift), jnp.stack(new_conv)


def kernel(x_prompt, x_sample, c_prompt, c_sample, state_ret, state_rwkv_wkv, state_rwkv_shift, state_ffn_conv, ada_w, ada_b, ret_w_in, ret_w_out, ret_gn_gain, rwkv_mu, rwkv_w_r, rwkv_w_k, rwkv_w_v, rwkv_w_o, rwkv_w0, rwkv_w1, rwkv_w2, rwkv_a0, rwkv_a1, rwkv_a2, rwkv_g1, rwkv_g2, rwkv_k_k, rwkv_k_a, rwkv_r_k, rwkv_gn_gain, ffn_w_gate, ffn_w_up, ffn_conv_w, ffn_conv_b, ffn_w_down, final_gain):
    B, T, D = x_prompt.shape
    Bs = x_sample.shape[0]
    depth = ada_w.shape[0]
    n_rwkv = rwkv_mu.shape[0]

    rows = -(-(B + Bs) // SUBLANE) * SUBLANE
    c_all = jnp.pad(jnp.concatenate([c_prompt, c_sample], axis=0), ((0, rows - B - Bs), (0, 0)))
    mod = _ada(c_all, ada_w, ada_b).reshape(depth, rows, 6, D)

    bf = lambda a: a.astype(BF16)
    rwkv = []
    for j in range(n_rwkv):
        lw = -(-rwkv_w1.shape[2] // LANE) * LANE
        la = -(-rwkv_a1.shape[2] // LANE) * LANE
        par = jnp.pad(jnp.stack([rwkv_k_k[j], rwkv_k_a[j], rwkv_r_k[j].reshape(D), rwkv_gn_gain[j]]),
                      ((0, SUBLANE - 4), (0, 0)))
        rwkv.append(dict(
            mu=rwkv_mu[j], w_r=bf(rwkv_w_r[j]), w_k=bf(rwkv_w_k[j]), w_v=bf(rwkv_w_v[j]),
            w_o=bf(rwkv_w_o[j]),
            w1=bf(_pad_cols(rwkv_w1[j], lw)), w2=bf(_pad_rows(rwkv_w2[j], lw)),
            a1=bf(_pad_cols(rwkv_a1[j], la)), a2=bf(_pad_rows(rwkv_a2[j], la)),
            g1=bf(rwkv_g1[j]), g2=bf(rwkv_g2[j]), w0=rwkv_w0[j], a0=rwkv_a0[j], par=par))
    per_layer = lambda a: [bf(a[i]) for i in range(a.shape[0])]
    w = dict(ret_w_in=per_layer(ret_w_in), ret_w_out=per_layer(ret_w_out), ret_gn_gain=ret_gn_gain,
             rwkv=rwkv, ffn_w_gate=per_layer(ffn_w_gate), ffn_w_up=per_layer(ffn_w_up),
             ffn_w_down=per_layer(ffn_w_down), ffn_conv_w=ffn_conv_w, ffn_conv_b=ffn_conv_b,
             final_gain=final_gain)

    n_ret = state_ret.shape[0]
    F_ = ffn_w_gate.shape[2]
    z_ret = jnp.zeros((n_ret, B) + state_ret.shape[2:], F32)
    z_wkv = jnp.zeros((n_rwkv, B) + state_rwkv_wkv.shape[2:], F32)
    z_shift = jnp.zeros((n_rwkv, B, D), F32)
    z_conv = jnp.zeros((depth, B, CONV_W - 1, F_), F32)
    y_p, p_ret, p_wkv, p_shift, p_conv = _run_group(
        x_prompt, mod[:, :B], 0, z_ret, z_wkv, z_shift, z_conv, w)
    y_s, s_ret, s_wkv, s_shift, s_conv = _run_group(
        x_sample, mod[:, B:B + Bs], PAST_LEN, state_ret, state_rwkv_wkv, state_rwkv_shift,
        state_ffn_conv, w)
    return (y_p, y_s, p_ret, p_wkv, p_shift, p_conv, s_ret, s_wkv, s_shift, s_conv)
```

```python
import functools

import jax
import jax.numpy as jnp
from jax import lax
from jax.experimental import pallas as pl
from jax.experimental.pallas import tpu as pltpu

F32 = jnp.float32
BF16 = jnp.bfloat16

NORM_EPS = 1e-6
RET_GN_EPS = 1e-5
RWKV_GN_EPS = 64e-5
ROPE_BASE = 10000.0
PAST_LEN = 4096
CONV_W = 3

LANE = 128
SUBLANE = 8
VMEM_LIMIT_MB = 56

RET_CHUNK = 256
ROW_CHUNK = 256
RET_TBLOCK = 1024
RWKV_CHUNK = 64
RWKV_HEAD = 64
RWKV_CHUNKS_PER_STEP = 2
DECAY_SCALE_LOG2 = 0.6065306597126334 * 1.4426950408889634


def _pick(n, cands):
    for c in cands:
        if n % c == 0:
            return c
    return n


def _cparams(sem):
    return pltpu.CompilerParams(dimension_semantics=sem, vmem_limit_bytes=VMEM_LIMIT_MB << 20)


def _token_tiles(B, T):
    if T >= 512:
        return 1, _pick(T, (1024, 512, 256, 128))
    return B, T


def _silu(x):
    return x * jax.nn.sigmoid(x)


def _rows8(p):
    return jnp.broadcast_to(p[:, None, :], (p.shape[0], SUBLANE, p.shape[1]))


def _ada_body(c_ref, w_ref, b_ref, o_ref):
    c = c_ref[...]
    s = _silu(c).astype(BF16)
    o_ref[0] = jnp.dot(s, w_ref[0].astype(BF16), preferred_element_type=F32) + b_ref[0]


def _ada(c_all, ada_w, ada_b):
    depth, D, N = ada_w.shape
    R = c_all.shape[0]
    tn = _pick(N, (1024, 512, 256, 128))
    return pl.pallas_call(
        _ada_body,
        grid=(depth, N // tn),
        in_specs=[pl.BlockSpec((R, D), lambda l, j: (0, 0)),
                  pl.BlockSpec((1, D, tn), lambda l, j: (l, 0, j)),
                  pl.BlockSpec((1, 1, tn), lambda l, j: (l, 0, j))],
        out_specs=pl.BlockSpec((1, R, tn), lambda l, j: (l, 0, j)),
        out_shape=jax.ShapeDtypeStruct((depth, R, N), F32),
        compiler_params=_cparams(("parallel", "parallel")),
        name="ada_mod",
    )(c_all, ada_w, ada_b.reshape(depth, 1, N))


def _norm_gain_body(x_ref, g_ref, o_ref):
    x = x_ref[...]
    xn = x * lax.rsqrt(jnp.mean(x * x, -1, keepdims=True) + NORM_EPS)
    o_ref[...] = (xn * g_ref[...]).astype(o_ref.dtype)


def _norm_gain(x, gain):
    B, T, D = x.shape
    bb, tm = _token_tiles(B, T)
    xs = pl.BlockSpec((bb, tm, D), lambda b, t: (b, t, 0))
    return pl.pallas_call(
        _norm_gain_body, grid=(B // bb, T // tm),
        in_specs=[xs, pl.BlockSpec((1, 1, D), lambda b, t: (0, 0, 0))], out_specs=xs,
        out_shape=jax.ShapeDtypeStruct((B, T, D), F32),
        compiler_params=_cparams(("parallel", "parallel")), name="final_norm",
    )(x, gain.reshape(1, 1, D))


def _mm2_body(x_ref, w_ref, x2_ref, w2_ref, b2_ref, o_ref, o2_ref, *, post):
    bb, tm, K = x_ref.shape
    K2 = x2_ref.shape[2]
    rc = _pick(tm, (ROW_CHUNK,))
    for r0 in range(0, tm, rc):
        acc2 = jnp.dot(x2_ref[:, r0:r0 + rc, :].reshape(bb * rc, K2), w2_ref[...], preferred_element_type=F32)
        acc2 = acc2.reshape(bb, rc // SUBLANE, SUBLANE, -1)
        if post == "log2_decay":
            acc2 = -DECAY_SCALE_LOG2 * jax.nn.sigmoid(acc2 + b2_ref[...])
        elif post == "sigmoid":
            acc2 = jax.nn.sigmoid(acc2 + b2_ref[...])
        o2_ref[:, r0:r0 + rc, :] = acc2.reshape(bb, rc, -1).astype(o2_ref.dtype)
        acc = jnp.dot(x_ref[:, r0:r0 + rc, :].reshape(bb * rc, K), w_ref[...], preferred_element_type=F32)
        o_ref[:, r0:r0 + rc, :] = acc.reshape(bb, rc, -1).astype(o_ref.dtype)


def _mm2(x, w, x2, w2, bias2, post, out_dtype, out2_dtype, name="mm2"):
    B, T, K = x.shape
    K2 = x2.shape[2]
    N = w.shape[1]
    assert w2.shape[1] == N
    bb, tm = _token_tiles(B, T)
    tn = _pick(N, (1024, 512, 256, 128))
    os_ = pl.BlockSpec((bb, tm, tn), lambda b, t, j: (b, t, j))
    return pl.pallas_call(
        functools.partial(_mm2_body, post=post),
        grid=(B // bb, T // tm, N // tn),
        in_specs=[pl.BlockSpec((bb, tm, K), lambda b, t, j: (b, t, 0)),
                  pl.BlockSpec((K, tn), lambda b, t, j: (0, j)),
                  pl.BlockSpec((bb, tm, K2), lambda b, t, j: (b, t, 0)),
                  pl.BlockSpec((K2, tn), lambda b, t, j: (0, j)),
                  pl.BlockSpec((SUBLANE, tn), lambda b, t, j: (0, j))],
        out_specs=[os_, os_],
        out_shape=[jax.ShapeDtypeStruct((B, T, N), out_dtype), jax.ShapeDtypeStruct((B, T, N), out2_dtype)],
        compiler_params=_cparams(("parallel", "parallel", "parallel")), name=name,
    )(x, w, x2, w2, _rows8(bias2.reshape(1, N))[0])


def _mm_res_body(y_ref, w_ref, x_ref, gate_ref, o_ref):
    bb, tm, K = y_ref.shape
    acc = jnp.dot(y_ref[...].reshape(bb * tm, K), w_ref[...], preferred_element_type=F32)
    o_ref[...] = x_ref[...] + gate_ref[...] * acc.reshape(bb, tm, -1)


def _mm_res(y, w, x, gate, name="mm_res"):
    B, T, K = y.shape
    N = w.shape[1]
    bb, tm = _token_tiles(B, T)
    tn = _pick(N, (1024, 512, 256, 128) if K <= 2048 else (512, 256, 128))
    return pl.pallas_call(
        _mm_res_body,
        grid=(B // bb, T // tm, N // tn),
        in_specs=[pl.BlockSpec((bb, tm, K), lambda b, t, j: (b, t, 0)),
                  pl.BlockSpec((K, tn), lambda b, t, j: (0, j)),
                  pl.BlockSpec((bb, tm, tn), lambda b, t, j: (b, t, j)),
                  pl.BlockSpec((bb, 1, tn), lambda b, t, j: (b, 0, j))],
        out_specs=pl.BlockSpec((bb, tm, tn), lambda b, t, j: (b, t, j)),
        out_shape=jax.ShapeDtypeStruct((B, T, N), F32),
        compiler_params=_cparams(("parallel", "parallel", "parallel")), name=name,
    )(y, w, x, gate)


def _norm_mod_rows(x, shift, scale):
    xn = x * lax.rsqrt(jnp.mean(x * x, -1, keepdims=True) + NORM_EPS)
    return xn * (1.0 + scale) + shift


def _ret_proj_body(x_ref, sh_ref, sc_ref, w_ref, cos_ref, sin_ref, gain_ref, o_ref, h_ref,
                   *, n_rot, n_q, n_gate, dk):
    j = pl.program_id(2)
    bb, tm, K = x_ref.shape
    tn = w_ref.shape[1]

    def rotary_tile(first):
        sc = jnp.where(j >= n_q, dk ** -0.5, 1.0).astype(F32)
        half = dk // 2
        rc = _pick(tm, (ROW_CHUNK,))
        for r0 in range(0, tm, rc):
            if first:
                h = _norm_mod_rows(x_ref[:, r0:r0 + rc, :], sh_ref[...], sc_ref[...]).astype(BF16)
                h_ref[:, r0:r0 + rc, :] = h
            else:
                h = h_ref[:, r0:r0 + rc, :]
            acc = jnp.dot(h.reshape(bb * rc, K), w_ref[...],
                          preferred_element_type=F32).reshape(bb, rc, tn)
            cos = cos_ref[r0:r0 + rc, :][None] * sc
            sin = sin_ref[r0:r0 + rc, :][None] * sc
            for hh in range(tn // dk):
                lo = hh * dk
                x1 = acc[:, :, lo:lo + half]
                x2 = acc[:, :, lo + half:lo + dk]
                o_ref[:, r0:r0 + rc, lo:lo + half] = (x1 * cos - x2 * sin).astype(o_ref.dtype)
                o_ref[:, r0:r0 + rc, lo + half:lo + dk] = (x1 * sin + x2 * cos).astype(o_ref.dtype)

    pl.when(j == 0)(functools.partial(rotary_tile, True))
    pl.when((j > 0) & (j < n_rot))(functools.partial(rotary_tile, False))

    @pl.when((j >= n_rot) & (j < n_gate))
    def _():
        acc = jnp.dot(h_ref[...].reshape(bb * tm, K), w_ref[...], preferred_element_type=F32)
        o_ref[...] = acc.reshape(bb, tm, tn).astype(o_ref.dtype)

    @pl.when(j >= n_gate)
    def _():
        rc = _pick(tm, (ROW_CHUNK,))
        for r0 in range(0, tm, rc):
            acc = jnp.dot(h_ref[:, r0:r0 + rc, :].reshape(bb * rc, K), w_ref[...], preferred_element_type=F32)
            gated = _silu(acc).reshape(bb, rc // SUBLANE, SUBLANE, tn) * gain_ref[...]
            o_ref[:, r0:r0 + rc, :] = gated.reshape(bb, rc, tn).astype(o_ref.dtype)


def _ret_proj(x, shift, scale, w_in, cos, sin, gn_gain, qk_dim, dk):
    B, T, K = x.shape
    N = w_in.shape[1]
    bb, tm = _token_tiles(B, T)
    tn = _pick(qk_dim, (1024, 512, 256))
    half = dk // 2
    n_gate = (N - gn_gain.shape[0]) // tn
    assert gn_gain.shape[0] % tn == 0
    ms = pl.BlockSpec((bb, 1, K), lambda b, t, j: (b, 0, 0))
    return pl.pallas_call(
        functools.partial(_ret_proj_body, n_rot=2 * qk_dim // tn, n_q=qk_dim // tn, n_gate=n_gate, dk=dk),
        grid=(B // bb, T // tm, N // tn),
        in_specs=[pl.BlockSpec((bb, tm, K), lambda b, t, j: (b, t, 0)), ms, ms,
                  pl.BlockSpec((K, tn), lambda b, t, j: (0, j)),
                  pl.BlockSpec((tm, half), lambda b, t, j: (t, 0)),
                  pl.BlockSpec((tm, half), lambda b, t, j: (t, 0)),
                  pl.BlockSpec((SUBLANE, tn), lambda b, t, j: (0, jnp.maximum(j - n_gate, 0)))],
        out_specs=pl.BlockSpec((bb, tm, tn), lambda b, t, j: (b, t, j)),
        out_shape=jax.ShapeDtypeStruct((B, T, N), BF16),
        scratch_shapes=[pltpu.VMEM((bb, tm, K), BF16)],
        compiler_params=_cparams(("parallel", "parallel", "arbitrary")), name="ret_proj",
    )(x, shift, scale, w_in, cos, sin, _rows8(gn_gain.reshape(1, -1))[0])


def _ret_body(q_ref, k_ref, v_ref, g_ref, intra_ref, cross_ref, into_ref, cd_ref, s0_ref,
              y_ref, s_ref):
    c = pl.program_id(2)
    L = intra_ref.shape[1]

    @pl.when(c == 0)
    def _():
        s_ref[...] = s0_ref[...]

    chunks = [slice(ci * L, (ci + 1) * L) for ci in range(q_ref.shape[1] // L)]
    q = [q_ref[0, rows, :] for rows in chunks]
    v = [v_ref[0, rows, :] for rows in chunks]
    scores = [(lax.dot_general(q[i], k_ref[0, rows, :], (((1,), (1,)), ((), ())),
                               preferred_element_type=F32) * intra_ref[0]).astype(BF16)
              for i, rows in enumerate(chunks)]
    kv = [lax.dot_general((k_ref[0, rows, :].astype(F32) * into_ref[0]).astype(BF16), v[i],
                          (((0,), (0,)), ((), ())), preferred_element_type=F32)
          for i, rows in enumerate(chunks)]
    pv = [jnp.dot(scores[i], v[i], preferred_element_type=F32) for i in range(len(chunks))]
    for i, rows in enumerate(chunks):
        S = s_ref[0, 0]
        o = pv[i] + jnp.dot(q[i], S.astype(BF16), preferred_element_type=F32) * cross_ref[0]
        s_ref[0, 0] = S * cd_ref[0] + kv[i]
        mu = jnp.mean(o, -1, keepdims=True)
        d = o - mu
        var = jnp.mean(d * d, -1, keepdims=True)
        gate = g_ref[0, rows, :].astype(F32)
        y_ref[0, rows, :] = (d * lax.rsqrt(var + RET_GN_EPS) * gate).astype(y_ref.dtype)


def _retention(proj, state, H, dk, dv, n_valid):
    B, T, _ = proj.shape
    L = min(RET_CHUNK, T)
    nv = min(n_valid, L)
    log_g = jnp.log1p(-(2.0 ** (-5.0 - jnp.arange(H, dtype=F32))))
    idx = jnp.arange(L, dtype=F32)
    diff = idx[:, None] - idx[None, :]
    intra = jnp.where(diff >= 0, jnp.exp(log_g[:, None, None] * jnp.maximum(diff, 0.0)), 0.0)
    cross = jnp.exp(log_g[:, None] * (idx[None, :] + 1.0))[:, :, None]
    into = jnp.exp(log_g[:, None] * (nv - 1.0 - idx[None, :]))[:, :, None]
    cd = jnp.exp(log_g * nv)[:, None, None]
    tb = _pick(T, (RET_TBLOCK, L))
    kq = (H * dk) // dk
    vq = (2 * H * dk) // dv
    return pl.pallas_call(
        _ret_body,
        grid=(B, H, T // tb),
        in_specs=[pl.BlockSpec((1, tb, dk), lambda b, h, c: (b, c, h)),
                  pl.BlockSpec((1, tb, dk), lambda b, h, c: (b, c, kq + h)),
                  pl.BlockSpec((1, tb, dv), lambda b, h, c: (b, c, vq + h)),
                  pl.BlockSpec((1, tb, dv), lambda b, h, c: (b, c, vq + H + h)),
                  pl.BlockSpec((1, L, L), lambda b, h, c: (h, 0, 0)),
                  pl.BlockSpec((1, L, 1), lambda b, h, c: (h, 0, 0)),
                  pl.BlockSpec((1, L, 1), lambda b, h, c: (h, 0, 0)),
                  pl.BlockSpec((1, 1, 1), lambda b, h, c: (h, 0, 0)),
                  pl.BlockSpec((1, 1, dk, dv), lambda b, h, c: (b, h, 0, 0))],
        out_specs=[pl.BlockSpec((1, tb, dv), lambda b, h, c: (b, c, h)),
                   pl.BlockSpec((1, 1, dk, dv), lambda b, h, c: (b, h, 0, 0))],
        out_shape=[jax.ShapeDtypeStruct((B, T, H * dv), BF16),
                   jax.ShapeDtypeStruct((B, H, dk, dv), F32)],
        compiler_params=_cparams(("parallel", "parallel", "arbitrary")), name="retention",
    )(proj, proj, proj, proj, intra, cross, into, cd, state)


def _ffn_up_body(x_ref, sh_ref, sc_ref, wg_ref, wu_ref, cw_ref, cb_ref, cs_ref, o_ref, tail_ref,
                 ext_ref, h_ref):
    t = pl.program_id(1)
    f = pl.program_id(2)
    bb, tm, D = x_ref.shape
    tf = wg_ref.shape[1]
    pad = SUBLANE

    @pl.when(t == 0)
    def _():
        tail_ref[f] = cs_ref[...]

    ext_ref[:, pad - 2:pad, :] = tail_ref[f]

    def tiles(a):
        return a.reshape(bb, a.shape[1] // SUBLANE, SUBLANE, tf)

    def column_tile(first):
        cw = cw_ref[...]
        rc = _pick(tm, (ROW_CHUNK,))
        for r0 in range(0, tm, rc):
            if first:
                h = _norm_mod_rows(x_ref[:, r0:r0 + rc, :], sh_ref[...], sc_ref[...]).astype(BF16)
                h_ref[:, r0:r0 + rc, :] = h
            else:
                h = h_ref[:, r0:r0 + rc, :]
            h = h.reshape(bb * rc, D)
            u = jnp.dot(h, wg_ref[...], preferred_element_type=F32).reshape(bb, rc, tf)
            up = jnp.dot(h, wu_ref[...], preferred_element_type=F32).reshape(bb, rc, tf)
            ext_ref[:, pad + r0:pad + r0 + rc, :] = u
            conv = (cb_ref[...] + tiles(ext_ref[:, pad - 2 + r0:pad - 2 + r0 + rc, :]) * cw[0]
                    + tiles(ext_ref[:, pad - 1 + r0:pad - 1 + r0 + rc, :]) * cw[1] + tiles(u) * cw[2])
            o_ref[:, r0:r0 + rc, :] = (_silu(conv) * tiles(up)).reshape(bb, rc, tf).astype(o_ref.dtype)

    pl.when(f == 0)(functools.partial(column_tile, True))
    pl.when(f > 0)(functools.partial(column_tile, False))
    tail_ref[f] = ext_ref[:, pad + tm - 2:pad + tm, :]


def _ffn(x, shift, scale, gate, wg, wu, wd, conv_w, conv_b, conv_state):
    B, T, D = x.shape
    Fd = wg.shape[1]
    bb, tm = _token_tiles(B, T)
    tf = _pick(Fd, (512, 256, 128))
    nf = Fd // tf
    ms = pl.BlockSpec((bb, 1, D), lambda b, t, f: (b, 0, 0))
    act, tails = pl.pallas_call(
        _ffn_up_body,
        grid=(B // bb, T // tm, nf),
        in_specs=[pl.BlockSpec((bb, tm, D), lambda b, t, f: (b, t, 0)), ms, ms,
                  pl.BlockSpec((D, tf), lambda b, t, f: (0, f)),
                  pl.BlockSpec((D, tf), lambda b, t, f: (0, f)),
                  pl.BlockSpec((CONV_W, SUBLANE, tf), lambda b, t, f: (0, 0, f)),
                  pl.BlockSpec((SUBLANE, tf), lambda b, t, f: (0, f)),
                  pl.BlockSpec((bb, CONV_W - 1, tf), lambda b, t, f: (b, 0, f))],
        out_specs=[pl.BlockSpec((bb, tm, tf), lambda b, t, f: (b, t, f)),
                   pl.BlockSpec((nf, bb, CONV_W - 1, tf), lambda b, t, f: (0, b, 0, 0))],
        out_shape=[jax.ShapeDtypeStruct((B, T, Fd), BF16),
                   jax.ShapeDtypeStruct((nf, B, CONV_W - 1, tf), F32)],
        scratch_shapes=[pltpu.VMEM((bb, tm + SUBLANE, tf), F32), pltpu.VMEM((bb, tm, D), BF16)],
        compiler_params=_cparams(("parallel", "arbitrary", "arbitrary")), name="ffn_up",
    )(x, shift, scale, wg, wu, _rows8(conv_w), _rows8(conv_b.reshape(1, Fd))[0], conv_state)
    y = _mm_res(act, wd, x, gate, name="ffn_down")
    return y, tails.transpose(1, 2, 0, 3).reshape(B, CONV_W - 1, Fd)


def _rwkv_mix_body(x_ref, sh_ref, sc_ref, mu_ref, st_ref, w1_ref, a1_ref, g1_ref,
                   xr_ref, xk_ref, xv_ref, tw_ref, ta_ref, tg_ref, last_ref, hbuf_ref, carry_ref):
    t = pl.program_id(1)
    tm = x_ref.shape[1]
    pad = SUBLANE
    x = x_ref[0]
    xn = x * lax.rsqrt(jnp.mean(x * x, -1, keepdims=True) + NORM_EPS)
    h = xn * (1.0 + sc_ref[0]) + sh_ref[0]

    @pl.when(t == 0)
    def _():
        carry_ref[...] = st_ref[0]

    hbuf_ref[pad - 1:pad, :] = carry_ref[...]
    hbuf_ref[pad:, :] = h
    xx = hbuf_ref[pad - 1:pad - 1 + tm, :] - h
    D = h.shape[1]
    h3 = h.reshape(tm // SUBLANE, SUBLANE, D)
    xx3 = xx.reshape(tm // SUBLANE, SUBLANE, D)

    def mix(n):
        return (h3 + xx3 * mu_ref[n][None]).reshape(tm, D).astype(BF16)

    xr_ref[0] = mix(0)
    xk_ref[0] = mix(2)
    xv_ref[0] = mix(3)
    tw_ref[0] = jnp.tanh(jnp.dot(mix(1), w1_ref[...], preferred_element_type=F32)).astype(tw_ref.dtype)
    ta_ref[0] = jnp.dot(mix(4), a1_ref[...], preferred_element_type=F32).astype(ta_ref.dtype)
    tg_ref[0] = jax.nn.sigmoid(jnp.dot(mix(5), g1_ref[...], preferred_element_type=F32)).astype(tg_ref.dtype)
    last = hbuf_ref[pad + tm - 1:pad + tm, :]
    carry_ref[...] = last
    last_ref[0] = last


def _rwkv_mix(x, shift, scale, mu, shift_state, w1, a1, g1):
    B, T, D = x.shape
    tm = _pick(T, (512, 256, 128))
    xs = pl.BlockSpec((1, tm, D), lambda b, t: (b, t, 0))
    ms = pl.BlockSpec((1, 1, D), lambda b, t: (b, 0, 0))
    loras = (w1, a1, g1)
    outs = pl.pallas_call(
        _rwkv_mix_body, grid=(B, T // tm),
        in_specs=[xs, ms, ms, pl.BlockSpec((6, SUBLANE, D), lambda b, t: (0, 0, 0)), ms]
        + [pl.BlockSpec(w.shape, lambda b, t: (0, 0)) for w in loras],
        out_specs=[xs] * 3 + [pl.BlockSpec((1, tm, w.shape[1]), lambda b, t: (b, t, 0)) for w in loras] + [ms],
        out_shape=[jax.ShapeDtypeStruct((B, T, D), BF16)] * 3
        + [jax.ShapeDtypeStruct((B, T, w.shape[1]), BF16) for w in loras]
        + [jax.ShapeDtypeStruct((B, 1, D), F32)],
        scratch_shapes=[pltpu.VMEM((tm + SUBLANE, D), F32), pltpu.VMEM((1, D), F32)],
        compiler_params=_cparams(("parallel", "arbitrary")), name="rwkv_mix",
    )(x, shift, scale, _rows8(mu), shift_state.reshape(B, 1, D), *loras)
    return outs[:6], outs[6].reshape(B, D)


def _dot(a, b):
    return jnp.dot(a.astype(BF16), b.astype(BF16), preferred_element_type=F32)


def _dot_nt(a, b):
    return lax.dot_general(a.astype(BF16), b.astype(BF16), (((1,), (1,)), ((), ())),
                           preferred_element_type=F32)


def _dot_tn(a, b):
    return lax.dot_general(a.astype(BF16), b.astype(BF16), (((0,), (0,)), ((), ())),
                           preferred_element_type=F32)


def _rwkv_rec_body(r_ref, k_ref, v_ref, ld_ref, a_ref, g_ref, par_ref, s0_ref, y_ref, s_ref,
                   *, L, npair, n_valid):
    c = pl.program_id(1)
    N = RWKV_HEAD
    P = 2 * N

    lane_l = lax.broadcasted_iota(jnp.int32, (L, P), 1)
    head0_l = lane_l < N
    r2 = lax.broadcasted_iota(jnp.int32, (2 * L, 2 * L), 0)
    c2 = lax.broadcasted_iota(jnp.int32, (2 * L, 2 * L), 1)
    same = (r2 >= L) == (c2 >= L)
    strict = same & (c2 < r2)
    incl = same & (c2 <= r2)
    rowhead = (lax.broadcasted_iota(jnp.int32, (2 * L, P), 0) >= L) == \
              (lax.broadcasted_iota(jnp.int32, (2 * L, P), 1) >= N)
    ip = lax.broadcasted_iota(jnp.int32, (P, P), 0)
    jp = lax.broadcasted_iota(jnp.int32, (P, P), 1)
    blockdiag = (ip >= N) == (jp >= N)
    seg = blockdiag.astype(BF16)
    live = lax.broadcasted_iota(jnp.int32, (L, P), 0) < n_valid

    def segsum(x):
        return jnp.dot(x.astype(BF16), seg, preferred_element_type=F32)

    row_l = lax.broadcasted_iota(jnp.int32, (L, P), 0)

    def cumsum_rows(x):
        s = 1
        while s < L:
            if s < SUBLANE:
                x = x + jnp.where(row_l >= s, pltpu.roll(x, s, 0), 0.0)
            else:
                x = x + jnp.concatenate([jnp.zeros((s, P), F32), x[:L - s]], axis=0)
            s *= 2
        return x

    def stack_heads(x):
        return jnp.concatenate([jnp.where(head0_l, x, 0.0), jnp.where(head0_l, 0.0, x)], axis=0)

    def fold_heads(x2):
        return x2[:L] + x2[L:]

    def twice(x):
        return jnp.concatenate([x, x], axis=0)

    @pl.when(c == 0)
    def _():
        s_ref[...] = s0_ref[...]

    def segsum_all(xs, per=4):
        outs = []
        for j in range(0, len(xs), per):
            part = xs[j:j + per]
            out = segsum(jnp.concatenate(part, axis=0))
            outs += [out[i * L:(i + 1) * L] for i in range(len(part))]
        return outs

    n = npair
    lanes = [slice(p * P, (p + 1) * P) for p in range(n)]
    par = [par_ref[:, ln] for ln in lanes]

    def stage_gram(ci):
        rows = slice(ci * L, (ci + 1) * L)
        r = [r_ref[0, rows, ln] for ln in lanes]
        k = [k_ref[0, rows, ln] for ln in lanes]
        v = [v_ref[0, rows, ln] for ln in lanes]
        lnd = [ld_ref[0, rows, ln] for ln in lanes]
        a = [a_ref[0, rows, ln] for ln in lanes]
        kkraw, kmod = [], []
        for i in range(n):
            k_k, k_a = par[i][0:1], par[i][1:2]
            kkraw.append(k[i] * k_k)
            kmod.append(k[i] * ((1.0 - k_a) + a[i] * k_a))
        sums = segsum_all([x * x for x in kkraw])
        kk = [kkraw[i] * lax.rsqrt(jnp.maximum(sums[i], 1e-24)) for i in range(n)]
        rkr = [r[i] * kmod[i] * par[i][2:3] for i in range(n)]
        if n_valid < L:
            lnd = [jnp.where(live, x, 0.0) for x in lnd]
            kk = [jnp.where(live, x, 0.0) for x in kk]
            kmod = [jnp.where(live, x, 0.0) for x in kmod]
            v = [jnp.where(live, x, 0.0) for x in v]
        cum = [cumsum_rows(x) for x in lnd]
        beta, kappa, rho, a2, lhs = [], [], [], [], []
        for i in range(n):
            e_neg = jnp.exp2(-cum[i])
            beta.append((kk[i] * a[i] * e_neg).astype(BF16))
            kappa.append((kmod[i] * e_neg).astype(BF16))
            rho.append(r[i] * jnp.exp2(cum[i]))
            a2.append(stack_heads(kk[i] * jnp.exp2(cum[i] - lnd[i])))
            lhs.append(jnp.concatenate([a2[i], stack_heads(rho[i])], axis=0).astype(BF16))
        gb = [_dot_nt(lhs[i], twice(beta[i])) for i in range(n)]
        gk = [_dot_nt(lhs[i], twice(kappa[i])) for i in range(n)]
        nmat = [jnp.where(strict, x[:2 * L], 0.0) for x in gb]
        pb = [jnp.where(incl, x[2 * L:], 0.0).astype(BF16) for x in gb]
        v2 = [twice(x).astype(BF16) for x in v]
        kv = [_dot(jnp.concatenate([jnp.where(strict, gk[i][:2 * L], 0.0),
                                    jnp.where(incl, gk[i][2 * L:], 0.0)], axis=0), v2[i]) for i in range(n)]
        akv = [jnp.where(rowhead, x[:2 * L], 0.0) for x in kv]
        pkv = [x[2 * L:] for x in kv]
        rk = segsum_all(rkr)
        return dict(rows=rows, v=v, cum=cum, beta=beta, kappa=kappa, rho=rho, a2=a2, nmat=nmat, pb=pb,
                    akv=akv, pkv=pkv, rk=rk)

    def stage_inverse(cx):
        nsq = max(L.bit_length() - 2, 0)
        corr = [-x for x in cx["nmat"]]
        pw = [x.astype(BF16) for x in cx["nmat"]]
        pw = [_dot(x, x) for x in pw]
        for it in range(nsq):
            pwb = [x.astype(BF16) for x in pw]
            if it < nsq - 1:
                res = [_dot(jnp.concatenate([pwb[i], corr[i].astype(BF16)], axis=0), pwb[i]) for i in range(n)]
                corr = [corr[i] + pw[i] + res[i][2 * L:] for i in range(n)]
                pw = [x[:2 * L] for x in res]
            else:
                corr = [corr[i] + pw[i] + _dot(corr[i], pwb[i]) for i in range(n)]
        cx["corr"] = corr

    def stage_apply(cx):
        v, beta, kappa = cx["v"], cx["beta"], cx["kappa"]
        both = [jnp.concatenate([cx["a2"][i], cx["akv"][i]], axis=1) for i in range(n)]
        both = [both[i] + _dot(cx["corr"][i], both[i]) for i in range(n)]
        at = [fold_heads(x[:, :P]) for x in both]
        w0_ = [fold_heads(x[:, P:]) for x in both]
        pbx = [_dot(cx["pb"][i], twice(jnp.concatenate([at[i], w0_[i]], axis=1))) for i in range(n)]
        cx["rho_t"] = [cx["rho"][i] - fold_heads(jnp.where(rowhead, pbx[i][:, :P], 0.0)) for i in range(n)]
        cx["o0"] = [fold_heads(jnp.where(rowhead, cx["pkv"][i] - pbx[i][:, P:], 0.0)) for i in range(n)]
        cx["gmat"] = [jnp.where(blockdiag, _dot_tn(at[i], beta[i]), 0.0) for i in range(n)]
        cx["umat"] = [jnp.where(blockdiag, _dot_tn(jnp.concatenate([v[i], -w0_[i]], axis=0),
                                                   jnp.concatenate([kappa[i], beta[i]], axis=0)), 0.0)
                      for i in range(n)]

    def stage_state(cx):
        S = [s_ref[0, p] for p in range(n)]
        Sb = [x.astype(BF16) for x in S]
        cx["o"] = [_dot_nt(cx["rho_t"][i], Sb[i]) + cx["o0"][i] for i in range(n)]
        for i in range(n):
            s_ref[0, i] = ((S[i] - _dot(Sb[i], cx["gmat"][i]) + cx["umat"][i])
                           * jnp.exp2(cx["cum"][i][L - 1:L, :]))

    def stage_mean(cx):
        mu = segsum_all(cx["o"])
        cx["d"] = [cx["o"][i] - mu[i] * (1.0 / N) for i in range(n)]

    def stage_out(cx):
        d, rows = cx["d"], cx["rows"]
        var = segsum_all([x * x for x in d])
        for i in range(n):
            yn = d[i] * lax.rsqrt(var[i] * (1.0 / N) + RWKV_GN_EPS) * par[i][3:4]
            y_ref[0, rows, lanes[i]] = ((yn + cx["rk"][i] * cx["v"][i])
                                        * g_ref[0, rows, lanes[i]]).astype(y_ref.dtype)

    prev = None
    for ci in range(r_ref.shape[1] // L):
        cx = stage_gram(ci)
        if prev is not None:
            stage_mean(prev)
        stage_inverse(cx)
        if prev is not None:
            stage_out(prev)
        stage_apply(cx)
        stage_state(cx)
        prev = cx
    stage_mean(prev)
    stage_out(prev)


def _rwkv_rec(r, k, v, wl, al, g, par, s0_blk, n_valid):
    B, T, D = r.shape
    P = 2 * RWKV_HEAD
    L = RWKV_CHUNK
    assert T % L == 0 and (n_valid == T or T == L)
    npair = D // P
    tb = _pick(T, (RWKV_CHUNKS_PER_STEP * L, L))
    ts = pl.BlockSpec((1, tb, D), lambda b, c: (b, c, 0))
    ss = pl.BlockSpec((1, npair, P, P), lambda b, c: (b, 0, 0, 0))
    return pl.pallas_call(
        functools.partial(_rwkv_rec_body, L=L, npair=npair, n_valid=min(n_valid, L)),
        grid=(B, T // tb),
        in_specs=[ts] * 6 + [pl.BlockSpec((SUBLANE, D), lambda b, c: (0, 0)), ss],
        out_specs=[ts, ss],
        out_shape=[jax.ShapeDtypeStruct((B, T, D), BF16),
                   jax.ShapeDtypeStruct((B, npair, P, P), F32)],
        compiler_params=_cparams(("parallel", "arbitrary")), name="rwkv_rec",
    )(r, k, v, wl, al, g, par, s0_blk)


def _pair_blockdiag(s):
    B, H, N, _ = s.shape
    s = s.reshape(B, H // 2, 2, N, N)
    z = jnp.zeros_like(s[:, :, 0])
    top = jnp.concatenate([s[:, :, 0], z], axis=-1)
    bot = jnp.concatenate([z, s[:, :, 1]], axis=-1)
    return jnp.concatenate([top, bot], axis=-2)


def _pair_unblock(sb):
    B, Pn, P, _ = sb.shape
    N = P // 2
    return jnp.stack([sb[:, :, :N, :N], sb[:, :, N:, N:]], axis=2).reshape(B, 2 * Pn, N, N)


def _pad_time(a, T2):
    if T2 == a.shape[1]:
        return a
    return jnp.pad(a, ((0, 0), (0, T2 - a.shape[1]), (0, 0)))


def _retention_layer(x, mods, pos0, state, w_in, w_out, gn_gain):
    B, T, D = x.shape
    H, dk, dv = state.shape[1:]
    shift_m, scale_m, gate_m = mods
    half = dk // 2
    inv = ROPE_BASE ** (-jnp.arange(half, dtype=F32) / half)
    ang = (pos0 + jnp.arange(T)).astype(F32)[:, None] * inv[None, :]
    proj = _ret_proj(x, shift_m, scale_m, w_in, jnp.cos(ang), jnp.sin(ang), gn_gain, H * dk, dk)
    Tp = T if T % RET_CHUNK == 0 else -(-T // 64) * 64
    assert Tp == T or Tp <= RET_CHUNK
    y, s_new = _retention(_pad_time(proj, Tp), state, H, dk, dv, T)
    x = _mm_res(y[:, :T], w_out, x, gate_m, name="ret_out")
    return x, s_new


def _rwkv_layer(x, mods, shift_state, wkv_state, p):
    B, T, D = x.shape
    shift_m, scale_m, gate_m = mods
    (xr, xk, xv, tw, ta, tg), last = _rwkv_mix(x, shift_m, scale_m, p["mu"], shift_state,
                                               p["w1"], p["a1"], p["g1"])
    zero = jnp.zeros((D,), F32)
    r, ld = _mm2(xr, p["w_r"], tw, p["w2"], p["w0"], "log2_decay", F32, F32, name="rwkv_r_w")
    k, a = _mm2(xk, p["w_k"], ta, p["a2"], p["a0"], "sigmoid", F32, F32, name="rwkv_k_a")
    v, g = _mm2(xv, p["w_v"], tg, p["g2"], zero, None, F32, BF16, name="rwkv_v_g")
    Tp = -(-T // RWKV_CHUNK) * RWKV_CHUNK
    ins = [_pad_time(t, Tp) for t in (r, k, v, ld, a, g)]
    y, s_blk = _rwkv_rec(*ins, p["par"], _pair_blockdiag(wkv_state), T)
    x = _mm_res(y[:, :T], p["w_o"], x, gate_m, name="rwkv_out")
    return x, _pair_unblock(s_blk), last


def _pad_cols(w, n):
    return jnp.pad(w, ((0, 0), (0, n - w.shape[1])))


def _pad_rows(w, n):
    return jnp.pad(w, ((0, n - w.shape[0]), (0, 0)))


def _run_group(x, mod, pos0, st_ret, st_wkv, st_shift, st_conv, w):
    depth = mod.shape[0]
    new_ret, new_wkv, new_shift, new_conv = [], [], [], []
    for i in range(depth):
        m = [mod[i, :, n][:, None, :] for n in range(6)]
        j = i // 2
        if i % 2 == 0:
            x, s = _retention_layer(x, m[:3], pos0, st_ret[j], w["ret_w_in"][j], w["ret_w_out"][j],
                                    w["ret_gn_gain"][j])
            new_ret.append(s)
        else:
            x, s, last = _rwkv_layer(x, m[:3], st_shift[j], st_wkv[j], w["rwkv"][j])
            new_wkv.append(s)
            new_shift.append(last)
        x, cs = _ffn(x, m[3], m[4], m[5], w["ffn_w_gate"][i], w["ffn_w_up"][i], w["ffn_w_down"][i],
                     w["ffn_conv_w"][i], w["ffn_conv_b"][i], st_conv[i])
        new_conv.append(cs)
    out = _norm_gain(x, w["final_gain"])
    return out, jnp.stack(new_ret), jnp.stack(new_wkv), jnp.stack(new_shift), jnp.stack(new_conv)


def kernel(x_prompt, x_sample, c_prompt, c_sample, state_ret, state_rwkv_wkv, state_rwkv_shift, state_ffn_conv, ada_w, ada_b, ret_w_in, ret_w_out, ret_gn_gain, rwkv_mu, rwkv_w_r, rwkv_w_k, rwkv_w_v, rwkv_w_o, rwkv_w0, rwkv_w1, rwkv_w2, rwkv_a0, rwkv_a1, rwkv_a2, rwkv_g1, rwkv_g2, rwkv_k_k, rwkv_k_a, rwkv_r_k, rwkv_gn_gain, ffn_w_gate, ffn_w_up, ffn_conv_w, ffn_conv_b, ffn_w_down, final_gain):
    B, T, D = x_prompt.shape
    Bs = x_sample.shape[0]
    depth = ada_w.shape[0]
    n_rwkv = rwkv_mu.shape[0]

    rows = -(-(B + Bs) // SUBLANE) * SUBLANE
    c_all = jnp.pad(jnp.concatenate([c_prompt, c_sample], axis=0), ((0, rows - B - Bs), (0, 0)))
    mod = _ada(c_all, ada_w, ada_b).reshape(depth, rows, 6, D)

    bf = lambda a: a.astype(BF16)
    rwkv = []
    for j in range(n_rwkv):
        lw = -(-rwkv_w1.shape[2] // LANE) * LANE
        la = -(-rwkv_a1.shape[2] // LANE) * LANE
        par = jnp.pad(jnp.stack([rwkv_k_k[j], rwkv_k_a[j], rwkv_r_k[j].reshape(D), rwkv_gn_gain[j]]),
                      ((0, SUBLANE - 4), (0, 0)))
        rwkv.append(dict(
            mu=rwkv_mu[j], w_r=bf(rwkv_w_r[j]), w_k=bf(rwkv_w_k[j]), w_v=bf(rwkv_w_v[j]),
            w_o=bf(rwkv_w_o[j]),
            w1=bf(_pad_cols(rwkv_w1[j], lw)), w2=bf(_pad_rows(rwkv_w2[j], lw)),
            a1=bf(_pad_cols(rwkv_a1[j], la)), a2=bf(_pad_rows(rwkv_a2[j], la)),
            g1=bf(rwkv_g1[j]), g2=bf(rwkv_g2[j]), w0=rwkv_w0[j], a0=rwkv_a0[j], par=par))
    per_layer = lambda a: [bf(a[i]) for i in range(a.shape[0])]
    w = dict(ret_w_in=per_layer(ret_w_in), ret_w_out=per_layer(ret_w_out), ret_gn_gain=ret_gn_gain,
             rwkv=rwkv, ffn_w_gate=per_layer(ffn_w_gate), ffn_w_up=per_layer(ffn_w_up),
             ffn_w_down=per_layer(ffn_w_down), ffn_conv_w=ffn_conv_w, ffn_conv_b=ffn_conv_b,
             final_gain=final_gain)

    n_ret = state_ret.shape[0]
    F_ = ffn_w_gate.shape[2]
    z_ret = jnp.zeros((n_ret, B) + state_ret.shape[2:], F32)
    z_wkv = jnp.zeros((n_rwkv, B) + state_rwkv_wkv.shape[2:], F32)
    z_shift = jnp.zeros((n_rwkv, B, D), F32)
    z_conv = jnp.zeros((depth, B, CONV_W - 1, F_), F32)
    y_p, p_ret, p_wkv, p_shift, p_conv = _run_group(
        x_prompt, mod[:, :B], 0, z_ret, z_wkv, z_shift, z_conv, w)
    y_s, s_ret, s_wkv, s_shift, s_conv = _run_group(
        x_sample, mod[:, B:B + Bs], PAST_LEN, state_ret, state_rwkv_wkv, state_rwkv_shift,
        state_ffn_conv, w)
    return (y_p, y_s, p_ret, p_wkv, p_shift, p_conv, s_ret, s_wkv, s_shift, s_conv)
```

```python
import functools

import jax
import jax.numpy as jnp
from jax import lax
from jax.experimental import pallas as pl
from jax.experimental.pallas import tpu as pltpu

F32 = jnp.float32
BF16 = jnp.bfloat16

NORM_EPS = 1e-6
RET_GN_EPS = 1e-5
RWKV_GN_EPS = 64e-5
ROPE_BASE = 10000.0
PAST_LEN = 4096
CONV_W = 3

LANE = 128
SUBLANE = 8
VMEM_LIMIT_MB = 56

RET_CHUNK = 256
ROW_CHUNK = 256
RET_TBLOCK = 2048
RWKV_CHUNK = 64
RWKV_HEAD = 64
RWKV_CHUNKS_PER_STEP = 2
DECAY_SCALE_LOG2 = 0.6065306597126334 * 1.4426950408889634


def _pick(n, cands):
    for c in cands:
        if n % c == 0:
            return c
    return n


def _cparams(sem):
    return pltpu.CompilerParams(dimension_semantics=sem, vmem_limit_bytes=VMEM_LIMIT_MB << 20)


def _token_tiles(B, T):
    if T >= 512:
        return 1, _pick(T, (1024, 512, 256, 128))
    return B, T


def _silu(x):
    return x * jax.nn.sigmoid(x)


def _rows8(p):
    return jnp.broadcast_to(p[:, None, :], (p.shape[0], SUBLANE, p.shape[1]))


def _ada_body(c_ref, w_ref, b_ref, o_ref):
    c = c_ref[...]
    s = _silu(c).astype(BF16)
    o_ref[0] = jnp.dot(s, w_ref[0].astype(BF16), preferred_element_type=F32) + b_ref[0]


def _ada(c_all, ada_w, ada_b):
    depth, D, N = ada_w.shape
    R = c_all.shape[0]
    tn = _pick(N, (1024, 512, 256, 128))
    return pl.pallas_call(
        _ada_body,
        grid=(depth, N // tn),
        in_specs=[pl.BlockSpec((R, D), lambda l, j: (0, 0)),
                  pl.BlockSpec((1, D, tn), lambda l, j: (l, 0, j)),
                  pl.BlockSpec((1, 1, tn), lambda l, j: (l, 0, j))],
        out_specs=pl.BlockSpec((1, R, tn), lambda l, j: (l, 0, j)),
        out_shape=jax.ShapeDtypeStruct((depth, R, N), F32),
        compiler_params=_cparams(("parallel", "parallel")),
        name="ada_mod",
    )(c_all, ada_w, ada_b.reshape(depth, 1, N))


def _norm_gain_body(x_ref, g_ref, o_ref):
    x = x_ref[...]
    xn = x * lax.rsqrt(jnp.mean(x * x, -1, keepdims=True) + NORM_EPS)
    o_ref[...] = (xn * g_ref[...]).astype(o_ref.dtype)


def _norm_gain(x, gain):
    B, T, D = x.shape
    bb, tm = _token_tiles(B, T)
    xs = pl.BlockSpec((bb, tm, D), lambda b, t: (b, t, 0))
    return pl.pallas_call(
        _norm_gain_body, grid=(B // bb, T // tm),
        in_specs=[xs, pl.BlockSpec((1, 1, D), lambda b, t: (0, 0, 0))], out_specs=xs,
        out_shape=jax.ShapeDtypeStruct((B, T, D), F32),
        compiler_params=_cparams(("parallel", "parallel")), name="final_norm",
    )(x, gain.reshape(1, 1, D))


def _mm2_body(x_ref, w_ref, x2_ref, w2_ref, b2_ref, o_ref, o2_ref, *, post):
    bb, tm, K = x_ref.shape
    K2 = x2_ref.shape[2]
    rc = _pick(tm, (ROW_CHUNK,))
    for r0 in range(0, tm, rc):
        acc2 = jnp.dot(x2_ref[:, r0:r0 + rc, :].reshape(bb * rc, K2), w2_ref[...], preferred_element_type=F32)
        acc2 = acc2.reshape(bb, rc // SUBLANE, SUBLANE, -1)
        if post == "log2_decay":
            acc2 = -DECAY_SCALE_LOG2 * jax.nn.sigmoid(acc2 + b2_ref[...])
        elif post == "sigmoid":
            acc2 = jax.nn.sigmoid(acc2 + b2_ref[...])
        o2_ref[:, r0:r0 + rc, :] = acc2.reshape(bb, rc, -1).astype(o2_ref.dtype)
        acc = jnp.dot(x_ref[:, r0:r0 + rc, :].reshape(bb * rc, K), w_ref[...], preferred_element_type=F32)
        o_ref[:, r0:r0 + rc, :] = acc.reshape(bb, rc, -1).astype(o_ref.dtype)


def _mm2(x, w, x2, w2, bias2, post, out_dtype, out2_dtype, name="mm2"):
    B, T, K = x.shape
    K2 = x2.shape[2]
    N = w.shape[1]
    assert w2.shape[1] == N
    bb, tm = _token_tiles(B, T)
    tn = _pick(N, (1024, 512, 256, 128))
    os_ = pl.BlockSpec((bb, tm, tn), lambda b, t, j: (b, t, j))
    return pl.pallas_call(
        functools.partial(_mm2_body, post=post),
        grid=(B // bb, T // tm, N // tn),
        in_specs=[pl.BlockSpec((bb, tm, K), lambda b, t, j: (b, t, 0)),
                  pl.BlockSpec((K, tn), lambda b, t, j: (0, j)),
                  pl.BlockSpec((bb, tm, K2), lambda b, t, j: (b, t, 0)),
                  pl.BlockSpec((K2, tn), lambda b, t, j: (0, j)),
                  pl.BlockSpec((SUBLANE, tn), lambda b, t, j: (0, j))],
        out_specs=[os_, os_],
        out_shape=[jax.ShapeDtypeStruct((B, T, N), out_dtype), jax.ShapeDtypeStruct((B, T, N), out2_dtype)],
        compiler_params=_cparams(("parallel", "parallel", "parallel")), name=name,
    )(x, w, x2, w2, _rows8(bias2.reshape(1, N))[0])


def _mm_res_body(y_ref, w_ref, x_ref, gate_ref, o_ref):
    bb, tm, K = y_ref.shape
    acc = jnp.dot(y_ref[...].reshape(bb * tm, K), w_ref[...], preferred_element_type=F32)
    o_ref[...] = x_ref[...] + gate_ref[...] * acc.reshape(bb, tm, -1)


def _mm_res(y, w, x, gate, name="mm_res"):
    B, T, K = y.shape
    N = w.shape[1]
    bb, tm = _token_tiles(B, T)
    tn = _pick(N, (1024, 512, 256, 128) if K <= 2048 else (512, 256, 128))
    return pl.pallas_call(
        _mm_res_body,
        grid=(B // bb, T // tm, N // tn),
        in_specs=[pl.BlockSpec((bb, tm, K), lambda b, t, j: (b, t, 0)),
                  pl.BlockSpec((K, tn), lambda b, t, j: (0, j)),
                  pl.BlockSpec((bb, tm, tn), lambda b, t, j: (b, t, j)),
                  pl.BlockSpec((bb, 1, tn), lambda b, t, j: (b, 0, j))],
        out_specs=pl.BlockSpec((bb, tm, tn), lambda b, t, j: (b, t, j)),
        out_shape=jax.ShapeDtypeStruct((B, T, N), F32),
        compiler_params=_cparams(("parallel", "parallel", "parallel")), name=name,
    )(y, w, x, gate)


def _norm_mod_rows(x, shift, scale):
    xn = x * lax.rsqrt(jnp.mean(x * x, -1, keepdims=True) + NORM_EPS)
    return xn * (1.0 + scale) + shift


def _ret_proj_body(x_ref, sh_ref, sc_ref, w_ref, cos_ref, sin_ref, gain_ref, o_ref, h_ref,
                   *, n_rot, n_q, n_gate, dk):
    j = pl.program_id(2)
    bb, tm, K = x_ref.shape
    tn = w_ref.shape[1]

    def rotary_tile(first):
        sc = jnp.where(j >= n_q, dk ** -0.5, 1.0).astype(F32)
        half = dk // 2
        rc = _pick(tm, (ROW_CHUNK,))
        for r0 in range(0, tm, rc):
            if first:
                h = _norm_mod_rows(x_ref[:, r0:r0 + rc, :], sh_ref[...], sc_ref[...]).astype(BF16)
                h_ref[:, r0:r0 + rc, :] = h
            else:
                h = h_ref[:, r0:r0 + rc, :]
            acc = jnp.dot(h.reshape(bb * rc, K), w_ref[...],
                          preferred_element_type=F32).reshape(bb, rc, tn)
            cos = cos_ref[r0:r0 + rc, :][None] * sc
            sin = sin_ref[r0:r0 + rc, :][None] * sc
            for hh in range(tn // dk):
                lo = hh * dk
                x1 = acc[:, :, lo:lo + half]
                x2 = acc[:, :, lo + half:lo + dk]
                o_ref[:, r0:r0 + rc, lo:lo + half] = (x1 * cos - x2 * sin).astype(o_ref.dtype)
                o_ref[:, r0:r0 + rc, lo + half:lo + dk] = (x1 * sin + x2 * cos).astype(o_ref.dtype)

    pl.when(j == 0)(functools.partial(rotary_tile, True))
    pl.when((j > 0) & (j < n_rot))(functools.partial(rotary_tile, False))

    @pl.when((j >= n_rot) & (j < n_gate))
    def _():
        acc = jnp.dot(h_ref[...].reshape(bb * tm, K), w_ref[...], preferred_element_type=F32)
        o_ref[...] = acc.reshape(bb, tm, tn).astype(o_ref.dtype)

    @pl.when(j >= n_gate)
    def _():
        rc = _pick(tm, (ROW_CHUNK,))
        for r0 in range(0, tm, rc):
            acc = jnp.dot(h_ref[:, r0:r0 + rc, :].reshape(bb * rc, K), w_ref[...], preferred_element_type=F32)
            gated = _silu(acc).reshape(bb, rc // SUBLANE, SUBLANE, tn) * gain_ref[...]
            o_ref[:, r0:r0 + rc, :] = gated.reshape(bb, rc, tn).astype(o_ref.dtype)


def _ret_proj(x, shift, scale, w_in, cos, sin, gn_gain, qk_dim, dk):
    B, T, K = x.shape
    N = w_in.shape[1]
    bb, tm = _token_tiles(B, T)
    tn = _pick(qk_dim, (1024, 512, 256))
    half = dk // 2
    n_gate = (N - gn_gain.shape[0]) // tn
    assert gn_gain.shape[0] % tn == 0
    ms = pl.BlockSpec((bb, 1, K), lambda b, t, j: (b, 0, 0))
    return pl.pallas_call(
        functools.partial(_ret_proj_body, n_rot=2 * qk_dim // tn, n_q=qk_dim // tn, n_gate=n_gate, dk=dk),
        grid=(B // bb, T // tm, N // tn),
        in_specs=[pl.BlockSpec((bb, tm, K), lambda b, t, j: (b, t, 0)), ms, ms,
                  pl.BlockSpec((K, tn), lambda b, t, j: (0, j)),
                  pl.BlockSpec((tm, half), lambda b, t, j: (t, 0)),
                  pl.BlockSpec((tm, half), lambda b, t, j: (t, 0)),
                  pl.BlockSpec((SUBLANE, tn), lambda b, t, j: (0, jnp.maximum(j - n_gate, 0)))],
        out_specs=pl.BlockSpec((bb, tm, tn), lambda b, t, j: (b, t, j)),
        out_shape=jax.ShapeDtypeStruct((B, T, N), BF16),
        scratch_shapes=[pltpu.VMEM((bb, tm, K), BF16)],
        compiler_params=_cparams(("parallel", "parallel", "arbitrary")), name="ret_proj",
    )(x, shift, scale, w_in, cos, sin, _rows8(gn_gain.reshape(1, -1))[0])


def _ret_body(q_ref, k_ref, v_ref, g_ref, intra_ref, cross_ref, into_ref, cd_ref, s0_ref,
              y_ref, s_ref):
    c = pl.program_id(2)
    L = intra_ref.shape[1]

    @pl.when(c == 0)
    def _():
        s_ref[...] = s0_ref[...]

    chunks = [slice(ci * L, (ci + 1) * L) for ci in range(q_ref.shape[1] // L)]
    q = [q_ref[0, rows, :] for rows in chunks]
    v = [v_ref[0, rows, :] for rows in chunks]
    scores = [(lax.dot_general(q[i], k_ref[0, rows, :], (((1,), (1,)), ((), ())),
                               preferred_element_type=F32) * intra_ref[0]).astype(BF16)
              for i, rows in enumerate(chunks)]
    kv = [lax.dot_general((k_ref[0, rows, :].astype(F32) * into_ref[0]).astype(BF16), v[i],
                          (((0,), (0,)), ((), ())), preferred_element_type=F32)
          for i, rows in enumerate(chunks)]
    pv = [jnp.dot(scores[i], v[i], preferred_element_type=F32) for i in range(len(chunks))]
    for i, rows in enumerate(chunks):
        S = s_ref[0, 0]
        o = pv[i] + jnp.dot(q[i], S.astype(BF16), preferred_element_type=F32) * cross_ref[0]
        s_ref[0, 0] = S * cd_ref[0] + kv[i]
        mu = jnp.mean(o, -1, keepdims=True)
        d = o - mu
        var = jnp.mean(d * d, -1, keepdims=True)
        gate = g_ref[0, rows, :].astype(F32)
        y_ref[0, rows, :] = (d * lax.rsqrt(var + RET_GN_EPS) * gate).astype(y_ref.dtype)


def _retention(proj, state, H, dk, dv, n_valid):
    B, T, _ = proj.shape
    L = min(RET_CHUNK, T)
    nv = min(n_valid, L)
    log_g = jnp.log1p(-(2.0 ** (-5.0 - jnp.arange(H, dtype=F32))))
    idx = jnp.arange(L, dtype=F32)
    diff = idx[:, None] - idx[None, :]
    intra = jnp.where(diff >= 0, jnp.exp(log_g[:, None, None] * jnp.maximum(diff, 0.0)), 0.0)
    cross = jnp.exp(log_g[:, None] * (idx[None, :] + 1.0))[:, :, None]
    into = jnp.exp(log_g[:, None] * (nv - 1.0 - idx[None, :]))[:, :, None]
    cd = jnp.exp(log_g * nv)[:, None, None]
    tb = _pick(T, (RET_TBLOCK, L))
    kq = (H * dk) // dk
    vq = (2 * H * dk) // dv
    return pl.pallas_call(
        _ret_body,
        grid=(B, H, T // tb),
        in_specs=[pl.BlockSpec((1, tb, dk), lambda b, h, c: (b, c, h)),
                  pl.BlockSpec((1, tb, dk), lambda b, h, c: (b, c, kq + h)),
                  pl.BlockSpec((1, tb, dv), lambda b, h, c: (b, c, vq + h)),
                  pl.BlockSpec((1, tb, dv), lambda b, h, c: (b, c, vq + H + h)),
                  pl.BlockSpec((1, L, L), lambda b, h, c: (h, 0, 0)),
                  pl.BlockSpec((1, L, 1), lambda b, h, c: (h, 0, 0)),
                  pl.BlockSpec((1, L, 1), lambda b, h, c: (h, 0, 0)),
                  pl.BlockSpec((1, 1, 1), lambda b, h, c: (h, 0, 0)),
                  pl.BlockSpec((1, 1, dk, dv), lambda b, h, c: (b, h, 0, 0))],
        out_specs=[pl.BlockSpec((1, tb, dv), lambda b, h, c: (b, c, h)),
                   pl.BlockSpec((1, 1, dk, dv), lambda b, h, c: (b, h, 0, 0))],
        out_shape=[jax.ShapeDtypeStruct((B, T, H * dv), BF16),
                   jax.ShapeDtypeStruct((B, H, dk, dv), F32)],
        compiler_params=_cparams(("parallel", "parallel", "arbitrary")), name="retention",
    )(proj, proj, proj, proj, intra, cross, into, cd, state)


def _ffn_up_body(x_ref, sh_ref, sc_ref, wg_ref, wu_ref, cw_ref, cb_ref, cs_ref, o_ref, tail_ref,
                 ext_ref, h_ref):
    t = pl.program_id(1)
    f = pl.program_id(2)
    bb, tm, D = x_ref.shape
    tf = wg_ref.shape[1]
    pad = SUBLANE

    @pl.when(t == 0)
    def _():
        tail_ref[f] = cs_ref[...]

    ext_ref[:, pad - 2:pad, :] = tail_ref[f]

    def tiles(a):
        return a.reshape(bb, a.shape[1] // SUBLANE, SUBLANE, tf)

    def column_tile(first):
        cw = cw_ref[...]
        rc = _pick(tm, (ROW_CHUNK,))
        for r0 in range(0, tm, rc):
            if first:
                h = _norm_mod_rows(x_ref[:, r0:r0 + rc, :], sh_ref[...], sc_ref[...]).astype(BF16)
                h_ref[:, r0:r0 + rc, :] = h
            else:
                h = h_ref[:, r0:r0 + rc, :]
            h = h.reshape(bb * rc, D)
            u = jnp.dot(h, wg_ref[...], preferred_element_type=F32).reshape(bb, rc, tf)
            up = jnp.dot(h, wu_ref[...], preferred_element_type=F32).reshape(bb, rc, tf)
            ext_ref[:, pad + r0:pad + r0 + rc, :] = u
            conv = (cb_ref[...] + tiles(ext_ref[:, pad - 2 + r0:pad - 2 + r0 + rc, :]) * cw[0]
                    + tiles(ext_ref[:, pad - 1 + r0:pad - 1 + r0 + rc, :]) * cw[1] + tiles(u) * cw[2])
            o_ref[:, r0:r0 + rc, :] = (_silu(conv) * tiles(up)).reshape(bb, rc, tf).astype(o_ref.dtype)

    pl.when(f == 0)(functools.partial(column_tile, True))
    pl.when(f > 0)(functools.partial(column_tile, False))
    tail_ref[f] = ext_ref[:, pad + tm - 2:pad + tm, :]


def _ffn(x, shift, scale, gate, wg, wu, wd, conv_w, conv_b, conv_state):
    B, T, D = x.shape
    Fd = wg.shape[1]
    bb, tm = _token_tiles(B, T)
    tf = _pick(Fd, (512, 256, 128))
    nf = Fd // tf
    ms = pl.BlockSpec((bb, 1, D), lambda b, t, f: (b, 0, 0))
    act, tails = pl.pallas_call(
        _ffn_up_body,
        grid=(B // bb, T // tm, nf),
        in_specs=[pl.BlockSpec((bb, tm, D), lambda b, t, f: (b, t, 0)), ms, ms,
                  pl.BlockSpec((D, tf), lambda b, t, f: (0, f)),
                  pl.BlockSpec((D, tf), lambda b, t, f: (0, f)),
                  pl.BlockSpec((CONV_W, SUBLANE, tf), lambda b, t, f: (0, 0, f)),
                  pl.BlockSpec((SUBLANE, tf), lambda b, t, f: (0, f)),
                  pl.BlockSpec((bb, CONV_W - 1, tf), lambda b, t, f: (b, 0, f))],
        out_specs=[pl.BlockSpec((bb, tm, tf), lambda b, t, f: (b, t, f)),
                   pl.BlockSpec((nf, bb, CONV_W - 1, tf), lambda b, t, f: (0, b, 0, 0))],
        out_shape=[jax.ShapeDtypeStruct((B, T, Fd), BF16),
                   jax.ShapeDtypeStruct((nf, B, CONV_W - 1, tf), F32)],
        scratch_shapes=[pltpu.VMEM((bb, tm + SUBLANE, tf), F32), pltpu.VMEM((bb, tm, D), BF16)],
        compiler_params=_cparams(("parallel", "arbitrary", "arbitrary")), name="ffn_up",
    )(x, shift, scale, wg, wu, _rows8(conv_w), _rows8(conv_b.reshape(1, Fd))[0], conv_state)
    y = _mm_res(act, wd, x, gate, name="ffn_down")
    return y, tails.transpose(1, 2, 0, 3).reshape(B, CONV_W - 1, Fd)


def _rwkv_mix_body(x_ref, sh_ref, sc_ref, mu_ref, st_ref, w1_ref, a1_ref, g1_ref,
                   xr_ref, xk_ref, xv_ref, tw_ref, ta_ref, tg_ref, last_ref, hbuf_ref, carry_ref):
    t = pl.program_id(1)
    tm = x_ref.shape[1]
    pad = SUBLANE
    x = x_ref[0]
    xn = x * lax.rsqrt(jnp.mean(x * x, -1, keepdims=True) + NORM_EPS)
    h = xn * (1.0 + sc_ref[0]) + sh_ref[0]

    @pl.when(t == 0)
    def _():
        carry_ref[...] = st_ref[0]

    hbuf_ref[pad - 1:pad, :] = carry_ref[...]
    hbuf_ref[pad:, :] = h
    xx = hbuf_ref[pad - 1:pad - 1 + tm, :] - h
    D = h.shape[1]
    h3 = h.reshape(tm // SUBLANE, SUBLANE, D)
    xx3 = xx.reshape(tm // SUBLANE, SUBLANE, D)

    def mix(n):
        return (h3 + xx3 * mu_ref[n][None]).reshape(tm, D).astype(BF16)

    xr_ref[0] = mix(0)
    xk_ref[0] = mix(2)
    xv_ref[0] = mix(3)
    tw_ref[0] = jnp.tanh(jnp.dot(mix(1), w1_ref[...], preferred_element_type=F32)).astype(tw_ref.dtype)
    ta_ref[0] = jnp.dot(mix(4), a1_ref[...], preferred_element_type=F32).astype(ta_ref.dtype)
    tg_ref[0] = jax.nn.sigmoid(jnp.dot(mix(5), g1_ref[...], preferred_element_type=F32)).astype(tg_ref.dtype)
    last = hbuf_ref[pad + tm - 1:pad + tm, :]
    carry_ref[...] = last
    last_ref[0] = last


def _rwkv_mix(x, shift, scale, mu, shift_state, w1, a1, g1):
    B, T, D = x.shape
    tm = _pick(T, (512, 256, 128))
    xs = pl.BlockSpec((1, tm, D), lambda b, t: (b, t, 0))
    ms = pl.BlockSpec((1, 1, D), lambda b, t: (b, 0, 0))
    loras = (w1, a1, g1)
    outs = pl.pallas_call(
        _rwkv_mix_body, grid=(B, T // tm),
        in_specs=[xs, ms, ms, pl.BlockSpec((6, SUBLANE, D), lambda b, t: (0, 0, 0)), ms]
        + [pl.BlockSpec(w.shape, lambda b, t: (0, 0)) for w in loras],
        out_specs=[xs] * 3 + [pl.BlockSpec((1, tm, w.shape[1]), lambda b, t: (b, t, 0)) for w in loras] + [ms],
        out_shape=[jax.ShapeDtypeStruct((B, T, D), BF16)] * 3
        + [jax.ShapeDtypeStruct((B, T, w.shape[1]), BF16) for w in loras]
        + [jax.ShapeDtypeStruct((B, 1, D), F32)],
        scratch_shapes=[pltpu.VMEM((tm + SUBLANE, D), F32), pltpu.VMEM((1, D), F32)],
        compiler_params=_cparams(("parallel", "arbitrary")), name="rwkv_mix",
    )(x, shift, scale, _rows8(mu), shift_state.reshape(B, 1, D), *loras)
    return outs[:6], outs[6].reshape(B, D)


def _dot(a, b):
    return jnp.dot(a.astype(BF16), b.astype(BF16), preferred_element_type=F32)


def _dot_nt(a, b):
    return lax.dot_general(a.astype(BF16), b.astype(BF16), (((1,), (1,)), ((), ())),
                           preferred_element_type=F32)


def _dot_tn(a, b):
    return lax.dot_general(a.astype(BF16), b.astype(BF16), (((0,), (0,)), ((), ())),
                           preferred_element_type=F32)


def _rwkv_rec_body(r_ref, k_ref, v_ref, ld_ref, a_ref, g_ref, par_ref, s0_ref, y_ref, s_ref,
                   *, L, npair, n_valid):
    c = pl.program_id(1)
    N = RWKV_HEAD
    P = 2 * N

    lane_l = lax.broadcasted_iota(jnp.int32, (L, P), 1)
    head0_l = lane_l < N
    r2 = lax.broadcasted_iota(jnp.int32, (2 * L, 2 * L), 0)
    c2 = lax.broadcasted_iota(jnp.int32, (2 * L, 2 * L), 1)
    same = (r2 >= L) == (c2 >= L)
    strict = same & (c2 < r2)
    incl = same & (c2 <= r2)
    rowhead = (lax.broadcasted_iota(jnp.int32, (2 * L, P), 0) >= L) == \
              (lax.broadcasted_iota(jnp.int32, (2 * L, P), 1) >= N)
    ip = lax.broadcasted_iota(jnp.int32, (P, P), 0)
    jp = lax.broadcasted_iota(jnp.int32, (P, P), 1)
    blockdiag = (ip >= N) == (jp >= N)
    seg = blockdiag.astype(BF16)
    live = lax.broadcasted_iota(jnp.int32, (L, P), 0) < n_valid

    def segsum(x):
        return jnp.dot(x.astype(BF16), seg, preferred_element_type=F32)

    row_l = lax.broadcasted_iota(jnp.int32, (L, P), 0)

    def cumsum_rows(x):
        s = 1
        while s < L:
            if s < SUBLANE:
                x = x + jnp.where(row_l >= s, pltpu.roll(x, s, 0), 0.0)
            else:
                x = x + jnp.concatenate([jnp.zeros((s, P), F32), x[:L - s]], axis=0)
            s *= 2
        return x

    def stack_heads(x):
        return jnp.concatenate([jnp.where(head0_l, x, 0.0), jnp.where(head0_l, 0.0, x)], axis=0)

    def fold_heads(x2):
        return x2[:L] + x2[L:]

    def twice(x):
        return jnp.concatenate([x, x], axis=0)

    @pl.when(c == 0)
    def _():
        s_ref[...] = s0_ref[...]

    def segsum_all(xs, per=4):
        outs = []
        for j in range(0, len(xs), per):
            part = xs[j:j + per]
            out = segsum(jnp.concatenate(part, axis=0))
            outs += [out[i * L:(i + 1) * L] for i in range(len(part))]
        return outs

    n = npair
    lanes = [slice(p * P, (p + 1) * P) for p in range(n)]
    par = [par_ref[:, ln] for ln in lanes]

    def stage_gram(ci):
        rows = slice(ci * L, (ci + 1) * L)
        r = [r_ref[0, rows, ln] for ln in lanes]
        k = [k_ref[0, rows, ln] for ln in lanes]
        v = [v_ref[0, rows, ln] for ln in lanes]
        lnd = [ld_ref[0, rows, ln] for ln in lanes]
        a = [a_ref[0, rows, ln] for ln in lanes]
        kkraw, kmod = [], []
        for i in range(n):
            k_k, k_a = par[i][0:1], par[i][1:2]
            kkraw.append(k[i] * k_k)
            kmod.append(k[i] * ((1.0 - k_a) + a[i] * k_a))
        sums = segsum_all([x * x for x in kkraw])
        kk = [kkraw[i] * lax.rsqrt(jnp.maximum(sums[i], 1e-24)) for i in range(n)]
        rkr = [r[i] * kmod[i] * par[i][2:3] for i in range(n)]
        if n_valid < L:
            lnd = [jnp.where(live, x, 0.0) for x in lnd]
            kk = [jnp.where(live, x, 0.0) for x in kk]
            kmod = [jnp.where(live, x, 0.0) for x in kmod]
            v = [jnp.where(live, x, 0.0) for x in v]
        cum = [cumsum_rows(x) for x in lnd]
        beta, kappa, rho, a2, lhs = [], [], [], [], []
        for i in range(n):
            e_neg = jnp.exp2(-cum[i])
            beta.append((kk[i] * a[i] * e_neg).astype(BF16))
            kappa.append((kmod[i] * e_neg).astype(BF16))
            rho.append(r[i] * jnp.exp2(cum[i]))
            a2.append(stack_heads(kk[i] * jnp.exp2(cum[i] - lnd[i])))
            lhs.append(jnp.concatenate([a2[i], stack_heads(rho[i])], axis=0).astype(BF16))
        gb = [_dot_nt(lhs[i], twice(beta[i])) for i in range(n)]
        gk = [_dot_nt(lhs[i], twice(kappa[i])) for i in range(n)]
        nmat = [jnp.where(strict, x[:2 * L], 0.0) for x in gb]
        pb = [jnp.where(incl, x[2 * L:], 0.0).astype(BF16) for x in gb]
        v2 = [twice(x).astype(BF16) for x in v]
        kv = [_dot(jnp.concatenate([jnp.where(strict, gk[i][:2 * L], 0.0),
                                    jnp.where(incl, gk[i][2 * L:], 0.0)], axis=0), v2[i]) for i in range(n)]
        akv = [jnp.where(rowhead, x[:2 * L], 0.0) for x in kv]
        pkv = [x[2 * L:] for x in kv]
        rk = segsum_all(rkr)
        return dict(rows=rows, v=v, cum=cum, beta=beta, kappa=kappa, rho=rho, a2=a2, nmat=nmat, pb=pb,
                    akv=akv, pkv=pkv, rk=rk)

    def stage_inverse(cx):
        nsq = max(L.bit_length() - 2, 0)
        corr = [-x for x in cx["nmat"]]
        pw = [x.astype(BF16) for x in cx["nmat"]]
        pw = [_dot(x, x) for x in pw]
        for it in range(nsq):
            pwb = [x.astype(BF16) for x in pw]
            if it < nsq - 1:
                res = [_dot(jnp.concatenate([pwb[i], corr[i].astype(BF16)], axis=0), pwb[i]) for i in range(n)]
                corr = [corr[i] + pw[i] + res[i][2 * L:] for i in range(n)]
                pw = [x[:2 * L] for x in res]
            else:
                corr = [corr[i] + pw[i] + _dot(corr[i], pwb[i]) for i in range(n)]
        cx["corr"] = corr

    def stage_apply(cx):
        v, beta, kappa = cx["v"], cx["beta"], cx["kappa"]
        both = [jnp.concatenate([cx["a2"][i], cx["akv"][i]], axis=1) for i in range(n)]
        both = [both[i] + _dot(cx["corr"][i], both[i]) for i in range(n)]
        at = [fold_heads(x[:, :P]) for x in both]
        w0_ = [fold_heads(x[:, P:]) for x in both]
        pbx = [_dot(cx["pb"][i], twice(jnp.concatenate([at[i], w0_[i]], axis=1))) for i in range(n)]
        cx["rho_t"] = [cx["rho"][i] - fold_heads(jnp.where(rowhead, pbx[i][:, :P], 0.0)) for i in range(n)]
        cx["o0"] = [fold_heads(jnp.where(rowhead, cx["pkv"][i] - pbx[i][:, P:], 0.0)) for i in range(n)]
        cx["gmat"] = [jnp.where(blockdiag, _dot_tn(at[i], beta[i]), 0.0) for i in range(n)]
        cx["umat"] = [jnp.where(blockdiag, _dot_tn(jnp.concatenate([v[i], -w0_[i]], axis=0),
                                                   jnp.concatenate([kappa[i], beta[i]], axis=0)), 0.0)
                      for i in range(n)]

    def stage_state(cx):
        S = [s_ref[0, p] for p in range(n)]
        Sb = [x.astype(BF16) for x in S]
        cx["o"] = [_dot_nt(cx["rho_t"][i], Sb[i]) + cx["o0"][i] for i in range(n)]
        for i in range(n):
            s_ref[0, i] = ((S[i] - _dot(Sb[i], cx["gmat"][i]) + cx["umat"][i])
                           * jnp.exp2(cx["cum"][i][L - 1:L, :]))

    def stage_mean(cx):
        mu = segsum_all(cx["o"])
        cx["d"] = [cx["o"][i] - mu[i] * (1.0 / N) for i in range(n)]

    def stage_out(cx):
        d, rows = cx["d"], cx["rows"]
        var = segsum_all([x * x for x in d])
        for i in range(n):
            yn = d[i] * lax.rsqrt(var[i] * (1.0 / N) + RWKV_GN_EPS) * par[i][3:4]
            y_ref[0, rows, lanes[i]] = ((yn + cx["rk"][i] * cx["v"][i])
                                        * g_ref[0, rows, lanes[i]]).astype(y_ref.dtype)

    prev = None
    for ci in range(r_ref.shape[1] // L):
        cx = stage_gram(ci)
        if prev is not None:
            stage_mean(prev)
        stage_inverse(cx)
        if prev is not None:
            stage_out(prev)
        stage_apply(cx)
        stage_state(cx)
        prev = cx
    stage_mean(prev)
    stage_out(prev)


def _rwkv_rec(r, k, v, wl, al, g, par, s0_blk, n_valid):
    B, T, D = r.shape
    P = 2 * RWKV_HEAD
    L = RWKV_CHUNK
    assert T % L == 0 and (n_valid == T or T == L)
    npair = D // P
    tb = _pick(T, (RWKV_CHUNKS_PER_STEP * L, L))
    ts = pl.BlockSpec((1, tb, D), lambda b, c: (b, c, 0))
    ss = pl.BlockSpec((1, npair, P, P), lambda b, c: (b, 0, 0, 0))
    return pl.pallas_call(
        functools.partial(_rwkv_rec_body, L=L, npair=npair, n_valid=min(n_valid, L)),
        grid=(B, T // tb),
        in_specs=[ts] * 6 + [pl.BlockSpec((SUBLANE, D), lambda b, c: (0, 0)), ss],
        out_specs=[ts, ss],
        out_shape=[jax.ShapeDtypeStruct((B, T, D), BF16),
                   jax.ShapeDtypeStruct((B, npair, P, P), F32)],
        compiler_params=_cparams(("parallel", "arbitrary")), name="rwkv_rec",
    )(r, k, v, wl, al, g, par, s0_blk)


def _pair_blockdiag(s):
    B, H, N, _ = s.shape
    s = s.reshape(B, H // 2, 2, N, N)
    z = jnp.zeros_like(s[:, :, 0])
    top = jnp.concatenate([s[:, :, 0], z], axis=-1)
    bot = jnp.concatenate([z, s[:, :, 1]], axis=-1)
    return jnp.concatenate([top, bot], axis=-2)


def _pair_unblock(sb):
    B, Pn, P, _ = sb.shape
    N = P // 2
    return jnp.stack([sb[:, :, :N, :N], sb[:, :, N:, N:]], axis=2).reshape(B, 2 * Pn, N, N)


def _pad_time(a, T2):
    if T2 == a.shape[1]:
        return a
    return jnp.pad(a, ((0, 0), (0, T2 - a.shape[1]), (0, 0)))


def _retention_layer(x, mods, pos0, state, w_in, w_out, gn_gain):
    B, T, D = x.shape
    H, dk, dv = state.shape[1:]
    shift_m, scale_m, gate_m = mods
    half = dk // 2
    inv = ROPE_BASE ** (-jnp.arange(half, dtype=F32) / half)
    ang = (pos0 + jnp.arange(T)).astype(F32)[:, None] * inv[None, :]
    proj = _ret_proj(x, shift_m, scale_m, w_in, jnp.cos(ang), jnp.sin(ang), gn_gain, H * dk, dk)
    Tp = T if T % RET_CHUNK == 0 else -(-T // 64) * 64
    assert Tp == T or Tp <= RET_CHUNK
    y, s_new = _retention(_pad_time(proj, Tp), state, H, dk, dv, T)
    x = _mm_res(y[:, :T], w_out, x, gate_m, name="ret_out")
    return x, s_new


def _rwkv_layer(x, mods, shift_state, wkv_state, p):
    B, T, D = x.shape
    shift_m, scale_m, gate_m = mods
    (xr, xk, xv, tw, ta, tg), last = _rwkv_mix(x, shift_m, scale_m, p["mu"], shift_state,
                                               p["w1"], p["a1"], p["g1"])
    zero = jnp.zeros((D,), F32)
    r, ld = _mm2(xr, p["w_r"], tw, p["w2"], p["w0"], "log2_decay", F32, F32, name="rwkv_r_w")
    k, a = _mm2(xk, p["w_k"], ta, p["a2"], p["a0"], "sigmoid", F32, F32, name="rwkv_k_a")
    v, g = _mm2(xv, p["w_v"], tg, p["g2"], zero, None, F32, BF16, name="rwkv_v_g")
    Tp = -(-T // RWKV_CHUNK) * RWKV_CHUNK
    ins = [_pad_time(t, Tp) for t in (r, k, v, ld, a, g)]
    y, s_blk = _rwkv_rec(*ins, p["par"], _pair_blockdiag(wkv_state), T)
    x = _mm_res(y[:, :T], p["w_o"], x, gate_m, name="rwkv_out")
    return x, _pair_unblock(s_blk), last


def _pad_cols(w, n):
    return jnp.pad(w, ((0, 0), (0, n - w.shape[1])))


def _pad_rows(w, n):
    return jnp.pad(w, ((0, n - w.shape[0]), (0, 0)))


def _run_group(x, mod, pos0, st_ret, st_wkv, st_shift, st_conv, w):
    depth = mod.shape[0]
    new_ret, new_wkv, new_shift, new_conv = [], [], [], []
    for i in range(depth):
        m = [mod[i, :, n][:, None, :] for n in range(6)]
        j = i // 2
        if i % 2 == 0:
            x, s = _retention_layer(x, m[:3], pos0, st_ret[j], w["ret_w_in"][j], w["ret_w_out"][j],
                                    w["ret_gn_gain"][j])
            new_ret.append(s)
        else:
            x, s, last = _rwkv_layer(x, m[:3], st_shift[j], st_wkv[j], w["rwkv"][j])
            new_wkv.append(s)
            new_shift.append(last)
        x, cs = _ffn(x, m[3], m[4], m[5], w["ffn_w_gate"][i], w["ffn_w_up"][i], w["ffn_w_down"][i],
                     w["ffn_conv_w"][i], w["ffn_conv_b"][i], st_conv[i])
        new_conv.append(cs)
    out = _norm_gain(x, w["final_gain"])
    return out, jnp.stack(new_ret), jnp.stack(new_wkv), jnp.stack(new_shift), jnp.stack(new_conv)


def kernel(x_prompt, x_sample, c_prompt, c_sample, state_ret, state_rwkv_wkv, state_rwkv_shift, state_ffn_conv, ada_w, ada_b, ret_w_in, ret_w_out, ret_gn_gain, rwkv_mu, rwkv_w_r, rwkv_w_k, rwkv_w_v, rwkv_w_o, rwkv_w0, rwkv_w1, rwkv_w2, rwkv_a0, rwkv_a1, rwkv_a2, rwkv_g1, rwkv_g2, rwkv_k_k, rwkv_k_a, rwkv_r_k, rwkv_gn_gain, ffn_w_gate, ffn_w_up, ffn_conv_w, ffn_conv_b, ffn_w_down, final_gain):
    B, T, D = x_prompt.shape
    Bs = x_sample.shape[0]
    depth = ada_w.shape[0]
    n_rwkv = rwkv_mu.shape[0]

    rows = -(-(B + Bs) // SUBLANE) * SUBLANE
    c_all = jnp.pad(jnp.concatenate([c_prompt, c_sample], axis=0), ((0, rows - B - Bs), (0, 0)))
    mod = _ada(c_all, ada_w, ada_b).reshape(depth, rows, 6, D)

    bf = lambda a: a.astype(BF16)
    rwkv = []
    for j in range(n_rwkv):
        lw = -(-rwkv_w1.shape[2] // LANE) * LANE
        la = -(-rwkv_a1.shape[2] // LANE) * LANE
        par = jnp.pad(jnp.stack([rwkv_k_k[j], rwkv_k_a[j], rwkv_r_k[j].reshape(D), rwkv_gn_gain[j]]),
                      ((0, SUBLANE - 4), (0, 0)))
        rwkv.append(dict(
            mu=rwkv_mu[j], w_r=bf(rwkv_w_r[j]), w_k=bf(rwkv_w_k[j]), w_v=bf(rwkv_w_v[j]),
            w_o=bf(rwkv_w_o[j]),
            w1=bf(_pad_cols(rwkv_w1[j], lw)), w2=bf(_pad_rows(rwkv_w2[j], lw)),
            a1=bf(_pad_cols(rwkv_a1[j], la)), a2=bf(_pad_rows(rwkv_a2[j], la)),
            g1=bf(rwkv_g1[j]), g2=bf(rwkv_g2[j]), w0=rwkv_w0[j], a0=rwkv_a0[j], par=par))
    per_layer = lambda a: [bf(a[i]) for i in range(a.shape[0])]
    w = dict(ret_w_in=per_layer(ret_w_in), ret_w_out=per_layer(ret_w_out), ret_gn_gain=ret_gn_gain,
             rwkv=rwkv, ffn_w_gate=per_layer(ffn_w_gate), ffn_w_up=per_layer(ffn_w_up),
             ffn_w_down=per_layer(ffn_w_down), ffn_conv_w=ffn_conv_w, ffn_conv_b=ffn_conv_b,
             final_gain=final_gain)

    n_ret = state_ret.shape[0]
    F_ = ffn_w_gate.shape[2]
    z_ret = jnp.zeros((n_ret, B) + state_ret.shape[2:], F32)
    z_wkv = jnp.zeros((n_rwkv, B) + state_rwkv_wkv.shape[2:], F32)
    z_shift = jnp.zeros((n_rwkv, B, D), F32)
    z_conv = jnp.zeros((depth, B, CONV_W - 1, F_), F32)
    y_p, p_ret, p_wkv, p_shift, p_conv = _run_group(
        x_prompt, mod[:, :B], 0, z_ret, z_wkv, z_shift, z_conv, w)
    y_s, s_ret, s_wkv, s_shift, s_conv = _run_group(
        x_sample, mod[:, B:B + Bs], PAST_LEN, state_ret, state_rwkv_wkv, state_rwkv_shift,
        state_ffn_conv, w)
    return (y_p, y_s, p_ret, p_wkv, p_shift, p_conv, s_ret, s_wkv, s_shift, s_conv)
```

```python
import functools

import jax
import jax.numpy as jnp
from jax import lax
from jax.experimental import pallas as pl
from jax.experimental.pallas import tpu as pltpu

F32 = jnp.float32
BF16 = jnp.bfloat16

NORM_EPS = 1e-6
RET_GN_EPS = 1e-5
RWKV_GN_EPS = 64e-5
ROPE_BASE = 10000.0
PAST_LEN = 4096
CONV_W = 3

LANE = 128
SUBLANE = 8
VMEM_LIMIT_MB = 56

RET_CHUNK = 256
ROW_CHUNK = 256
RET_TBLOCK = 2048
RWKV_CHUNK = 64
RWKV_HEAD = 64
RWKV_CHUNKS_PER_STEP = 4
DECAY_SCALE_LOG2 = 0.6065306597126334 * 1.4426950408889634


def _pick(n, cands):
    for c in cands:
        if n % c == 0:
            return c
    return n


def _cparams(sem):
    return pltpu.CompilerParams(dimension_semantics=sem, vmem_limit_bytes=VMEM_LIMIT_MB << 20)


def _token_tiles(B, T):
    if T >= 512:
        return 1, _pick(T, (1024, 512, 256, 128))
    return B, T


def _silu(x):
    return x * jax.nn.sigmoid(x)


def _rows8(p):
    return jnp.broadcast_to(p[:, None, :], (p.shape[0], SUBLANE, p.shape[1]))


def _ada_body(c_ref, w_ref, b_ref, o_ref):
    c = c_ref[...]
    s = _silu(c).astype(BF16)
    o_ref[0] = jnp.dot(s, w_ref[0].astype(BF16), preferred_element_type=F32) + b_ref[0]


def _ada(c_all, ada_w, ada_b):
    depth, D, N = ada_w.shape
    R = c_all.shape[0]
    tn = _pick(N, (1024, 512, 256, 128))
    return pl.pallas_call(
        _ada_body,
        grid=(depth, N // tn),
        in_specs=[pl.BlockSpec((R, D), lambda l, j: (0, 0)),
                  pl.BlockSpec((1, D, tn), lambda l, j: (l, 0, j)),
                  pl.BlockSpec((1, 1, tn), lambda l, j: (l, 0, j))],
        out_specs=pl.BlockSpec((1, R, tn), lambda l, j: (l, 0, j)),
        out_shape=jax.ShapeDtypeStruct((depth, R, N), F32),
        compiler_params=_cparams(("parallel", "parallel")),
        name="ada_mod",
    )(c_all, ada_w, ada_b.reshape(depth, 1, N))


def _norm_gain_body(x_ref, g_ref, o_ref):
    x = x_ref[...]
    xn = x * lax.rsqrt(jnp.mean(x * x, -1, keepdims=True) + NORM_EPS)
    o_ref[...] = (xn * g_ref[...]).astype(o_ref.dtype)


def _norm_gain(x, gain):
    B, T, D = x.shape
    bb, tm = _token_tiles(B, T)
    xs = pl.BlockSpec((bb, tm, D), lambda b, t: (b, t, 0))
    return pl.pallas_call(
        _norm_gain_body, grid=(B // bb, T // tm),
        in_specs=[xs, pl.BlockSpec((1, 1, D), lambda b, t: (0, 0, 0))], out_specs=xs,
        out_shape=jax.ShapeDtypeStruct((B, T, D), F32),
        compiler_params=_cparams(("parallel", "parallel")), name="final_norm",
    )(x, gain.reshape(1, 1, D))


def _mm2_body(x_ref, w_ref, x2_ref, w2_ref, b2_ref, o_ref, o2_ref, *, post):
    bb, tm, K = x_ref.shape
    K2 = x2_ref.shape[2]
    rc = _pick(tm, (ROW_CHUNK,))
    for r0 in range(0, tm, rc):
        acc2 = jnp.dot(x2_ref[:, r0:r0 + rc, :].reshape(bb * rc, K2), w2_ref[...], preferred_element_type=F32)
        acc2 = acc2.reshape(bb, rc // SUBLANE, SUBLANE, -1)
        if post == "log2_decay":
            acc2 = -DECAY_SCALE_LOG2 * jax.nn.sigmoid(acc2 + b2_ref[...])
        elif post == "sigmoid":
            acc2 = jax.nn.sigmoid(acc2 + b2_ref[...])
        o2_ref[:, r0:r0 + rc, :] = acc2.reshape(bb, rc, -1).astype(o2_ref.dtype)
        acc = jnp.dot(x_ref[:, r0:r0 + rc, :].reshape(bb * rc, K), w_ref[...], preferred_element_type=F32)
        o_ref[:, r0:r0 + rc, :] = acc.reshape(bb, rc, -1).astype(o_ref.dtype)


def _mm2(x, w, x2, w2, bias2, post, out_dtype, out2_dtype, name="mm2"):
    B, T, K = x.shape
    K2 = x2.shape[2]
    N = w.shape[1]
    assert w2.shape[1] == N
    bb, tm = _token_tiles(B, T)
    tn = _pick(N, (1024, 512, 256, 128))
    os_ = pl.BlockSpec((bb, tm, tn), lambda b, t, j: (b, t, j))
    return pl.pallas_call(
        functools.partial(_mm2_body, post=post),
        grid=(B // bb, T // tm, N // tn),
        in_specs=[pl.BlockSpec((bb, tm, K), lambda b, t, j: (b, t, 0)),
                  pl.BlockSpec((K, tn), lambda b, t, j: (0, j)),
                  pl.BlockSpec((bb, tm, K2), lambda b, t, j: (b, t, 0)),
                  pl.BlockSpec((K2, tn), lambda b, t, j: (0, j)),
                  pl.BlockSpec((SUBLANE, tn), lambda b, t, j: (0, j))],
        out_specs=[os_, os_],
        out_shape=[jax.ShapeDtypeStruct((B, T, N), out_dtype), jax.ShapeDtypeStruct((B, T, N), out2_dtype)],
        compiler_params=_cparams(("parallel", "parallel", "parallel")), name=name,
    )(x, w, x2, w2, _rows8(bias2.reshape(1, N))[0])


def _mm_res_body(y_ref, w_ref, x_ref, gate_ref, o_ref):
    bb, tm, K = y_ref.shape
    acc = jnp.dot(y_ref[...].reshape(bb * tm, K), w_ref[...], preferred_element_type=F32)
    o_ref[...] = x_ref[...] + gate_ref[...] * acc.reshape(bb, tm, -1)


def _mm_res(y, w, x, gate, name="mm_res"):
    B, T, K = y.shape
    N = w.shape[1]
    bb, tm = _token_tiles(B, T)
    tn = _pick(N, (1024, 512, 256, 128) if K <= 2048 else (512, 256, 128))
    return pl.pallas_call(
        _mm_res_body,
        grid=(B // bb, T // tm, N // tn),
        in_specs=[pl.BlockSpec((bb, tm, K), lambda b, t, j: (b, t, 0)),
                  pl.BlockSpec((K, tn), lambda b, t, j: (0, j)),
                  pl.BlockSpec((bb, tm, tn), lambda b, t, j: (b, t, j)),
                  pl.BlockSpec((bb, 1, tn), lambda b, t, j: (b, 0, j))],
        out_specs=pl.BlockSpec((bb, tm, tn), lambda b, t, j: (b, t, j)),
        out_shape=jax.ShapeDtypeStruct((B, T, N), F32),
        compiler_params=_cparams(("parallel", "parallel", "parallel")), name=name,
    )(y, w, x, gate)


def _norm_mod_rows(x, shift, scale):
    xn = x * lax.rsqrt(jnp.mean(x * x, -1, keepdims=True) + NORM_EPS)
    return xn * (1.0 + scale) + shift


def _ret_proj_body(x_ref, sh_ref, sc_ref, w_ref, cos_ref, sin_ref, gain_ref, o_ref, h_ref,
                   *, n_rot, n_q, n_gate, dk):
    j = pl.program_id(2)
    bb, tm, K = x_ref.shape
    tn = w_ref.shape[1]

    def rotary_tile(first):
        sc = jnp.where(j >= n_q, dk ** -0.5, 1.0).astype(F32)
        half = dk // 2
        rc = _pick(tm, (ROW_CHUNK,))
        for r0 in range(0, tm, rc):
            if first:
                h = _norm_mod_rows(x_ref[:, r0:r0 + rc, :], sh_ref[...], sc_ref[...]).astype(BF16)
                h_ref[:, r0:r0 + rc, :] = h
            else:
                h = h_ref[:, r0:r0 + rc, :]
            acc = jnp.dot(h.reshape(bb * rc, K), w_ref[...],
                          preferred_element_type=F32).reshape(bb, rc, tn)
            cos = cos_ref[r0:r0 + rc, :][None] * sc
            sin = sin_ref[r0:r0 + rc, :][None] * sc
            for hh in range(tn // dk):
                lo = hh * dk
                x1 = acc[:, :, lo:lo + half]
                x2 = acc[:, :, lo + half:lo + dk]
                o_ref[:, r0:r0 + rc, lo:lo + half] = (x1 * cos - x2 * sin).astype(o_ref.dtype)
                o_ref[:, r0:r0 + rc, lo + half:lo + dk] = (x1 * sin + x2 * cos).astype(o_ref.dtype)

    pl.when(j == 0)(functools.partial(rotary_tile, True))
    pl.when((j > 0) & (j < n_rot))(functools.partial(rotary_tile, False))

    @pl.when((j >= n_rot) & (j < n_gate))
    def _():
        acc = jnp.dot(h_ref[...].reshape(bb * tm, K), w_ref[...], preferred_element_type=F32)
        o_ref[...] = acc.reshape(bb, tm, tn).astype(o_ref.dtype)

    @pl.when(j >= n_gate)
    def _():
        rc = _pick(tm, (ROW_CHUNK,))
        for r0 in range(0, tm, rc):
            acc = jnp.dot(h_ref[:, r0:r0 + rc, :].reshape(bb * rc, K), w_ref[...], preferred_element_type=F32)
            gated = _silu(acc).reshape(bb, rc // SUBLANE, SUBLANE, tn) * gain_ref[...]
            o_ref[:, r0:r0 + rc, :] = gated.reshape(bb, rc, tn).astype(o_ref.dtype)


def _ret_proj(x, shift, scale, w_in, cos, sin, gn_gain, qk_dim, dk):
    B, T, K = x.shape
    N = w_in.shape[1]
    bb, tm = _token_tiles(B, T)
    tn = _pick(qk_dim, (1024, 512, 256))
    half = dk // 2
    n_gate = (N - gn_gain.shape[0]) // tn
    assert gn_gain.shape[0] % tn == 0
    ms = pl.BlockSpec((bb, 1, K), lambda b, t, j: (b, 0, 0))
    return pl.pallas_call(
        functools.partial(_ret_proj_body, n_rot=2 * qk_dim // tn, n_q=qk_dim // tn, n_gate=n_gate, dk=dk),
        grid=(B // bb, T // tm, N // tn),
        in_specs=[pl.BlockSpec((bb, tm, K), lambda b, t, j: (b, t, 0)), ms, ms,
                  pl.BlockSpec((K, tn), lambda b, t, j: (0, j)),
                  pl.BlockSpec((tm, half), lambda b, t, j: (t, 0)),
                  pl.BlockSpec((tm, half), lambda b, t, j: (t, 0)),
                  pl.BlockSpec((SUBLANE, tn), lambda b, t, j: (0, jnp.maximum(j - n_gate, 0)))],
        out_specs=pl.BlockSpec((bb, tm, tn), lambda b, t, j: (b, t, j)),
        out_shape=jax.ShapeDtypeStruct((B, T, N), BF16),
        scratch_shapes=[pltpu.VMEM((bb, tm, K), BF16)],
        compiler_params=_cparams(("parallel", "parallel", "arbitrary")), name="ret_proj",
    )(x, shift, scale, w_in, cos, sin, _rows8(gn_gain.reshape(1, -1))[0])


def _ret_body(q_ref, k_ref, v_ref, g_ref, intra_ref, cross_ref, into_ref, cd_ref, s0_ref,
              y_ref, s_ref):
    c = pl.program_id(2)
    L = intra_ref.shape[1]

    @pl.when(c == 0)
    def _():
        s_ref[...] = s0_ref[...]

    chunks = [slice(ci * L, (ci + 1) * L) for ci in range(q_ref.shape[1] // L)]
    q = [q_ref[0, rows, :] for rows in chunks]
    v = [v_ref[0, rows, :] for rows in chunks]
    scores = [(lax.dot_general(q[i], k_ref[0, rows, :], (((1,), (1,)), ((), ())),
                               preferred_element_type=F32) * intra_ref[0]).astype(BF16)
              for i, rows in enumerate(chunks)]
    kv = [lax.dot_general((k_ref[0, rows, :].astype(F32) * into_ref[0]).astype(BF16), v[i],
                          (((0,), (0,)), ((), ())), preferred_element_type=F32)
          for i, rows in enumerate(chunks)]
    pv = [jnp.dot(scores[i], v[i], preferred_element_type=F32) for i in range(len(chunks))]
    for i, rows in enumerate(chunks):
        S = s_ref[0, 0]
        o = pv[i] + jnp.dot(q[i], S.astype(BF16), preferred_element_type=F32) * cross_ref[0]
        s_ref[0, 0] = S * cd_ref[0] + kv[i]
        mu = jnp.mean(o, -1, keepdims=True)
        d = o - mu
        var = jnp.mean(d * d, -1, keepdims=True)
        gate = g_ref[0, rows, :].astype(F32)
        y_ref[0, rows, :] = (d * lax.rsqrt(var + RET_GN_EPS) * gate).astype(y_ref.dtype)


def _retention(proj, state, H, dk, dv, n_valid):
    B, T, _ = proj.shape
    L = min(RET_CHUNK, T)
    nv = min(n_valid, L)
    log_g = jnp.log1p(-(2.0 ** (-5.0 - jnp.arange(H, dtype=F32))))
    idx = jnp.arange(L, dtype=F32)
    diff = idx[:, None] - idx[None, :]
    intra = jnp.where(diff >= 0, jnp.exp(log_g[:, None, None] * jnp.maximum(diff, 0.0)), 0.0)
    cross = jnp.exp(log_g[:, None] * (idx[None, :] + 1.0))[:, :, None]
    into = jnp.exp(log_g[:, None] * (nv - 1.0 - idx[None, :]))[:, :, None]
    cd = jnp.exp(log_g * nv)[:, None, None]
    tb = _pick(T, (RET_TBLOCK, L))
    kq = (H * dk) // dk
    vq = (2 * H * dk) // dv
    return pl.pallas_call(
        _ret_body,
        grid=(B, H, T // tb),
        in_specs=[pl.BlockSpec((1, tb, dk), lambda b, h, c: (b, c, h)),
                  pl.BlockSpec((1, tb, dk), lambda b, h, c: (b, c, kq + h)),
                  pl.BlockSpec((1, tb, dv), lambda b, h, c: (b, c, vq + h)),
                  pl.BlockSpec((1, tb, dv), lambda b, h, c: (b, c, vq + H + h)),
                  pl.BlockSpec((1, L, L), lambda b, h, c: (h, 0, 0)),
                  pl.BlockSpec((1, L, 1), lambda b, h, c: (h, 0, 0)),
                  pl.BlockSpec((1, L, 1), lambda b, h, c: (h, 0, 0)),
                  pl.BlockSpec((1, 1, 1), lambda b, h, c: (h, 0, 0)),
                  pl.BlockSpec((1, 1, dk, dv), lambda b, h, c: (b, h, 0, 0))],
        out_specs=[pl.BlockSpec((1, tb, dv), lambda b, h, c: (b, c, h)),
                   pl.BlockSpec((1, 1, dk, dv), lambda b, h, c: (b, h, 0, 0))],
        out_shape=[jax.ShapeDtypeStruct((B, T, H * dv), BF16),
                   jax.ShapeDtypeStruct((B, H, dk, dv), F32)],
        compiler_params=_cparams(("parallel", "parallel", "arbitrary")), name="retention",
    )(proj, proj, proj, proj, intra, cross, into, cd, state)


def _ffn_up_body(x_ref, sh_ref, sc_ref, wg_ref, wu_ref, cw_ref, cb_ref, cs_ref, o_ref, tail_ref,
                 ext_ref, h_ref):
    t = pl.program_id(1)
    f = pl.program_id(2)
    bb, tm, D = x_ref.shape
    tf = wg_ref.shape[1]
    pad = SUBLANE

    @pl.when(t == 0)
    def _():
        tail_ref[f] = cs_ref[...]

    ext_ref[:, pad - 2:pad, :] = tail_ref[f]

    def tiles(a):
        return a.reshape(bb, a.shape[1] // SUBLANE, SUBLANE, tf)

    def column_tile(first):
        cw = cw_ref[...]
        rc = _pick(tm, (ROW_CHUNK,))
        for r0 in range(0, tm, rc):
            if first:
                h = _norm_mod_rows(x_ref[:, r0:r0 + rc, :], sh_ref[...], sc_ref[...]).astype(BF16)
                h_ref[:, r0:r0 + rc, :] = h
            else:
                h = h_ref[:, r0:r0 + rc, :]
            h = h.reshape(bb * rc, D)
            u = jnp.dot(h, wg_ref[...], preferred_element_type=F32).reshape(bb, rc, tf)
            up = jnp.dot(h, wu_ref[...], preferred_element_type=F32).reshape(bb, rc, tf)
            ext_ref[:, pad + r0:pad + r0 + rc, :] = u
            conv = (cb_ref[...] + tiles(ext_ref[:, pad - 2 + r0:pad - 2 + r0 + rc, :]) * cw[0]
                    + tiles(ext_ref[:, pad - 1 + r0:pad - 1 + r0 + rc, :]) * cw[1] + tiles(u) * cw[2])
            o_ref[:, r0:r0 + rc, :] = (_silu(conv) * tiles(up)).reshape(bb, rc, tf).astype(o_ref.dtype)

    pl.when(f == 0)(functools.partial(column_tile, True))
    pl.when(f > 0)(functools.partial(column_tile, False))
    tail_ref[f] = ext_ref[:, pad + tm - 2:pad + tm, :]


def _ffn(x, shift, scale, gate, wg, wu, wd, conv_w, conv_b, conv_state):
    B, T, D = x.shape
    Fd = wg.shape[1]
    bb, tm = _token_tiles(B, T)
    tf = _pick(Fd, (512, 256, 128))
    nf = Fd // tf
    ms = pl.BlockSpec((bb, 1, D), lambda b, t, f: (b, 0, 0))
    act, tails = pl.pallas_call(
        _ffn_up_body,
        grid=(B // bb, T // tm, nf),
        in_specs=[pl.BlockSpec((bb, tm, D), lambda b, t, f: (b, t, 0)), ms, ms,
                  pl.BlockSpec((D, tf), lambda b, t, f: (0, f)),
                  pl.BlockSpec((D, tf), lambda b, t, f: (0, f)),
                  pl.BlockSpec((CONV_W, SUBLANE, tf), lambda b, t, f: (0, 0, f)),
                  pl.BlockSpec((SUBLANE, tf), lambda b, t, f: (0, f)),
                  pl.BlockSpec((bb, CONV_W - 1, tf), lambda b, t, f: (b, 0, f))],
        out_specs=[pl.BlockSpec((bb, tm, tf), lambda b, t, f: (b, t, f)),
                   pl.BlockSpec((nf, bb, CONV_W - 1, tf), lambda b, t, f: (0, b, 0, 0))],
        out_shape=[jax.ShapeDtypeStruct((B, T, Fd), BF16),
                   jax.ShapeDtypeStruct((nf, B, CONV_W - 1, tf), F32)],
        scratch_shapes=[pltpu.VMEM((bb, tm + SUBLANE, tf), F32), pltpu.VMEM((bb, tm, D), BF16)],
        compiler_params=_cparams(("parallel", "arbitrary", "arbitrary")), name="ffn_up",
    )(x, shift, scale, wg, wu, _rows8(conv_w), _rows8(conv_b.reshape(1, Fd))[0], conv_state)
    y = _mm_res(act, wd, x, gate, name="ffn_down")
    return y, tails.transpose(1, 2, 0, 3).reshape(B, CONV_W - 1, Fd)


def _rwkv_mix_body(x_ref, sh_ref, sc_ref, mu_ref, st_ref, w1_ref, a1_ref, g1_ref,
                   xr_ref, xk_ref, xv_ref, tw_ref, ta_ref, tg_ref, last_ref, hbuf_ref, carry_ref):
    t = pl.program_id(1)
    tm = x_ref.shape[1]
    pad = SUBLANE
    x = x_ref[0]
    xn = x * lax.rsqrt(jnp.mean(x * x, -1, keepdims=True) + NORM_EPS)
    h = xn * (1.0 + sc_ref[0]) + sh_ref[0]

    @pl.when(t == 0)
    def _():
        carry_ref[...] = st_ref[0]

    hbuf_ref[pad - 1:pad, :] = carry_ref[...]
    hbuf_ref[pad:, :] = h
    xx = hbuf_ref[pad - 1:pad - 1 + tm, :] - h
    D = h.shape[1]
    h3 = h.reshape(tm // SUBLANE, SUBLANE, D)
    xx3 = xx.reshape(tm // SUBLANE, SUBLANE, D)

    def mix(n):
        return (h3 + xx3 * mu_ref[n][None]).reshape(tm, D).astype(BF16)

    xr_ref[0] = mix(0)
    xk_ref[0] = mix(2)
    xv_ref[0] = mix(3)
    tw_ref[0] = jnp.tanh(jnp.dot(mix(1), w1_ref[...], preferred_element_type=F32)).astype(tw_ref.dtype)
    ta_ref[0] = jnp.dot(mix(4), a1_ref[...], preferred_element_type=F32).astype(ta_ref.dtype)
    tg_ref[0] = jax.nn.sigmoid(jnp.dot(mix(5), g1_ref[...], preferred_element_type=F32)).astype(tg_ref.dtype)
    last = hbuf_ref[pad + tm - 1:pad + tm, :]
    carry_ref[...] = last
    last_ref[0] = last


def _rwkv_mix(x, shift, scale, mu, shift_state, w1, a1, g1):
    B, T, D = x.shape
    tm = _pick(T, (512, 256, 128))
    xs = pl.BlockSpec((1, tm, D), lambda b, t: (b, t, 0))
    ms = pl.BlockSpec((1, 1, D), lambda b, t: (b, 0, 0))
    loras = (w1, a1, g1)
    outs = pl.pallas_call(
        _rwkv_mix_body, grid=(B, T // tm),
        in_specs=[xs, ms, ms, pl.BlockSpec((6, SUBLANE, D), lambda b, t: (0, 0, 0)), ms]
        + [pl.BlockSpec(w.shape, lambda b, t: (0, 0)) for w in loras],
        out_specs=[xs] * 3 + [pl.BlockSpec((1, tm, w.shape[1]), lambda b, t: (b, t, 0)) for w in loras] + [ms],
        out_shape=[jax.ShapeDtypeStruct((B, T, D), BF16)] * 3
        + [jax.ShapeDtypeStruct((B, T, w.shape[1]), BF16) for w in loras]
        + [jax.ShapeDtypeStruct((B, 1, D), F32)],
        scratch_shapes=[pltpu.VMEM((tm + SUBLANE, D), F32), pltpu.VMEM((1, D), F32)],
        compiler_params=_cparams(("parallel", "arbitrary")), name="rwkv_mix",
    )(x, shift, scale, _rows8(mu), shift_state.reshape(B, 1, D), *loras)
    return outs[:6], outs[6].reshape(B, D)


def _dot(a, b):
    return jnp.dot(a.astype(BF16), b.astype(BF16), preferred_element_type=F32)


def _dot_nt(a, b):
    return lax.dot_general(a.astype(BF16), b.astype(BF16), (((1,), (1,)), ((), ())),
                           preferred_element_type=F32)


def _dot_tn(a, b):
    return lax.dot_general(a.astype(BF16), b.astype(BF16), (((0,), (0,)), ((), ())),
                           preferred_element_type=F32)


def _rwkv_rec_body(r_ref, k_ref, v_ref, ld_ref, a_ref, g_ref, par_ref, s0_ref, y_ref, s_ref,
                   *, L, npair, n_valid):
    c = pl.program_id(1)
    N = RWKV_HEAD
    P = 2 * N

    lane_l = lax.broadcasted_iota(jnp.int32, (L, P), 1)
    head0_l = lane_l < N
    r2 = lax.broadcasted_iota(jnp.int32, (2 * L, 2 * L), 0)
    c2 = lax.broadcasted_iota(jnp.int32, (2 * L, 2 * L), 1)
    same = (r2 >= L) == (c2 >= L)
    strict = same & (c2 < r2)
    incl = same & (c2 <= r2)
    rowhead = (lax.broadcasted_iota(jnp.int32, (2 * L, P), 0) >= L) == \
              (lax.broadcasted_iota(jnp.int32, (2 * L, P), 1) >= N)
    ip = lax.broadcasted_iota(jnp.int32, (P, P), 0)
    jp = lax.broadcasted_iota(jnp.int32, (P, P), 1)
    blockdiag = (ip >= N) == (jp >= N)
    seg = blockdiag.astype(BF16)
    live = lax.broadcasted_iota(jnp.int32, (L, P), 0) < n_valid

    def segsum(x):
        return jnp.dot(x.astype(BF16), seg, preferred_element_type=F32)

    row_l = lax.broadcasted_iota(jnp.int32, (L, P), 0)

    def cumsum_rows(x):
        s = 1
        while s < L:
            if s < SUBLANE:
                x = x + jnp.where(row_l >= s, pltpu.roll(x, s, 0), 0.0)
            else:
                x = x + jnp.concatenate([jnp.zeros((s, P), F32), x[:L - s]], axis=0)
            s *= 2
        return x

    def stack_heads(x):
        return jnp.concatenate([jnp.where(head0_l, x, 0.0), jnp.where(head0_l, 0.0, x)], axis=0)

    def fold_heads(x2):
        return x2[:L] + x2[L:]

    def twice(x):
        return jnp.concatenate([x, x], axis=0)

    @pl.when(c == 0)
    def _():
        s_ref[...] = s0_ref[...]

    def segsum_all(xs, per=4):
        outs = []
        for j in range(0, len(xs), per):
            part = xs[j:j + per]
            out = segsum(jnp.concatenate(part, axis=0))
            outs += [out[i * L:(i + 1) * L] for i in range(len(part))]
        return outs

    n = npair
    lanes = [slice(p * P, (p + 1) * P) for p in range(n)]
    par = [par_ref[:, ln] for ln in lanes]

    def gram_gen(ci, cx):
        rows = slice(ci * L, (ci + 1) * L)
        r = [r_ref[0, rows, ln] for ln in lanes]
        k = [k_ref[0, rows, ln] for ln in lanes]
        v = [v_ref[0, rows, ln] for ln in lanes]
        lnd = [ld_ref[0, rows, ln] for ln in lanes]
        a = [a_ref[0, rows, ln] for ln in lanes]
        kkraw, kmod, sums = [], [], []
        per = 4
        for i0 in range(0, n, per):
            for i in range(i0, min(i0 + per, n)):
                k_k, k_a = par[i][0:1], par[i][1:2]
                kkraw.append(k[i] * k_k)
                kmod.append(k[i] * ((1.0 - k_a) + a[i] * k_a))
            sums += segsum_all([x * x for x in kkraw[i0:i0 + per]], per)
            yield
        cum, beta, kappa, rho, a2, gb, gk = [], [], [], [], [], [], []
        for i in range(n):
            kk = kkraw[i] * lax.rsqrt(jnp.maximum(sums[i], 1e-24))
            if n_valid < L:
                lnd[i] = jnp.where(live, lnd[i], 0.0)
                kk = jnp.where(live, kk, 0.0)
                kmod[i] = jnp.where(live, kmod[i], 0.0)
                v[i] = jnp.where(live, v[i], 0.0)
            cum.append(cumsum_rows(lnd[i]))
            e_neg = jnp.exp2(-cum[i])
            beta.append((kk * a[i] * e_neg).astype(BF16))
            kappa.append((kmod[i] * e_neg).astype(BF16))
            rho.append(r[i] * jnp.exp2(cum[i]))
            a2.append(stack_heads(kk * jnp.exp2(cum[i] - lnd[i])))
            lhs = jnp.concatenate([a2[i], stack_heads(rho[i])], axis=0).astype(BF16)
            gb.append(_dot_nt(lhs, twice(beta[i])))
            yield
            gk.append(_dot_nt(lhs, twice(kappa[i])))
            yield
        akv, pkv, rk = [], [], []
        for i in range(n):
            kv = _dot(jnp.concatenate([jnp.where(strict, gk[i][:2 * L], 0.0),
                                       jnp.where(incl, gk[i][2 * L:], 0.0)], axis=0), twice(v[i]))
            akv.append(jnp.where(rowhead, kv[:2 * L], 0.0))
            pkv.append(kv[2 * L:])
            yield
        for i0 in range(0, n, per):
            rk += segsum_all([r[i] * kmod[i] * par[i][2:3] for i in range(i0, min(i0 + per, n))], per)
            yield
        cx.update(rows=rows, v=v, cum=cum, beta=beta, kappa=kappa, rho=rho, a2=a2, akv=akv, pkv=pkv, rk=rk,
                  nmat=[jnp.where(strict, x[:2 * L], 0.0) for x in gb],
                  pb=[jnp.where(incl, x[2 * L:], 0.0).astype(BF16) for x in gb])

    def stage_inverse(cx, filler=None):
        nsq = max(L.bit_length() - 2, 0)

        def mm(x, w):
            out = _dot(x, w)
            if filler is not None:
                next(filler, None)
            return out

        corr = [-x for x in cx["nmat"]]
        pw = [x.astype(BF16) for x in cx["nmat"]]
        pw = [mm(x, x) for x in pw]
        for it in range(nsq):
            pwb = [x.astype(BF16) for x in pw]
            if it < nsq - 1:
                res = [mm(jnp.concatenate([pwb[i], corr[i].astype(BF16)], axis=0), pwb[i]) for i in range(n)]
                corr = [corr[i] + pw[i] + res[i][2 * L:] for i in range(n)]
                pw = [x[:2 * L] for x in res]
            else:
                corr = [corr[i] + pw[i] + mm(corr[i], pwb[i]) for i in range(n)]
        cx["corr"] = corr

    def stage_apply(cx):
        v, beta, kappa = cx["v"], cx["beta"], cx["kappa"]
        both = [jnp.concatenate([cx["a2"][i], cx["akv"][i]], axis=1) for i in range(n)]
        both = [both[i] + _dot(cx["corr"][i], both[i]) for i in range(n)]
        at = [fold_heads(x[:, :P]) for x in both]
        w0_ = [fold_heads(x[:, P:]) for x in both]
        pbx = [_dot(cx["pb"][i], twice(jnp.concatenate([at[i], w0_[i]], axis=1))) for i in range(n)]
        cx["rho_t"] = [cx["rho"][i] - fold_heads(jnp.where(rowhead, pbx[i][:, :P], 0.0)) for i in range(n)]
        cx["o0"] = [fold_heads(jnp.where(rowhead, cx["pkv"][i] - pbx[i][:, P:], 0.0)) for i in range(n)]
        cx["gmat"] = [jnp.where(blockdiag, _dot_tn(at[i], beta[i]), 0.0) for i in range(n)]
        cx["umat"] = [jnp.where(blockdiag, _dot_tn(jnp.concatenate([v[i], -w0_[i]], axis=0),
                                                   jnp.concatenate([kappa[i], beta[i]], axis=0)), 0.0)
                      for i in range(n)]

    def stage_state(cx):
        S = [s_ref[0, p] for p in range(n)]
        Sb = [x.astype(BF16) for x in S]
        cx["o"] = [_dot_nt(cx["rho_t"][i], Sb[i]) + cx["o0"][i] for i in range(n)]
        for i in range(n):
            s_ref[0, i] = ((S[i] - _dot(Sb[i], cx["gmat"][i]) + cx["umat"][i])
                           * jnp.exp2(cx["cum"][i][L - 1:L, :]))

    def stage_mean(cx):
        mu = segsum_all(cx["o"])
        cx["d"] = [cx["o"][i] - mu[i] * (1.0 / N) for i in range(n)]

    def stage_out(cx):
        d, rows = cx["d"], cx["rows"]
        var = segsum_all([x * x for x in d])
        for i in range(n):
            yn = d[i] * lax.rsqrt(var[i] * (1.0 / N) + RWKV_GN_EPS) * par[i][3:4]
            y_ref[0, rows, lanes[i]] = ((yn + cx["rk"][i] * cx["v"][i])
                                        * g_ref[0, rows, lanes[i]]).astype(y_ref.dtype)

    nchunk = r_ref.shape[1] // L
    cxs = [dict() for _ in range(nchunk)]
    for _ in gram_gen(0, cxs[0]):
        pass
    for ci in range(nchunk):
        cx = cxs[ci]
        filler = gram_gen(ci + 1, cxs[ci + 1]) if ci + 1 < nchunk else None
        if ci:
            stage_mean(cxs[ci - 1])
        stage_inverse(cx, filler)
        if filler is not None:
            for _ in filler:
                pass
        if ci:
            stage_out(cxs[ci - 1])
        stage_apply(cx)
        stage_state(cx)
    stage_mean(cxs[-1])
    stage_out(cxs[-1])


def _rwkv_rec(r, k, v, wl, al, g, par, s0_blk, n_valid):
    B, T, D = r.shape
    P = 2 * RWKV_HEAD
    L = RWKV_CHUNK
    assert T % L == 0 and (n_valid == T or T == L)
    npair = D // P
    tb = _pick(T, (RWKV_CHUNKS_PER_STEP * L, L))
    ts = pl.BlockSpec((1, tb, D), lambda b, c: (b, c, 0))
    ss = pl.BlockSpec((1, npair, P, P), lambda b, c: (b, 0, 0, 0))
    return pl.pallas_call(
        functools.partial(_rwkv_rec_body, L=L, npair=npair, n_valid=min(n_valid, L)),
        grid=(B, T // tb),
        in_specs=[ts] * 6 + [pl.BlockSpec((SUBLANE, D), lambda b, c: (0, 0)), ss],
        out_specs=[ts, ss],
        out_shape=[jax.ShapeDtypeStruct((B, T, D), BF16),
                   jax.ShapeDtypeStruct((B, npair, P, P), F32)],
        compiler_params=_cparams(("parallel", "arbitrary")), name="rwkv_rec",
    )(r, k, v, wl, al, g, par, s0_blk)


def _pair_blockdiag(s):
    B, H, N, _ = s.shape
    s = s.reshape(B, H // 2, 2, N, N)
    z = jnp.zeros_like(s[:, :, 0])
    top = jnp.concatenate([s[:, :, 0], z], axis=-1)
    bot = jnp.concatenate([z, s[:, :, 1]], axis=-1)
    return jnp.concatenate([top, bot], axis=-2)


def _pair_unblock(sb):
    B, Pn, P, _ = sb.shape
    N = P // 2
    return jnp.stack([sb[:, :, :N, :N], sb[:, :, N:, N:]], axis=2).reshape(B, 2 * Pn, N, N)


def _pad_time(a, T2):
    if T2 == a.shape[1]:
        return a
    return jnp.pad(a, ((0, 0), (0, T2 - a.shape[1]), (0, 0)))


def _retention_layer(x, mods, pos0, state, w_in, w_out, gn_gain):
    B, T, D = x.shape
    H, dk, dv = state.shape[1:]
    shift_m, scale_m, gate_m = mods
    half = dk // 2
    inv = ROPE_BASE ** (-jnp.arange(half, dtype=F32) / half)
    ang = (pos0 + jnp.arange(T)).astype(F32)[:, None] * inv[None, :]
    proj = _ret_proj(x, shift_m, scale_m, w_in, jnp.cos(ang), jnp.sin(ang), gn_gain, H * dk, dk)
    Tp = T if T % RET_CHUNK == 0 else -(-T // 64) * 64
    assert Tp == T or Tp <= RET_CHUNK
    y, s_new = _retention(_pad_time(proj, Tp), state, H, dk, dv, T)
    x = _mm_res(y[:, :T], w_out, x, gate_m, name="ret_out")
    return x, s_new


def _rwkv_layer(x, mods, shift_state, wkv_state, p):
    B, T, D = x.shape
    shift_m, scale_m, gate_m = mods
    (xr, xk, xv, tw, ta, tg), last = _rwkv_mix(x, shift_m, scale_m, p["mu"], shift_state,
                                               p["w1"], p["a1"], p["g1"])
    zero = jnp.zeros((D,), F32)
    r, ld = _mm2(xr, p["w_r"], tw, p["w2"], p["w0"], "log2_decay", F32, F32, name="rwkv_r_w")
    k, a = _mm2(xk, p["w_k"], ta, p["a2"], p["a0"], "sigmoid", F32, F32, name="rwkv_k_a")
    v, g = _mm2(xv, p["w_v"], tg, p["g2"], zero, None, F32, BF16, name="rwkv_v_g")
    Tp = -(-T // RWKV_CHUNK) * RWKV_CHUNK
    ins = [_pad_time(t, Tp) for t in (r, k, v, ld, a, g)]
    y, s_blk = _rwkv_rec(*ins, p["par"], _pair_blockdiag(wkv_state), T)
    x = _mm_res(y[:, :T], p["w_o"], x, gate_m, name="rwkv_out")
    return x, _pair_unblock(s_blk), last


def _pad_cols(w, n):
    return jnp.pad(w, ((0, 0), (0, n - w.shape[1])))


def _pad_rows(w, n):
    return jnp.pad(w, ((0, n - w.shape[0]), (0, 0)))


def _run_group(x, mod, pos0, st_ret, st_wkv, st_shift, st_conv, w):
    depth = mod.shape[0]
    new_ret, new_wkv, new_shift, new_conv = [], [], [], []
    for i in range(depth):
        m = [mod[i, :, n][:, None, :] for n in range(6)]
        j = i // 2
        if i % 2 == 0:
            x, s = _retention_layer(x, m[:3], pos0, st_ret[j], w["ret_w_in"][j], w["ret_w_out"][j],
                                    w["ret_gn_gain"][j])
            new_ret.append(s)
        else:
            x, s, last = _rwkv_layer(x, m[:3], st_shift[j], st_wkv[j], w["rwkv"][j])
            new_wkv.append(s)
            new_shift.append(last)
        x, cs = _ffn(x, m[3], m[4], m[5], w["ffn_w_gate"][i], w["ffn_w_up"][i], w["ffn_w_down"][i],
                     w["ffn_conv_w"][i], w["ffn_conv_b"][i], st_conv[i])
        new_conv.append(cs)
    out = _norm_gain(x, w["final_gain"])
    return out, jnp.stack(new_ret), jnp.stack(new_wkv), jnp.stack(new_shift), jnp.stack(new_conv)


def kernel(x_prompt, x_sample, c_prompt, c_sample, state_ret, state_rwkv_wkv, state_rwkv_shift, state_ffn_conv, ada_w, ada_b, ret_w_in, ret_w_out, ret_gn_gain, rwkv_mu, rwkv_w_r, rwkv_w_k, rwkv_w_v, rwkv_w_o, rwkv_w0, rwkv_w1, rwkv_w2, rwkv_a0, rwkv_a1, rwkv_a2, rwkv_g1, rwkv_g2, rwkv_k_k, rwkv_k_a, rwkv_r_k, rwkv_gn_gain, ffn_w_gate, ffn_w_up, ffn_conv_w, ffn_conv_b, ffn_w_down, final_gain):
    B, T, D = x_prompt.shape
    Bs = x_sample.shape[0]
    depth = ada_w.shape[0]
    n_rwkv = rwkv_mu.shape[0]

    rows = -(-(B + Bs) // SUBLANE) * SUBLANE
    c_all = jnp.pad(jnp.concatenate([c_prompt, c_sample], axis=0), ((0, rows - B - Bs), (0, 0)))
    mod = _ada(c_all, ada_w, ada_b).reshape(depth, rows, 6, D)

    bf = lambda a: a.astype(BF16)
    rwkv = []
    for j in range(n_rwkv):
        lw = -(-rwkv_w1.shape[2] // LANE) * LANE
        la = -(-rwkv_a1.shape[2] // LANE) * LANE
        par = jnp.pad(jnp.stack([rwkv_k_k[j], rwkv_k_a[j], rwkv_r_k[j].reshape(D), rwkv_gn_gain[j]]),
                      ((0, SUBLANE - 4), (0, 0)))
        rwkv.append(dict(
            mu=rwkv_mu[j], w_r=bf(rwkv_w_r[j]), w_k=bf(rwkv_w_k[j]), w_v=bf(rwkv_w_v[j]),
            w_o=bf(rwkv_w_o[j]),
            w1=bf(_pad_cols(rwkv_w1[j], lw)), w2=bf(_pad_rows(rwkv_w2[j], lw)),
            a1=bf(_pad_cols(rwkv_a1[j], la)), a2=bf(_pad_rows(rwkv_a2[j], la)),
            g1=bf(rwkv_g1[j]), g2=bf(rwkv_g2[j]), w0=rwkv_w0[j], a0=rwkv_a0[j], par=par))
    per_layer = lambda a: [bf(a[i]) for i in range(a.shape[0])]
    w = dict(ret_w_in=per_layer(ret_w_in), ret_w_out=per_layer(ret_w_out), ret_gn_gain=ret_gn_gain,
             rwkv=rwkv, ffn_w_gate=per_layer(ffn_w_gate), ffn_w_up=per_layer(ffn_w_up),
             ffn_w_down=per_layer(ffn_w_down), ffn_conv_w=ffn_conv_w, ffn_conv_b=ffn_conv_b,
             final_gain=final_gain)

    n_ret = state_ret.shape[0]
    F_ = ffn_w_gate.shape[2]
    z_ret = jnp.zeros((n_ret, B) + state_ret.shape[2:], F32)
    z_wkv = jnp.zeros((n_rwkv, B) + state_rwkv_wkv.shape[2:], F32)
    z_shift = jnp.zeros((n_rwkv, B, D), F32)
    z_conv = jnp.zeros((depth, B, CONV_W - 1, F_), F32)
    y_p, p_ret, p_wkv, p_shift, p_conv = _run_group(
        x_prompt, mod[:, :B], 0, z_ret, z_wkv, z_shift, z_conv, w)
    y_s, s_ret, s_wkv, s_shift, s_conv = _run_group(
        x_sample, mod[:, B:B + Bs], PAST_LEN, state_ret, state_rwkv_wkv, state_rwkv_shift,
        state_ffn_conv, w)
    return (y_p, y_s, p_ret, p_wkv, p_shift, p_conv, s_ret, s_wkv, s_shift, s_conv)
```
